```python
import functools
import jax
import jax.numpy as jnp
from jax import lax
import numpy as np

D_MODEL = 1024
BATCH = 4
SEQ = 4096
DEPTH = 4

GRID_W = 64
CTX_LEN = 256
HEAD_DIM = 64
N_EVEN = (DEPTH + 1) // 2
N_ODD = DEPTH // 2

NA_HEADS = 8
NA_WIN_H_MAX = 8
NA_WIN_W = 16
NA_WIDTH = NA_HEADS * HEAD_DIM
GLA_HEADS = 4
GLA_DK = 64
GLA_DV = 128
GLA_KW = GLA_HEADS * GLA_DK
GLA_VW = GLA_HEADS * GLA_DV
GLA_GATE_RANK = 16
GLA_TAU = 16.0
GLA_CHUNK = 64
AB_SIZES = (NA_WIDTH, NA_WIDTH, NA_WIDTH, GLA_KW, GLA_KW, GLA_VW, GLA_VW, GLA_GATE_RANK, GLA_GATE_RANK)
IN_AB = 3 * NA_WIDTH + 2 * GLA_KW + 2 * GLA_VW + 2 * GLA_GATE_RANK
SWA_HEADS = 16
SWA_KV_HEADS = 4
SWA_GROUP = SWA_HEADS // SWA_KV_HEADS
SWA_WINDOW = 128
SWA_BLOCK = 128
IN_C = (SWA_HEADS + 2 * SWA_KV_HEADS) * HEAD_DIM
D_FF = 2816
CONV_W = 3
ROPE_BASE = 10000.0
ROPE_AXIS_DIM = HEAD_DIM // 2
EPS = 1e-6
NEG_INF = -1e30

kernel_name = 'hybrid_na_gla_swa_prefix_dit_trunk'


def rmsnorm(x, g):
    xf = x.astype(jnp.float32)
    y = xf * lax.rsqrt(jnp.mean(xf * xf, axis=-1, keepdims=True) + EPS)
    return (y * g.astype(jnp.float32)).astype(x.dtype)


def modulate(h, shift, scale):
    return h * (1 + scale) + shift


def axial_rope_tables(n):
    t = jnp.arange(n)
    row = (t // GRID_W).astype(jnp.float32)
    col = (t % GRID_W).astype(jnp.float32)
    inv = 1.0 / (ROPE_BASE ** (jnp.arange(0, ROPE_AXIS_DIM, 2, dtype=jnp.float32) / ROPE_AXIS_DIM))
    ang = jnp.concatenate([row[:, None] * inv, col[:, None] * inv], axis=-1)
    return jnp.cos(ang), jnp.sin(ang)


def apply_rope(x, cos, sin):
    shp = (1, cos.shape[0]) + (1,) * (x.ndim - 3) + (cos.shape[1],)
    cos, sin = cos.reshape(shp), sin.reshape(shp)
    x1, x2 = x[..., 0::2], x[..., 1::2]
    out = jnp.stack([x1 * cos - x2 * sin, x1 * sin + x2 * cos], axis=-1)
    return out.reshape(x.shape).astype(x.dtype)


def joint_softmax(parts, sink=None):
    m = functools.reduce(jnp.maximum, [jnp.max(s, axis=-1, keepdims=True) for s in parts])
    if sink is not None:
        m = jnp.maximum(m, sink)
    ps = [jnp.exp(s - m) for s in parts]
    den = functools.reduce(jnp.add, [jnp.sum(p, axis=-1, keepdims=True) for p in ps])
    if sink is not None:
        den = den + jnp.exp(sink - m)
    return [p / den for p in ps]


def context_attention(q, k, v):
    B, n, H, d = q.shape
    s = jnp.einsum('bqhd,bkhd->bhqk', q, k).astype(jnp.float32) * d ** -0.5
    p = jax.nn.softmax(s, axis=-1).astype(v.dtype)
    return jnp.einsum('bhqk,bkhd->bqhd', p, v).reshape(B, n, H * d)


def neighbourhood_attention(q, k, v, k_ctx, v_ctx, rpb):
    B, S, H, d = q.shape
    rows = S // GRID_W
    kh = min(NA_WIN_H_MAX, rows)
    r = jnp.arange(rows)
    r_start = jnp.clip(r - kh // 2, 0, rows - kh)
    row_idx = r_start[:, None] + jnp.arange(kh)[None, :]
    cidx = jnp.arange(GRID_W)
    c_start = jnp.clip(cidx - NA_WIN_W // 2, 0, GRID_W - NA_WIN_W)
    col_ok = (cidx[None, :] >= c_start[:, None]) & (cidx[None, :] < c_start[:, None] + NA_WIN_W)
    dr = row_idx - r[:, None] + (NA_WIN_H_MAX - 1)
    dc = jnp.clip(cidx[None, :] - cidx[:, None] + NA_WIN_W - 1, 0, 2 * NA_WIN_W - 2)
    bias = rpb[:, dr[:, None, :, None], dc[None, :, None, :]].astype(jnp.float32)
    bias = jnp.where(col_ok[None, None, :, None, :], bias, NEG_INF).reshape(H, rows, GRID_W, kh * GRID_W)
    qg = q.reshape(B, rows, GRID_W, H, d)
    kg = jnp.take(k.reshape(B, rows, GRID_W, H, d), row_idx, axis=1).reshape(B, rows, kh * GRID_W, H, d)
    vg = jnp.take(v.reshape(B, rows, GRID_W, H, d), row_idx, axis=1).reshape(B, rows, kh * GRID_W, H, d)
    scale = d ** -0.5
    s_lat = jnp.einsum('brqhd,brkhd->bhrqk', qg, kg).astype(jnp.float32) * scale + bias[None]
    s_ctx = jnp.einsum('brqhd,bchd->bhrqc', qg, k_ctx).astype(jnp.float32) * scale
    p_lat, p_ctx = joint_softmax([s_lat, s_ctx])
    o = (jnp.einsum('bhrqk,brkhd->brqhd', p_lat.astype(v.dtype), vg)
         + jnp.einsum('bhrqc,bchd->brqhd', p_ctx.astype(v.dtype), v_ctx))
    return o.reshape(B, S, H * d)


def gla_log_decay(z_low, w2, b):
    z = z_low.astype(jnp.float32) @ w2.astype(jnp.float32) + b.astype(jnp.float32)
    return jax.nn.log_sigmoid(z) / GLA_TAU


def gla_chunked(q, k, v, log_a, s0, with_output):
    B, T, H, dk = k.shape
    dv = v.shape[-1]
    n = T // GLA_CHUNK
    blk = lambda t: t.astype(jnp.float32).reshape(B, n, GLA_CHUNK, H, t.shape[-1])
    kc, vc = blk(k), blk(v)
    b = jnp.cumsum(blk(log_a), axis=2)
    b_last = b[:, :, -1:]
    d_state = jnp.einsum('bnchd,bnchv->bnhdv', kc * jnp.exp(b_last - b), vc)
    decay = jnp.exp(b_last[:, :, 0])

    def step(s, inp):
        dec, ds = inp
        return dec[..., None] * s + ds, (s if with_output else None)

    s_final, s_in = lax.scan(step, s0, (jnp.moveaxis(decay, 1, 0), jnp.moveaxis(d_state, 1, 0)))
    if not with_output:
        return None, s_final
    qt = blk(q) * jnp.exp(b)
    kt = kc * jnp.exp(-b)
    tri = jnp.tril(jnp.ones((GLA_CHUNK, GLA_CHUNK), dtype=bool))
    att = jnp.where(tri, jnp.einsum('bnqhd,bnkhd->bnhqk', qt, kt), 0.0)
    o = jnp.einsum('bnhqk,bnkhv->bnqhv', att, vc) + jnp.einsum('bnqhd,nbhdv->bnqhv', qt, s_in)
    return o.reshape(B, T, H, dv), s_final


def gla_output(o, r, g):
    B, T = o.shape[:2]
    on = (o * lax.rsqrt(jnp.mean(o * o, axis=-1, keepdims=True) + EPS)).reshape(B, T, GLA_VW)
    return on.astype(r.dtype) * g * jax.nn.silu(r)


def project_na_gla(h, w_in, wa2, ba):
    B, n, _ = h.shape
    cuts, acc = [], 0
    for size in AB_SIZES[:-1]:
        acc += size
        cuts.append(acc)
    qa, ka, va, qb, kb, vb, rb, z_fwd, z_bwd = jnp.split(h @ w_in, cuts, axis=-1)
    na = lambda t: t.reshape(B, n, NA_HEADS, HEAD_DIM)
    gk = lambda t: t.reshape(B, n, GLA_HEADS, GLA_DK)
    la = (gk(gla_log_decay(z_fwd, wa2[0], ba[0])), gk(gla_log_decay(z_bwd, wa2[1], ba[1])))
    return (na(qa), na(ka), na(va), gk(qb) * GLA_DK ** -0.5, gk(kb),
            vb.reshape(B, n, GLA_HEADS, GLA_DV), rb, la)


def mixer_na_gla(hc, hl, w_in, rpb, wa2, ba, g_out, need_ctx):
    qa_c, ka_c, va_c, qb_c, kb_c, vb_c, rb_c, la_c = project_na_gla(hc, w_in, wa2, ba)
    qa_l, ka_l, va_l, qb_l, kb_l, vb_l, rb_l, la_l = project_na_gla(hl, w_in, wa2, ba)
    flip = lambda t: jnp.flip(t, axis=1)
    zero = jnp.zeros((hl.shape[0], GLA_HEADS, GLA_DK, GLA_DV), jnp.float32)
    o_cf, s_cf = gla_chunked(qb_c, kb_c, vb_c, la_c[0], zero, need_ctx)
    o_cb, s_cb = gla_chunked(flip(qb_c), flip(kb_c), flip(vb_c), flip(la_c[1]), zero, need_ctx)
    o_lf, _ = gla_chunked(qb_l, kb_l, vb_l, la_l[0], s_cf, True)
    o_lb, _ = gla_chunked(flip(qb_l), flip(kb_l), flip(vb_l), flip(la_l[1]), s_cb, True)
    y_l = jnp.concatenate([neighbourhood_attention(qa_l, ka_l, va_l, ka_c, va_c, rpb),
                           gla_output(o_lf + flip(o_lb), rb_l, g_out)], axis=-1)
    y_c = None
    if need_ctx:
        y_c = jnp.concatenate([context_attention(qa_c, ka_c, va_c),
                               gla_output(o_cf + flip(o_cb), rb_c, g_out)], axis=-1)
    return y_c, y_l


def mixer_swa(hc, hl, w_in, sink, cos, sin, need_ctx):
    B, T, _ = hl.shape
    KV, G, d = SWA_KV_HEADS, SWA_GROUP, HEAD_DIM
    cut_q, cut_k = SWA_HEADS * d, (SWA_HEADS + KV) * d

    def proj(h):
        n = h.shape[1]
        q, k, v = jnp.split(h @ w_in, [cut_q, cut_k], axis=-1)
        return q.reshape(B, n, KV, G, d), k.reshape(B, n, KV, d), v.reshape(B, n, KV, d)

    qc, kc, vc = proj(hc)
    ql, kl, vl = proj(hl)
    ql, kl = apply_rope(ql, cos, sin), apply_rope(kl, cos, sin)
    scale = d ** -0.5
    sink_f = sink.astype(jnp.float32).reshape(KV, G)
    nb = T // SWA_BLOCK
    qb = ql.reshape(B, nb, SWA_BLOCK, KV, G, d)

    def band(t):
        tp = jnp.pad(t.reshape(B, nb, SWA_BLOCK, KV, d), ((0, 0), (1, 1), (0, 0), (0, 0), (0, 0)))
        return jnp.concatenate([tp[:, :-2], tp[:, 1:-1], tp[:, 2:]], axis=2)

    kb, vb = band(kl), band(vl)
    qpos = jnp.arange(T).reshape(nb, SWA_BLOCK)
    kpos = jnp.arange(nb)[:, None] * SWA_BLOCK + jnp.arange(-SWA_BLOCK, 2 * SWA_BLOCK)[None, :]
    ok = ((jnp.abs(qpos[:, :, None] - kpos[:, None, :]) <= SWA_WINDOW)
          & (kpos[:, None, :] >= 0) & (kpos[:, None, :] < T))
    s_lat = jnp.einsum('bnqhgd,bnmhd->bhgnqm', qb, kb).astype(jnp.float32) * scale
    s_lat = jnp.where(ok[None, None, None], s_lat, NEG_INF)
    s_ctx = jnp.einsum('bnqhgd,bchd->bhgnqc', qb, kc).astype(jnp.float32) * scale
    p_lat, p_ctx = joint_softmax([s_lat, s_ctx], sink_f[None, :, :, None, None, None])
    o = (jnp.einsum('bhgnqm,bnmhd->bnqhgd', p_lat.astype(vb.dtype), vb)
         + jnp.einsum('bhgnqc,bchd->bnqhgd', p_ctx.astype(vc.dtype), vc))
    y_l = o.reshape(B, T, SWA_HEADS * d)
    y_c = None
    if need_ctx:
        s_cc = jnp.einsum('bqhgd,bchd->bhgqc', qc, kc).astype(jnp.float32) * scale
        (p_cc,) = joint_softmax([s_cc], sink_f[None, :, :, None, None])
        y_c = jnp.einsum('bhgqc,bchd->bqhgd', p_cc.astype(vc.dtype), vc).reshape(B, hc.shape[1], SWA_HEADS * d)
    return y_c, y_l


def conv_ffn(h, w_up, conv_w, conv_b, w_down):
    u = h @ w_up
    T = u.shape[1]
    pad = CONV_W // 2
    up = jnp.pad(u, ((0, 0), (pad, pad), (0, 0)))
    u = functools.reduce(jnp.add, [up[:, t:t + T] * conv_w[t] for t in range(CONV_W)]) + conv_b
    gate, val = jnp.split(u, 2, axis=-1)
    return (jax.nn.silu(gate) * val) @ w_down


def setup_inputs(seed: int = 0) -> dict:
    key = jax.random.key(seed)
    ks = iter(jax.random.split(key, 32))
    f32 = jnp.float32
    D = D_MODEL

    def w(shape, fan_in, s=1.0):
        return jax.random.normal(next(ks), shape, f32) * (s * fan_in ** -0.5)

    def gain(shape):
        return 1.0 + 0.1 * jax.random.normal(next(ks), shape, f32)

    def small(shape, s=0.02):
        return s * jax.random.normal(next(ks), shape, f32)

    return {
        'x': jax.random.normal(next(ks), (BATCH, SEQ, D), f32),
        'c': jax.random.normal(next(ks), (BATCH, D), f32),
        'ctx': jax.random.normal(next(ks), (BATCH, CTX_LEN, D), f32),
        'c_ctx': jax.random.normal(next(ks), (D,), f32),
        'w_mod': w((DEPTH, D, 6 * D), D, 0.5),
        'b_mod': small((DEPTH, 6 * D)),
        'g_mix_pre': gain((DEPTH, D)),
        'g_mix_post': gain((DEPTH, D)),
        'g_ffn_pre': gain((DEPTH, D)),
        'g_ffn_post': gain((DEPTH, D)),
        'w_out': w((DEPTH, D, D), D),
        'w_up': w((DEPTH, D, 2 * D_FF), D),
        'conv_w': w((DEPTH, CONV_W, 2 * D_FF), CONV_W),
        'conv_b': small((DEPTH, 2 * D_FF)),
        'w_down': w((DEPTH, D_FF, D), D_FF),
        'w_in_ab': w((N_EVEN, D, IN_AB), D),
        'na_rpb': small((N_EVEN, NA_HEADS, 2 * NA_WIN_H_MAX - 1, 2 * NA_WIN_W - 1), 0.1),
        'gla_wa2': w((N_EVEN, 2, GLA_GATE_RANK, GLA_KW), GLA_GATE_RANK),
        'gla_ba': small((N_EVEN, 2, GLA_KW), 0.1),
        'gla_g': gain((N_EVEN, GLA_VW)),
        'w_in_c': w((N_ODD, D, IN_C), D),
        'swa_sink': jax.random.normal(next(ks), (N_ODD, SWA_HEADS), f32),
    }


def reference(x, c, ctx, c_ctx, w_mod, b_mod, g_mix_pre, g_mix_post, g_ffn_pre, g_ffn_post,
              w_out, w_up, conv_w, conv_b, w_down, w_in_ab, na_rpb, gla_wa2, gla_ba, gla_g,
              w_in_c, swa_sink):
    xl, xc = x, ctx
    cos, sin = axial_rope_tables(x.shape[1])
    s_lat = jax.nn.silu(c)
    s_ctx = jax.nn.silu(c_ctx)
    for i in range(DEPTH):
        need_ctx = i < DEPTH - 1
        j = i // 2
        m_l = jnp.split((s_lat @ w_mod[i] + b_mod[i])[:, None, :], 6, axis=-1)
        m_c = jnp.split(s_ctx @ w_mod[i] + b_mod[i], 6, axis=-1)
        hl = modulate(rmsnorm(xl, g_mix_pre[i]), m_l[0], m_l[1])
        hc = modulate(rmsnorm(xc, g_mix_pre[i]), m_c[0], m_c[1])
        if i % 2 == 0:
            yc, yl = mixer_na_gla(hc, hl, w_in_ab[j], na_rpb[j], gla_wa2[j], gla_ba[j], gla_g[j], need_ctx)
        else:
            yc, yl = mixer_swa(hc, hl, w_in_c[j], swa_sink[j], cos, sin, need_ctx)
        xl = xl + m_l[2] * rmsnorm(yl @ w_out[i], g_mix_post[i])
        hl = modulate(rmsnorm(xl, g_ffn_pre[i]), m_l[3], m_l[4])
        xl = xl + m_l[5] * rmsnorm(conv_ffn(hl, w_up[i], conv_w[i], conv_b[i], w_down[i]), g_ffn_post[i])
        if need_ctx:
            xc = xc + m_c[2] * rmsnorm(yc @ w_out[i], g_mix_post[i])
            hc = modulate(rmsnorm(xc, g_ffn_pre[i]), m_c[3], m_c[4])
            xc = xc + m_c[5] * rmsnorm(conv_ffn(hc, w_up[i], conv_w[i], conv_b[i], w_down[i]), g_ffn_post[i])
    return xl
```

```python
import functools

import numpy as np
import jax
import jax.numpy as jnp
from jax import lax
from jax.experimental import pallas as pl
from jax.experimental.pallas import tpu as pltpu

F32 = jnp.float32
BF16 = jnp.bfloat16

LANES = 128
HEAD_DIM = 64
GRID_W = 64
D_MODEL = 1024
EPS = 1e-6
NEG_INF = -1e30
ATTN_SCALE = HEAD_DIM ** -0.5

NA_HEADS = 8
NA_WIN_H = 8
NA_WIN_W = 16
NA_WIDTH = NA_HEADS * HEAD_DIM
GLA_HEADS = 4
GLA_DK = 64
GLA_DV = 128
GLA_KW = GLA_HEADS * GLA_DK
GLA_VW = GLA_HEADS * GLA_DV
GLA_RANK = 16
GLA_TAU = 16.0
GLA_CHUNK = 64
IN_AB = 3 * NA_WIDTH + 2 * GLA_KW + 2 * GLA_VW + 2 * GLA_RANK
IN_AB_PAD = 3200
SWA_HEADS = 16
SWA_KV = 4
SWA_GROUP = SWA_HEADS // SWA_KV
SWA_BLOCK = 128
IN_C = (SWA_HEADS + 2 * SWA_KV) * HEAD_DIM
D_FF = 2816
FF_CHUNK = 256
N_FF_CHUNKS = D_FF // FF_CHUNK
HALO = 16
ROPE_BASE = 10000.0
ROPE_TILE = 256
VMEM_LIMIT = 56 * 1024 * 1024

AB_QA, AB_KA, AB_VA = 0, 4, 8
AB_QB, AB_KB = 12, 14
AB_VB, AB_RB = 8, 10
AB_Z = 24
C_Q, C_K, C_V = 0, 8, 10


def _dot(a, b):
    return jnp.dot(a, b, preferred_element_type=F32)


def _dot_nt(a, b):
    return lax.dot_general(a, b, (((1,), (1,)), ((), ())), preferred_element_type=F32)


def _rms(x, g):
    return x * lax.rsqrt(jnp.mean(x * x, axis=-1, keepdims=True) + EPS) * g


def _silu(x):
    return x / (1.0 + jnp.exp(-x))


def _params(n_grid):
    return pltpu.CompilerParams(dimension_semantics=("arbitrary",) * n_grid,
                                vmem_limit_bytes=VMEM_LIMIT)


def _resident(shape):
    nd = len(shape)
    return pl.BlockSpec(shape, lambda *_: (0,) * nd, pipeline_mode=pl.Buffered(1))


def _head_mask(rows, h):
    lane = lax.broadcasted_iota(jnp.int32, (rows, LANES), 1)
    return (lane < HEAD_DIM) if h == 0 else (lane >= HEAD_DIM)


def _mod_kernel(c_ref, w_ref, b_ref, o_ref):
    s = _silu(c_ref[...]).astype(BF16)
    o_ref[0] = _dot(s, w_ref[0].astype(BF16)) + b_ref[0]


def _modulation(c_all, w_mod, b_mod):
    depth, d, n = w_mod.shape
    tn = 1536
    return pl.pallas_call(
        _mod_kernel,
        out_shape=jax.ShapeDtypeStruct((depth, 8, n), F32),
        grid=(depth, n // tn),
        in_specs=[pl.BlockSpec((8, d), lambda i, j: (0, 0)),
                  pl.BlockSpec((1, d, tn), lambda i, j: (i, 0, j)),
                  pl.BlockSpec((1, 1, tn), lambda i, j: (i, 0, j))],
        out_specs=pl.BlockSpec((1, 8, tn), lambda i, j: (i, 0, j)),
        compiler_params=_params(2),
        name="modulation",
    )(c_all, w_mod, b_mod.reshape(depth, 1, n))


def _rope(acc, cos, sin):
    n = acc.shape[1]
    lane = lax.broadcasted_iota(jnp.int32, acc.shape, 1)
    partner = jnp.where(lane % 2 == 0, pltpu.roll(acc, n - 1, 1), pltpu.roll(acc, 1, 1))
    return acc * cos + partner * sin


def _proj_kernel(*refs, chunk, rope_chunks, shift_idx):
    if rope_chunks:
        x_ref, mod_ref, g_ref, w_ref, cos_ref, sin_ref, o_ref = refs
    else:
        x_ref, mod_ref, g_ref, w_ref, o_ref = refs
    mod = mod_ref[0]
    h = _rms(x_ref[0], g_ref[...]) * (1.0 + mod[shift_idx + 1:shift_idx + 2]) + mod[shift_idx:shift_idx + 1]
    hb = h.astype(BF16)
    for c in range(w_ref.shape[1] // chunk):
        acc = _dot(hb, w_ref[:, c * chunk:(c + 1) * chunk])
        if c < rope_chunks:
            acc = _rope(acc, cos_ref[...], sin_ref[...])
        o_ref[0, :, c * chunk:(c + 1) * chunk] = acc.astype(BF16)


def _proj(x, mod, g, w, tm, chunk, rope=None, rope_chunks=0):
    b, s, d = x.shape
    n = w.shape[1]
    per_batch = mod.shape[0] > 1
    in_specs = [pl.BlockSpec((1, tm, d), lambda bi, i: (bi, i, 0)),
                pl.BlockSpec((1, 6, d), (lambda bi, i: (bi, 0, 0)) if per_batch else (lambda bi, i: (0, 0, 0))),
                _resident((1, d)),
                _resident((d, n))]
    args = [x, mod, g.reshape(1, d), w]
    if rope_chunks:
        in_specs += [pl.BlockSpec((tm, chunk), lambda bi, i: (i, 0))] * 2
        args += list(rope)
    return pl.pallas_call(
        functools.partial(_proj_kernel, chunk=chunk, rope_chunks=rope_chunks, shift_idx=0),
        out_shape=jax.ShapeDtypeStruct((b, s, n), BF16),
        grid=(b, s // tm),
        in_specs=in_specs,
        out_specs=pl.BlockSpec((1, tm, n), lambda bi, i: (bi, i, 0)),
        compiler_params=_params(2),
        name="proj",
    )(*args)


def _outproj_kernel(*refs, n_parts):
    y_refs, w_refs = refs[:n_parts], refs[n_parts:2 * n_parts]
    x_ref, mod_ref, g_ref, o_ref = refs[2 * n_parts:]
    acc = _dot(y_refs[0][0], w_refs[0][...])
    for yr, wr in zip(y_refs[1:], w_refs[1:]):
        acc = acc + _dot(yr[0], wr[...])
    o_ref[0] = x_ref[0] + mod_ref[0][2:3] * _rms(acc, g_ref[...])


def _outproj(ys, ws, x, mod, g, tm):
    b, s, d = x.shape
    per_batch = mod.shape[0] > 1
    in_specs = [pl.BlockSpec((1, tm, y.shape[2]), lambda bi, i: (bi, i, 0)) for y in ys]
    in_specs += [_resident(w.shape) for w in ws]
    in_specs += [pl.BlockSpec((1, tm, d), lambda bi, i: (bi, i, 0)),
                 pl.BlockSpec((1, 6, d), (lambda bi, i: (bi, 0, 0)) if per_batch else (lambda bi, i: (0, 0, 0))),
                 _resident((1, d))]
    return pl.pallas_call(
        functools.partial(_outproj_kernel, n_parts=len(ys)),
        out_shape=jax.ShapeDtypeStruct((b, s, d), F32),
        grid=(b, s // tm),
        in_specs=in_specs,
        out_specs=pl.BlockSpec((1, tm, d), lambda bi, i: (bi, i, 0)),
        compiler_params=_params(2),
        name="outproj",
    )(*ys, *ws, x, mod, g.reshape(1, d))


def _ffn_kernel(x_ref, xp_ref, xn_ref, mod_ref, gpre_ref, gpost_ref, wu_ref, cw_ref, wd_ref, o_ref,
                h_ref, acc_ref, *, tm):
    i = pl.program_id(1)
    mod = mod_ref[0]
    gpre = gpre_ref[...]

    def norm_mod(x):
        return _rms(x, gpre) * (1.0 + mod[4:5]) + mod[3:4]

    keep_prev = (i > 0).astype(F32)
    keep_next = (i < pl.num_programs(1) - 1).astype(F32)
    h_ref[0:HALO] = (norm_mod(xp_ref[0]) * keep_prev).astype(BF16)
    h_ref[HALO:HALO + tm] = norm_mod(x_ref[0]).astype(BF16)
    h_ref[HALO + tm:] = (norm_mod(xn_ref[0]) * keep_next).astype(BF16)
    acc_ref[...] = jnp.zeros_like(acc_ref)
    rows = tm + 2 * HALO

    def chunk(c, carry):
        u = _dot(h_ref[...], wu_ref[c])
        cw = cw_ref[c]
        y = (pltpu.roll(u, 1, 0)[HALO:HALO + tm] * cw[0:1]
             + u[HALO:HALO + tm] * cw[1:2]
             + pltpu.roll(u, rows - 1, 0)[HALO:HALO + tm] * cw[2:3]
             + cw[3:4])
        a = (_silu(y[:, :FF_CHUNK]) * y[:, FF_CHUNK:]).astype(BF16)
        acc_ref[...] += _dot(a, wd_ref[c])
        return carry

    lax.fori_loop(0, N_FF_CHUNKS, chunk, 0)
    o_ref[0] = x_ref[0] + mod[5:6] * _rms(acc_ref[...], gpost_ref[...])


def _ffn(x, mod, g_pre, g_post, wu, cw, wd, tm):
    b, s, d = x.shape
    per_batch = mod.shape[0] > 1
    hb = tm // HALO
    n_halo = s // HALO
    return pl.pallas_call(
        functools.partial(_ffn_kernel, tm=tm),
        out_shape=jax.ShapeDtypeStruct((b, s, d), F32),
        grid=(b, s // tm),
        in_specs=[pl.BlockSpec((1, tm, d), lambda bi, i: (bi, i, 0)),
                  pl.BlockSpec((1, HALO, d), lambda bi, i: (bi, jnp.maximum(i * hb - 1, 0), 0)),
                  pl.BlockSpec((1, HALO, d), lambda bi, i: (bi, jnp.minimum((i + 1) * hb, n_halo - 1), 0)),
                  pl.BlockSpec((1, 6, d), (lambda bi, i: (bi, 0, 0)) if per_batch else (lambda bi, i: (0, 0, 0))),
                  _resident((1, d)), _resident((1, d)),
                  _resident(wu.shape), _resident(cw.shape), _resident(wd.shape)],
        out_specs=pl.BlockSpec((1, tm, d), lambda bi, i: (bi, i, 0)),
        scratch_shapes=[pltpu.VMEM((tm + 2 * HALO, d), BF16), pltpu.VMEM((tm, d), F32)],
        compiler_params=_params(2),
        name="ffn",
    )(x, x, x, mod, g_pre.reshape(1, d), g_post.reshape(1, d), wu, cw, wd)


def _na_kernel(q_ref, k_ref, v_ref, kc_ref, vc_ref, bias_ref, o_ref, *, rows):
    kc = kc_ref[0]
    vc = vc_ref[0]
    win = NA_WIN_H * GRID_W
    masks = [_head_mask(GRID_W, 0), _head_mask(GRID_W, 1)]

    def row(r, carry):
        r0 = jnp.clip(r - NA_WIN_H // 2, 0, rows - NA_WIN_H)
        e = r - r0
        q = q_ref[0, pl.ds(pl.multiple_of(r * GRID_W, GRID_W), GRID_W), :]
        k0 = pl.multiple_of(r0 * GRID_W, GRID_W)
        kw = k_ref[0, pl.ds(k0, win), :]
        vw = v_ref[0, pl.ds(k0, win), :]
        outs = []
        for h in range(2):
            qm = jnp.where(masks[h], q, jnp.zeros_like(q))
            s = _dot_nt(qm, kw) + bias_ref[0, h, e]
            sc = _dot_nt(qm, kc)
            m = jnp.maximum(jnp.max(s, axis=-1, keepdims=True), jnp.max(sc, axis=-1, keepdims=True))
            p = jnp.exp(s - m)
            pc = jnp.exp(sc - m)
            den = jnp.sum(p, axis=-1, keepdims=True) + jnp.sum(pc, axis=-1, keepdims=True)
            outs.append((_dot(p.astype(BF16), vw) + _dot(pc.astype(BF16), vc)) / den)
        o = jnp.where(masks[0], outs[0], outs[1])
        o_ref[0, pl.ds(pl.multiple_of(r * GRID_W, GRID_W), GRID_W), :] = o.astype(BF16)
        return carry

    lax.fori_loop(0, rows, row, 0)


def _na(pl_, pc_, bias):
    b, s, _ = pl_.shape
    n_ctx = pc_.shape[1]
    n_pairs = NA_HEADS // 2
    return pl.pallas_call(
        functools.partial(_na_kernel, rows=s // GRID_W),
        out_shape=jax.ShapeDtypeStruct((b, s, NA_WIDTH), BF16),
        grid=(b, n_pairs),
        in_specs=[pl.BlockSpec((1, s, LANES), lambda bi, p: (bi, 0, AB_QA + p)),
                  pl.BlockSpec((1, s, LANES), lambda bi, p: (bi, 0, AB_KA + p)),
                  pl.BlockSpec((1, s, LANES), lambda bi, p: (bi, 0, AB_VA + p)),
                  pl.BlockSpec((1, n_ctx, LANES), lambda bi, p: (bi, 0, AB_KA + p)),
                  pl.BlockSpec((1, n_ctx, LANES), lambda bi, p: (bi, 0, AB_VA + p)),
                  pl.BlockSpec((1, 2, NA_WIN_H, GRID_W, NA_WIN_H * GRID_W), lambda bi, p: (p, 0, 0, 0, 0))],
        out_specs=pl.BlockSpec((1, s, LANES), lambda bi, p: (bi, 0, p)),
        compiler_params=_params(2),
        name="na",
    )(pl_, pl_, pl_, pc_, pc_, bias)


def _na_bias(rpb):
    h = rpb.shape[0]
    cidx = np.arange(GRID_W)
    c_start = np.clip(cidx - NA_WIN_W // 2, 0, GRID_W - NA_WIN_W)
    col_ok = (cidx[None, :] >= c_start[:, None]) & (cidx[None, :] < c_start[:, None] + NA_WIN_W)
    dc = np.clip(cidx[None, :] - cidx[:, None] + NA_WIN_W - 1, 0, 2 * NA_WIN_W - 2)
    bm = jnp.where(col_ok[None, None], rpb.astype(F32)[:, :, dc], NEG_INF)
    dr = np.arange(NA_WIN_H)[None, :] - np.arange(NA_WIN_H)[:, None] + NA_WIN_H - 1
    t = bm[:, dr]
    t = t.transpose(0, 1, 3, 2, 4).reshape(h // 2, 2, NA_WIN_H, GRID_W, NA_WIN_H * GRID_W)
    return t


def _ctx_attn_kernel(*refs, has_sink):
    if has_sink:
        q_ref, k_ref, v_ref, sink_ref, o_ref = refs
    else:
        q_ref, k_ref, v_ref, o_ref = refs
    q, k, v = q_ref[0], k_ref[0], v_ref[0]
    n = q.shape[0]
    masks = [_head_mask(n, 0), _head_mask(n, 1)]
    outs = []
    for h in range(2):
        qm = jnp.where(masks[h], q, jnp.zeros_like(q))
        s = _dot_nt(qm, k)
        m = jnp.max(s, axis=-1, keepdims=True)
        if has_sink:
            sk = sink_ref[0, h:h + 1, 0:1]
            m = jnp.maximum(m, sk)
        p = jnp.exp(s - m)
        den = jnp.sum(p, axis=-1, keepdims=True)
        if has_sink:
            den = den + jnp.exp(sk - m)
        outs.append(_dot(p.astype(BF16), v) / den)
    o_ref[0] = jnp.where(masks[0], outs[0], outs[1]).astype(BF16)


def _ctx_attn(pc_, q_off, k_off, v_off, n_q_tiles, group, sink=None):
    b, n, _ = pc_.shape
    in_specs = [pl.BlockSpec((1, n, LANES), lambda bi, t: (bi, 0, q_off + t)),
                pl.BlockSpec((1, n, LANES), lambda bi, t: (bi, 0, k_off + t // group)),
                pl.BlockSpec((1, n, LANES), lambda bi, t: (bi, 0, v_off + t // group))]
    args = [pc_, pc_, pc_]
    if sink is not None:
        in_specs.append(pl.BlockSpec((1, 2, LANES), lambda bi, t: (t, 0, 0)))
        args.append(sink)
    return pl.pallas_call(
        functools.partial(_ctx_attn_kernel, has_sink=sink is not None),
        out_shape=jax.ShapeDtypeStruct((b, n, n_q_tiles * LANES), BF16),
        grid=(b, n_q_tiles),
        in_specs=in_specs,
        out_specs=pl.BlockSpec((1, n, LANES), lambda bi, t: (bi, 0, t)),
        compiler_params=_params(2),
        name="ctx_attn",
    )(*args)


def _gla_kernel(*refs, n_lat, n_ctx, need_ctx):
    (ql, kl, vl, zl, rl, qc, kc, vc, zc, rc, w2f, w2b, baf, bab, g_ref) = refs[:15]
    if need_ctx:
        yl_ref, yc_ref, of_l, of_c = refs[15:]
    else:
        yl_ref, of_l = refs[15:]
        yc_ref = of_c = None
    ck = GLA_CHUNK
    ri = lax.broadcasted_iota(jnp.int32, (ck, ck), 0)
    ci = lax.broadcasted_iota(jnp.int32, (ck, ck), 1)
    tril, triu = ri >= ci, ri <= ci
    cum_f, cum_b = tril.astype(BF16), triu.astype(BF16)
    masks = [_head_mask(ck, 0), _head_mask(ck, 1)]
    g = g_ref[...]

    def step(refs4, t0, state, fwd):
        q_ref, k_ref, v_ref, z_ref = refs4
        t0 = pl.multiple_of(t0, ck)
        q = q_ref[0, pl.ds(t0, ck), :].astype(F32)
        k = k_ref[0, pl.ds(t0, ck), :].astype(F32)
        v = v_ref[0, pl.ds(t0, ck), :]
        z = z_ref[0, pl.ds(t0, ck), :]
        x = _dot(z, (w2f if fwd else w2b)[...]) + (baf if fwd else bab)[...]
        la = (jnp.minimum(x, 0.0) - jnp.log(1.0 + jnp.exp(-jnp.abs(x)))) * (1.0 / GLA_TAU)
        hi = la.astype(BF16)
        lo = (la - hi.astype(F32)).astype(BF16)
        bb = _dot(cum_f if fwd else cum_b, jnp.concatenate([hi, lo], axis=1))
        b = bb[:, :LANES] + bb[:, LANES:]
        btot = b[ck - 1:ck] if fwd else b[0:1]
        qt = q * jnp.exp(b)
        kt = (k * jnp.exp(-b)).astype(BF16)
        kd = k * jnp.exp(btot - b)
        sb = state.astype(BF16)
        tri = tril if fwd else triu
        outs = []
        for h in range(2):
            qm = jnp.where(masks[h], qt, 0.0).astype(BF16)
            att = jnp.where(tri, _dot_nt(qm, kt), 0.0).astype(BF16)
            outs.append(_dot(att, v[:, h * GLA_DV:(h + 1) * GLA_DV])
                        + _dot(qm, sb[:, h * GLA_DV:(h + 1) * GLA_DV]))
        o = jnp.concatenate(outs, axis=1)
        xt = jnp.concatenate([kd, jnp.broadcast_to(jnp.exp(btot), (ck, LANES))], axis=0).T
        new_state = xt[:, ck:ck + 1] * state + _dot(xt[:, :ck].astype(BF16), v)
        return o, new_state

    def finish(o, r_ref, t0, y_ref):
        t0 = pl.multiple_of(t0, ck)
        r = r_ref[0, pl.ds(t0, ck), :].astype(F32)
        on = jnp.concatenate(
            [o[:, h * GLA_DV:(h + 1) * GLA_DV]
             * lax.rsqrt(jnp.mean(o[:, h * GLA_DV:(h + 1) * GLA_DV] ** 2, axis=-1, keepdims=True) + EPS)
             for h in range(2)], axis=1)
        y_ref[0, pl.ds(t0, ck), :] = (on * g * _silu(r)).astype(BF16)

    lat4, ctx4 = (ql, kl, vl, zl), (qc, kc, vc, zc)
    zero = jnp.zeros((2 * GLA_DK, 2 * GLA_DV), F32)

    def fwd_ctx(n, s):
        o, s = step(ctx4, n * ck, s, True)
        if need_ctx:
            of_c[pl.ds(pl.multiple_of(n * ck, ck), ck), :] = o
        return s

    def fwd_lat(n, s):
        o, s = step(lat4, n * ck, s, True)
        of_l[pl.ds(pl.multiple_of(n * ck, ck), ck), :] = o
        return s

    def bwd_ctx(i, s):
        n = n_ctx - 1 - i
        o, s = step(ctx4, n * ck, s, False)
        if need_ctx:
            finish(o + of_c[pl.ds(pl.multiple_of(n * ck, ck), ck), :], rc, n * ck, yc_ref)
        return s

    def bwd_lat(i, s):
        n = n_lat - 1 - i
        o, s = step(lat4, n * ck, s, False)
        finish(o + of_l[pl.ds(pl.multiple_of(n * ck, ck), ck), :], rl, n * ck, yl_ref)
        return s

    s = lax.fori_loop(0, n_ctx, fwd_ctx, zero)
    lax.fori_loop(0, n_lat, fwd_lat, s)
    s = lax.fori_loop(0, n_ctx, bwd_ctx, zero)
    lax.fori_loop(0, n_lat, bwd_lat, s)


def _gla(pl_, pc_, w2, ba, g, need_ctx):
    b, s, _ = pl_.shape
    n_c = pc_.shape[1]
    n_pairs = GLA_HEADS // 2
    w = 2 * GLA_DV

    def stream(n):
        return [pl.BlockSpec((1, n, LANES), lambda bi, p: (bi, 0, AB_QB + p)),
                pl.BlockSpec((1, n, LANES), lambda bi, p: (bi, 0, AB_KB + p)),
                pl.BlockSpec((1, n, w), lambda bi, p: (bi, 0, AB_VB + p)),
                pl.BlockSpec((1, n, LANES), lambda bi, p: (bi, 0, AB_Z)),
                pl.BlockSpec((1, n, w), lambda bi, p: (bi, 0, AB_RB + p))]

    in_specs = stream(s) + stream(n_c) + [
        pl.BlockSpec((LANES, LANES), lambda bi, p: (0, p)),
        pl.BlockSpec((LANES, LANES), lambda bi, p: (0, n_pairs + p)),
        pl.BlockSpec((1, LANES), lambda bi, p: (0, p)),
        pl.BlockSpec((1, LANES), lambda bi, p: (0, n_pairs + p)),
        pl.BlockSpec((1, w), lambda bi, p: (0, p))]
    out_shape = [jax.ShapeDtypeStruct((b, s, GLA_VW), BF16)]
    out_specs = [pl.BlockSpec((1, s, w), lambda bi, p: (bi, 0, p))]
    scratch = [pltpu.VMEM((s, w), F32)]
    if need_ctx:
        out_shape.append(jax.ShapeDtypeStruct((b, n_c, GLA_VW), BF16))
        out_specs.append(pl.BlockSpec((1, n_c, w), lambda bi, p: (bi, 0, p)))
        scratch.append(pltpu.VMEM((n_c, w), F32))
    res = pl.pallas_call(
        functools.partial(_gla_kernel, n_lat=s // GLA_CHUNK, n_ctx=n_c // GLA_CHUNK, need_ctx=need_ctx),
        out_shape=out_shape,
        grid=(b, n_pairs),
        in_specs=in_specs,
        out_specs=out_specs,
        scratch_shapes=scratch,
        compiler_params=_params(2),
        name="gla",
    )(*([pl_] * 5), *([pc_] * 5), w2, w2, ba, ba, g.reshape(1, GLA_VW))
    return (res[0], res[1]) if need_ctx else (res[0], None)


def _gla_gate_weights(wa2, ba):
    w = jnp.zeros((LANES, 2 * GLA_KW), F32)
    w = w.at[0:GLA_RANK, 0:GLA_KW].set(wa2[0])
    w = w.at[GLA_RANK:2 * GLA_RANK, GLA_KW:].set(wa2[1])
    return w.astype(BF16), ba.reshape(1, 2 * GLA_KW).astype(F32)


def _swa_kernel(q_ref, kp_ref, kn_ref, kx_ref, vp_ref, vn_ref, vx_ref, kc_ref, vc_ref, sink_ref, o_ref):
    n = pl.program_id(2)
    blk = SWA_BLOCK
    n_c = kc_ref.shape[1]
    kcat = jnp.concatenate([kp_ref[0], kn_ref[0], kx_ref[0], kc_ref[0]], axis=0)
    vcat = jnp.concatenate([vp_ref[0], vn_ref[0], vx_ref[0], vc_ref[0]], axis=0)
    qi = lax.broadcasted_iota(jnp.int32, (blk, 3 * blk + n_c), 0)
    kj = lax.broadcasted_iota(jnp.int32, (blk, 3 * blk + n_c), 1)
    has_prev = n > 0
    has_next = n < pl.num_programs(2) - 1
    ok = ((kj >= 3 * blk)
          | ((kj >= blk) & (kj < 2 * blk))
          | ((kj < blk) & (kj >= qi) & has_prev)
          | ((kj >= 2 * blk) & (kj < 3 * blk) & (kj - 2 * blk <= qi) & has_next))
    masks = [_head_mask(blk, 0), _head_mask(blk, 1)]
    for gi in range(SWA_GROUP):
        q = q_ref[0, :, gi * LANES:(gi + 1) * LANES]
        outs = []
        for h in range(2):
            qm = jnp.where(masks[h], q, jnp.zeros_like(q))
            s = jnp.where(ok, _dot_nt(qm, kcat), NEG_INF)
            sk = sink_ref[gi, h:h + 1, 0:1]
            m = jnp.maximum(jnp.max(s, axis=-1, keepdims=True), sk)
            p = jnp.exp(s - m)
            den = jnp.sum(p, axis=-1, keepdims=True) + jnp.exp(sk - m)
            outs.append(_dot(p.astype(BF16), vcat) / den)
        o_ref[0, :, gi * LANES:(gi + 1) * LANES] = jnp.where(masks[0], outs[0], outs[1]).astype(BF16)


def _swa(pl_, pc_, sink):
    b, s, _ = pl_.shape
    n_c = pc_.shape[1]
    nb = s // SWA_BLOCK
    n_pairs = SWA_KV // 2
    qw = SWA_GROUP * LANES

    def kv(off):
        return [pl.BlockSpec((1, SWA_BLOCK, LANES), lambda bi, p, n: (bi, jnp.maximum(n - 1, 0), off + p)),
                pl.BlockSpec((1, SWA_BLOCK, LANES), lambda bi, p, n: (bi, n, off + p)),
                pl.BlockSpec((1, SWA_BLOCK, LANES), lambda bi, p, n: (bi, jnp.minimum(n + 1, nb - 1), off + p))]

    in_specs = ([pl.BlockSpec((1, SWA_BLOCK, qw), lambda bi, p, n: (bi, n, p))] + kv(C_K) + kv(C_V)
                + [pl.BlockSpec((1, n_c, LANES), lambda bi, p, n: (bi, 0, C_K + p)),
                   pl.BlockSpec((1, n_c, LANES), lambda bi, p, n: (bi, 0, C_V + p)),
                   pl.BlockSpec((SWA_GROUP, 2, LANES), lambda bi, p, n: (p, 0, 0))])
    return pl.pallas_call(
        _swa_kernel,
        out_shape=jax.ShapeDtypeStruct((b, s, SWA_HEADS * HEAD_DIM), BF16),
        grid=(b, n_pairs, nb),
        in_specs=in_specs,
        out_specs=pl.BlockSpec((1, SWA_BLOCK, qw), lambda bi, p, n: (bi, n, p)),
        compiler_params=_params(3),
        name="swa",
    )(pl_, pl_, pl_, pl_, pl_, pl_, pl_, pc_, pc_, sink)


def _swa_head_perm():
    perm = np.zeros(SWA_HEADS * HEAD_DIM, np.int32)
    d = np.arange(HEAD_DIM)
    for p in range(SWA_KV // 2):
        for gi in range(SWA_GROUP):
            for hh in range(2):
                perm[(p * SWA_GROUP + gi) * LANES + hh * HEAD_DIM + d] = ((2 * p + hh) * SWA_GROUP + gi) * HEAD_DIM + d
    return perm


def _rope_tables(n):
    t = jnp.arange(n)
    row = (t // GRID_W).astype(F32)
    col = (t % GRID_W).astype(F32)
    half = HEAD_DIM // 2
    inv = 1.0 / (ROPE_BASE ** (jnp.arange(0, half, 2, dtype=F32) / half))
    ang = jnp.concatenate([row[:, None] * inv, col[:, None] * inv], axis=-1)
    cos = jnp.repeat(jnp.cos(ang), 2, axis=1)
    sin = jnp.repeat(jnp.sin(ang), 2, axis=1) * jnp.tile(jnp.array([-1.0, 1.0], F32), half)
    reps = ROPE_TILE // HEAD_DIM
    return jnp.tile(cos, (1, reps)), jnp.tile(sin, (1, reps))


def kernel(x, c, ctx, c_ctx, w_mod, b_mod, g_mix_pre, g_mix_post, g_ffn_pre, g_ffn_post, w_out, w_up, conv_w,
           conv_b, w_down, w_in_ab, na_rpb, gla_wa2, gla_ba, gla_g, w_in_c, swa_sink):
    bsz, seq, d = x.shape
    n_ctx = ctx.shape[1]
    depth = w_mod.shape[0]
    tm_l = 512 if seq % 512 == 0 else 256
    tm_c = n_ctx

    c_all = jnp.zeros((8, d), F32).at[:bsz].set(c).at[bsz].set(c_ctx)
    mods = _modulation(c_all, w_mod, b_mod).reshape(depth, 8, 6, d)
    rope = _rope_tables(seq)
    perm = _swa_head_perm()

    col_scale_ab = np.ones((IN_AB,), np.float32)
    col_scale_ab[0:NA_WIDTH] = ATTN_SCALE
    col_scale_ab[3 * NA_WIDTH:3 * NA_WIDTH + GLA_KW] = GLA_DK ** -0.5

    xl, xc = x, ctx
    for i in range(depth):
        need_ctx = i < depth - 1
        j = i // 2
        mod_l = mods[i, :bsz]
        mod_c = mods[i, bsz:bsz + 1]
        if i % 2 == 0:
            w_in = jnp.pad(w_in_ab[j] * col_scale_ab, ((0, 0), (0, IN_AB_PAD - IN_AB))).astype(BF16)
            p_l = _proj(xl, mod_l, g_mix_pre[i], w_in, tm_l, 640)
            p_c = _proj(xc, mod_c, g_mix_pre[i], w_in, tm_c, 640)
            w2, ba = _gla_gate_weights(gla_wa2[j], gla_ba[j])
            y_na = _na(p_l, p_c, _na_bias(na_rpb[j]))
            y_gla, yc_gla = _gla(p_l, p_c, w2, ba, gla_g[j], need_ctx)
            ys_l = [y_na, y_gla]
            ws = [w_out[i][:NA_WIDTH].astype(BF16), w_out[i][NA_WIDTH:].astype(BF16)]
            if need_ctx:
                ys_c = [_ctx_attn(p_c, AB_QA, AB_KA, AB_VA, NA_HEADS // 2, 1), yc_gla]
        else:
            wq = w_in_c[j][:, :SWA_HEADS * HEAD_DIM][:, perm] * ATTN_SCALE
            w_in = jnp.concatenate([wq, w_in_c[j][:, SWA_HEADS * HEAD_DIM:]], axis=1).astype(BF16)
            p_l = _proj(xl, mod_l, g_mix_pre[i], w_in, tm_l, ROPE_TILE, rope, (C_V * LANES) // ROPE_TILE)
            p_c = _proj(xc, mod_c, g_mix_pre[i], w_in, tm_c, ROPE_TILE)
            sink = jnp.broadcast_to(swa_sink[j][perm[::HEAD_DIM] // HEAD_DIM].reshape(-1, 2, 1),
                                    (SWA_HEADS // 2, 2, LANES)).astype(F32)
            ys_l = [_swa(p_l, p_c, sink)]
            ws = [w_out[i][perm].astype(BF16)]
            if need_ctx:
                ys_c = [_ctx_attn(p_c, C_Q, C_K, C_V, SWA_HEADS // 2, SWA_GROUP, sink)]

        wu = jnp.concatenate([w_up[i][:, :D_FF].reshape(d, N_FF_CHUNKS, FF_CHUNK),
                              w_up[i][:, D_FF:].reshape(d, N_FF_CHUNKS, FF_CHUNK)], axis=2)
        wu = wu.transpose(1, 0, 2).astype(BF16)
        cwb = jnp.concatenate([conv_w[i], conv_b[i][None]], axis=0)
        cw = jnp.concatenate([cwb[:, :D_FF].reshape(4, N_FF_CHUNKS, FF_CHUNK),
                              cwb[:, D_FF:].reshape(4, N_FF_CHUNKS, FF_CHUNK)], axis=2).transpose(1, 0, 2)
        wd = w_down[i].reshape(N_FF_CHUNKS, FF_CHUNK, d).astype(BF16)

        xl = _outproj(ys_l, ws, xl, mod_l, g_mix_post[i], tm_l)
        xl = _ffn(xl, mod_l, g_ffn_pre[i], g_ffn_post[i], wu, cw, wd, tm_l)
        if need_ctx:
            xc = _outproj(ys_c, ws, xc, mod_c, g_mix_post[i], tm_c)
            xc = _ffn(xc, mod_c, g_ffn_pre[i], g_ffn_post[i], wu, cw, wd, tm_c)
    return xl
```

```python
import functools

import numpy as np
import jax
import jax.numpy as jnp
from jax import lax
from jax.experimental import pallas as pl
from jax.experimental.pallas import tpu as pltpu

F32 = jnp.float32
BF16 = jnp.bfloat16

LANES = 128
HEAD_DIM = 64
GRID_W = 64
D_MODEL = 1024
EPS = 1e-6
NEG_INF = -1e30
LOG2E = float(np.log2(np.e))
ATTN_SCALE = HEAD_DIM ** -0.5 * LOG2E

NA_HEADS = 8
NA_WIN_H = 8
NA_WIN_W = 16
NA_WIDTH = NA_HEADS * HEAD_DIM
NA_GROUP = 4
GLA_HEADS = 4
GLA_DK = 64
GLA_DV = 128
GLA_KW = GLA_HEADS * GLA_DK
GLA_VW = GLA_HEADS * GLA_DV
GLA_RANK = 16
GLA_TAU = 16.0
GLA_CHUNK = 64
IN_AB = 3 * NA_WIDTH + 2 * GLA_KW + 2 * GLA_VW + 2 * GLA_RANK
IN_AB_PAD = 3200
SWA_HEADS = 16
SWA_KV = 4
SWA_GROUP = SWA_HEADS // SWA_KV
SWA_BLOCK = 128
IN_C = (SWA_HEADS + 2 * SWA_KV) * HEAD_DIM
D_FF = 2816
FF_CHUNK = 256
N_FF_CHUNKS = D_FF // FF_CHUNK
assert N_FF_CHUNKS * FF_CHUNK == D_FF and N_FF_CHUNKS % 2 == 1
HALO = 16
ROPE_BASE = 10000.0
ROPE_TILE = 256
VMEM_LIMIT = 56 * 1024 * 1024

AB_QA, AB_KA, AB_VA = 0, 4, 8
AB_QB, AB_KB = 12, 14
AB_VB, AB_RB = 8, 10
AB_Z = 24
C_Q, C_K, C_V = 0, 8, 10


def _dot(a, b):
    return jnp.dot(a, b, preferred_element_type=F32)


def _dot_nt(a, b):
    return lax.dot_general(a, b, (((1,), (1,)), ((), ())), preferred_element_type=F32)


def _rms(x, g):
    return x * lax.rsqrt(jnp.mean(x * x, axis=-1, keepdims=True) + EPS) * g


def _silu(x):
    return x / (1.0 + jnp.exp(-x))


def _params(n_grid):
    return pltpu.CompilerParams(dimension_semantics=("arbitrary",) * n_grid,
                                vmem_limit_bytes=VMEM_LIMIT)


def _resident(shape):
    nd = len(shape)
    return pl.BlockSpec(shape, lambda *_: (0,) * nd, pipeline_mode=pl.Buffered(1))


def _head_mask(rows, h):
    lane = lax.broadcasted_iota(jnp.int32, (rows, LANES), 1)
    return (lane < HEAD_DIM) if h == 0 else (lane >= HEAD_DIM)


def _mod_kernel(c_ref, w_ref, b_ref, o_ref):
    s = _silu(c_ref[...]).astype(BF16)
    o_ref[0] = _dot(s, w_ref[0].astype(BF16)) + b_ref[0]


def _modulation(c_all, w_mod, b_mod):
    depth, d, n = w_mod.shape
    tn = 1536
    return pl.pallas_call(
        _mod_kernel,
        out_shape=jax.ShapeDtypeStruct((depth, 8, n), F32),
        grid=(depth, n // tn),
        in_specs=[pl.BlockSpec((8, d), lambda i, j: (0, 0)),
                  pl.BlockSpec((1, d, tn), lambda i, j: (i, 0, j)),
                  pl.BlockSpec((1, 1, tn), lambda i, j: (i, 0, j))],
        out_specs=pl.BlockSpec((1, 8, tn), lambda i, j: (i, 0, j)),
        compiler_params=_params(2),
        name="modulation",
    )(c_all, w_mod, b_mod.reshape(depth, 1, n))


def _rope(acc, cos, sin):
    n = acc.shape[1]
    lane = lax.broadcasted_iota(jnp.int32, acc.shape, 1)
    partner = jnp.where(lane % 2 == 0, pltpu.roll(acc, n - 1, 1), pltpu.roll(acc, 1, 1))
    return acc * cos + partner * sin


def _proj_kernel(*refs, chunk, rope_chunks, shift_idx):
    if rope_chunks:
        x_ref, mod_ref, g_ref, w_ref, cos_ref, sin_ref, o_ref = refs
    else:
        x_ref, mod_ref, g_ref, w_ref, o_ref = refs
    mod = mod_ref[0]
    h = _rms(x_ref[0], g_ref[...]) * (1.0 + mod[shift_idx + 1:shift_idx + 2]) + mod[shift_idx:shift_idx + 1]
    hb = h.astype(BF16)
    for c in range(w_ref.shape[1] // chunk):
        acc = _dot(hb, w_ref[:, c * chunk:(c + 1) * chunk])
        if c < rope_chunks:
            acc = _rope(acc, cos_ref[...], sin_ref[...])
        o_ref[0, :, c * chunk:(c + 1) * chunk] = acc.astype(BF16)


def _proj(x, mod, g, w, tm, chunk, rope=None, rope_chunks=0):
    b, s, d = x.shape
    n = w.shape[1]
    per_batch = mod.shape[0] > 1
    in_specs = [pl.BlockSpec((1, tm, d), lambda bi, i: (bi, i, 0)),
                pl.BlockSpec((1, 6, d), (lambda bi, i: (bi, 0, 0)) if per_batch else (lambda bi, i: (0, 0, 0))),
                _resident((1, d)),
                _resident((d, n))]
    args = [x, mod, g.reshape(1, d), w]
    if rope_chunks:
        in_specs += [pl.BlockSpec((tm, chunk), lambda bi, i: (i, 0))] * 2
        args += list(rope)
    return pl.pallas_call(
        functools.partial(_proj_kernel, chunk=chunk, rope_chunks=rope_chunks, shift_idx=0),
        out_shape=jax.ShapeDtypeStruct((b, s, n), BF16),
        grid=(b, s // tm),
        in_specs=in_specs,
        out_specs=pl.BlockSpec((1, tm, n), lambda bi, i: (bi, i, 0)),
        compiler_params=_params(2),
        name="proj",
    )(*args)


def _outproj_kernel(*refs, n_parts):
    y_refs, w_refs = refs[:n_parts], refs[n_parts:2 * n_parts]
    x_ref, mod_ref, g_ref, o_ref = refs[2 * n_parts:]
    acc = _dot(y_refs[0][0], w_refs[0][...])
    for yr, wr in zip(y_refs[1:], w_refs[1:]):
        acc = acc + _dot(yr[0], wr[...])
    o_ref[0] = x_ref[0] + mod_ref[0][2:3] * _rms(acc, g_ref[...])


def _outproj(ys, ws, x, mod, g, tm):
    b, s, d = x.shape
    per_batch = mod.shape[0] > 1
    in_specs = [pl.BlockSpec((1, tm, y.shape[2]), lambda bi, i: (bi, i, 0)) for y in ys]
    in_specs += [_resident(w.shape) for w in ws]
    in_specs += [pl.BlockSpec((1, tm, d), lambda bi, i: (bi, i, 0)),
                 pl.BlockSpec((1, 6, d), (lambda bi, i: (bi, 0, 0)) if per_batch else (lambda bi, i: (0, 0, 0))),
                 _resident((1, d))]
    return pl.pallas_call(
        functools.partial(_outproj_kernel, n_parts=len(ys)),
        out_shape=jax.ShapeDtypeStruct((b, s, d), F32),
        grid=(b, s // tm),
        in_specs=in_specs,
        out_specs=pl.BlockSpec((1, tm, d), lambda bi, i: (bi, i, 0)),
        compiler_params=_params(2),
        name="outproj",
    )(*ys, *ws, x, mod, g.reshape(1, d))


def _ffn_kernel(x_ref, xp_ref, xn_ref, mod_ref, gpre_ref, gpost_ref, wu_ref, cw_ref, wd_ref, o_ref,
                h_ref, u_ref, acc_ref, *, tm):
    i = pl.program_id(1)
    mod = mod_ref[0]
    gpre = gpre_ref[...]

    def norm_mod(x):
        return _rms(x, gpre) * (1.0 + mod[4:5]) + mod[3:4]

    keep_prev = (i > 0).astype(F32)
    keep_next = (i < pl.num_programs(1) - 1).astype(F32)
    h_ref[0:HALO] = (norm_mod(xp_ref[0]) * keep_prev).astype(BF16)
    h_ref[HALO:HALO + tm] = norm_mod(x_ref[0]).astype(BF16)
    h_ref[HALO + tm:] = (norm_mod(xn_ref[0]) * keep_next).astype(BF16)
    acc_ref[...] = jnp.zeros_like(acc_ref)
    rows = tm + 2 * HALO

    def up(c, slot):
        u_ref[slot] = _dot(h_ref[...], wu_ref[c])

    def act_down(c, slot):
        u = u_ref[slot]
        cw = cw_ref[c]
        y = (pltpu.roll(u, 1, 0)[HALO:HALO + tm] * cw[0:1]
             + u[HALO:HALO + tm] * cw[1:2]
             + pltpu.roll(u, rows - 1, 0)[HALO:HALO + tm] * cw[2:3]
             + cw[3:4])
        a = (_silu(y[:, :FF_CHUNK]) * y[:, FF_CHUNK:]).astype(BF16)
        acc_ref[...] += _dot(a, wd_ref[c])

    def two_chunks(j, carry):
        c = 2 * j
        up(c + 1, 1)
        act_down(c, 0)
        up(c + 2, 0)
        act_down(c + 1, 1)
        return carry

    up(0, 0)
    lax.fori_loop(0, N_FF_CHUNKS // 2, two_chunks, 0)
    act_down(N_FF_CHUNKS - 1, 0)
    o_ref[0] = x_ref[0] + mod[5:6] * _rms(acc_ref[...], gpost_ref[...])


def _ffn(x, mod, g_pre, g_post, wu, cw, wd, tm):
    b, s, d = x.shape
    per_batch = mod.shape[0] > 1
    hb = tm // HALO
    n_halo = s // HALO
    return pl.pallas_call(
        functools.partial(_ffn_kernel, tm=tm),
        out_shape=jax.ShapeDtypeStruct((b, s, d), F32),
        grid=(b, s // tm),
        in_specs=[pl.BlockSpec((1, tm, d), lambda bi, i: (bi, i, 0)),
                  pl.BlockSpec((1, HALO, d), lambda bi, i: (bi, jnp.maximum(i * hb - 1, 0), 0)),
                  pl.BlockSpec((1, HALO, d), lambda bi, i: (bi, jnp.minimum((i + 1) * hb, n_halo - 1), 0)),
                  pl.BlockSpec((1, 6, d), (lambda bi, i: (bi, 0, 0)) if per_batch else (lambda bi, i: (0, 0, 0))),
                  _resident((1, d)), _resident((1, d)),
                  _resident(wu.shape), _resident(cw.shape), _resident(wd.shape)],
        out_specs=pl.BlockSpec((1, tm, d), lambda bi, i: (bi, i, 0)),
        scratch_shapes=[pltpu.VMEM((tm + 2 * HALO, d), BF16),
                        pltpu.VMEM((2, tm + 2 * HALO, 2 * FF_CHUNK), F32),
                        pltpu.VMEM((tm, d), F32)],
        compiler_params=_params(2),
        name="ffn",
    )(x, x, x, mod, g_pre.reshape(1, d), g_post.reshape(1, d), wu, cw, wd)


def _na_kernel(q_ref, k_ref, v_ref, kc_ref, vc_ref, bias_ref, o_ref, *, rows):
    kc = kc_ref[0]
    vc = vc_ref[0]
    win = NA_WIN_H * GRID_W
    mask_a = _head_mask(GRID_W, 0)
    vc_ext = jnp.concatenate([vc, jnp.ones_like(vc)], axis=1)
    ones_w = jnp.ones((win, LANES), BF16)

    def group(gi, carry):
        starts, scores, probs = [], [], []
        for i in range(NA_GROUP):
            r = gi * NA_GROUP + i
            r0 = jnp.clip(r - NA_WIN_H // 2, 0, rows - NA_WIN_H)
            q0 = pl.multiple_of(r * GRID_W, GRID_W)
            k0 = pl.multiple_of(r0 * GRID_W, GRID_W)
            q = q_ref[0, pl.ds(q0, GRID_W), :]
            zero = jnp.zeros_like(q)
            q2 = jnp.concatenate([jnp.where(mask_a, q, zero), jnp.where(mask_a, zero, q)], axis=0)
            s = _dot_nt(q2, k_ref[0, pl.ds(k0, win), :]) + bias_ref[0, r - r0]
            sc = _dot_nt(q2, kc)
            starts.append((q0, k0))
            scores.append((s, sc))
        for s, sc in scores:
            m = jnp.maximum(jnp.max(s, axis=-1, keepdims=True), jnp.max(sc, axis=-1, keepdims=True))
            probs.append((jnp.exp2(s - m).astype(BF16), jnp.exp2(sc - m).astype(BF16)))
        for (q0, k0), (p, pc) in zip(starts, probs):
            v_ext = jnp.concatenate([v_ref[0, pl.ds(k0, win), :], ones_w], axis=1)
            o_ext = _dot(p, v_ext) + _dot(pc, vc_ext)
            o2 = o_ext[:, :LANES] / o_ext[:, LANES:]
            o_ref[0, pl.ds(q0, GRID_W), :] = jnp.where(mask_a, o2[:GRID_W], o2[GRID_W:]).astype(BF16)
        return carry

    lax.fori_loop(0, rows // NA_GROUP, group, 0)


def _na(pl_, pc_, bias):
    b, s, _ = pl_.shape
    n_ctx = pc_.shape[1]
    n_pairs = NA_HEADS // 2
    return pl.pallas_call(
        functools.partial(_na_kernel, rows=s // GRID_W),
        out_shape=jax.ShapeDtypeStruct((b, s, NA_WIDTH), BF16),
        grid=(b, n_pairs),
        in_specs=[pl.BlockSpec((1, s, LANES), lambda bi, p: (bi, 0, AB_QA + p)),
                  pl.BlockSpec((1, s, LANES), lambda bi, p: (bi, 0, AB_KA + p)),
                  pl.BlockSpec((1, s, LANES), lambda bi, p: (bi, 0, AB_VA + p)),
                  pl.BlockSpec((1, n_ctx, LANES), lambda bi, p: (bi, 0, AB_KA + p)),
                  pl.BlockSpec((1, n_ctx, LANES), lambda bi, p: (bi, 0, AB_VA + p)),
                  pl.BlockSpec((1, NA_WIN_H, 2 * GRID_W, NA_WIN_H * GRID_W), lambda bi, p: (p, 0, 0, 0))],
        out_specs=pl.BlockSpec((1, s, LANES), lambda bi, p: (bi, 0, p)),
        compiler_params=_params(2),
        name="na",
    )(pl_, pl_, pl_, pc_, pc_, bias)


def _na_bias(rpb):
    h = rpb.shape[0]
    cidx = np.arange(GRID_W)
    c_start = np.clip(cidx - NA_WIN_W // 2, 0, GRID_W - NA_WIN_W)
    col_ok = (cidx[None, :] >= c_start[:, None]) & (cidx[None, :] < c_start[:, None] + NA_WIN_W)
    dc = np.clip(cidx[None, :] - cidx[:, None] + NA_WIN_W - 1, 0, 2 * NA_WIN_W - 2)
    bm = jnp.where(col_ok[None, None], rpb.astype(F32)[:, :, dc] * LOG2E, NEG_INF)
    dr = np.arange(NA_WIN_H)[None, :] - np.arange(NA_WIN_H)[:, None] + NA_WIN_H - 1
    t = bm[:, dr]
    t = t.transpose(0, 1, 3, 2, 4).reshape(h // 2, 2, NA_WIN_H, GRID_W, NA_WIN_H * GRID_W)
    return t.transpose(0, 2, 1, 3, 4).reshape(h // 2, NA_WIN_H, 2 * GRID_W, NA_WIN_H * GRID_W)


def _ctx_attn_kernel(*refs, has_sink):
    if has_sink:
        q_ref, k_ref, v_ref, sink_ref, o_ref = refs
    else:
        q_ref, k_ref, v_ref, o_ref = refs
    q, k, v = q_ref[0], k_ref[0], v_ref[0]
    n = q.shape[0]
    masks = [_head_mask(n, 0), _head_mask(n, 1)]
    outs = []
    for h in range(2):
        qm = jnp.where(masks[h], q, jnp.zeros_like(q))
        s = _dot_nt(qm, k)
        m = jnp.max(s, axis=-1, keepdims=True)
        if has_sink:
            sk = sink_ref[0, h:h + 1, 0:1]
            m = jnp.maximum(m, sk)
        p = jnp.exp2(s - m)
        den = jnp.sum(p, axis=-1, keepdims=True)
        if has_sink:
            den = den + jnp.exp2(sk - m)
        outs.append(_dot(p.astype(BF16), v) / den)
    o_ref[0] = jnp.where(masks[0], outs[0], outs[1]).astype(BF16)


def _ctx_attn(pc_, q_off, k_off, v_off, n_q_tiles, group, sink=None):
    b, n, _ = pc_.shape
    in_specs = [pl.BlockSpec((1, n, LANES), lambda bi, t: (bi, 0, q_off + t)),
                pl.BlockSpec((1, n, LANES), lambda bi, t: (bi, 0, k_off + t // group)),
                pl.BlockSpec((1, n, LANES), lambda bi, t: (bi, 0, v_off + t // group))]
    args = [pc_, pc_, pc_]
    if sink is not None:
        in_specs.append(pl.BlockSpec((1, 2, LANES), lambda bi, t: (t, 0, 0)))
        args.append(sink)
    return pl.pallas_call(
        functools.partial(_ctx_attn_kernel, has_sink=sink is not None),
        out_shape=jax.ShapeDtypeStruct((b, n, n_q_tiles * LANES), BF16),
        grid=(b, n_q_tiles),
        in_specs=in_specs,
        out_specs=pl.BlockSpec((1, n, LANES), lambda bi, t: (bi, 0, t)),
        compiler_params=_params(2),
        name="ctx_attn",
    )(*args)


def _gla_kernel(*refs, n_lat, n_ctx, need_ctx):
    (ql, kl, vl, zl, rl, qc, kc, vc, zc, rc, w2f, w2b, baf, bab, g_ref) = refs[:15]
    if need_ctx:
        yl_ref, yc_ref, of_l, of_c = refs[15:]
    else:
        yl_ref, of_l = refs[15:]
        yc_ref = of_c = None
    ck = GLA_CHUNK
    ri = lax.broadcasted_iota(jnp.int32, (ck, ck), 0)
    ci = lax.broadcasted_iota(jnp.int32, (ck, ck), 1)
    tril, triu = ri >= ci, ri <= ci
    cum_f, cum_b = tril.astype(BF16), triu.astype(BF16)
    masks = [_head_mask(ck, 0), _head_mask(ck, 1)]
    g = g_ref[...]

    def step(refs4, t0, state, fwd):
        q_ref, k_ref, v_ref, z_ref = refs4
        t0 = pl.multiple_of(t0, ck)
        q = q_ref[0, pl.ds(t0, ck), :].astype(F32)
        k = k_ref[0, pl.ds(t0, ck), :].astype(F32)
        v = v_ref[0, pl.ds(t0, ck), :]
        z = z_ref[0, pl.ds(t0, ck), :]
        x = _dot(z, (w2f if fwd else w2b)[...]) + (baf if fwd else bab)[...]
        la = (jnp.minimum(x, 0.0) - jnp.log(1.0 + jnp.exp(-jnp.abs(x)))) * (1.0 / GLA_TAU)
        hi = la.astype(BF16)
        lo = (la - hi.astype(F32)).astype(BF16)
        bb = _dot(cum_f if fwd else cum_b, jnp.concatenate([hi, lo], axis=1))
        b = bb[:, :LANES] + bb[:, LANES:]
        btot = b[ck - 1:ck] if fwd else b[0:1]
        qt = q * jnp.exp(b)
        kt = (k * jnp.exp(-b)).astype(BF16)
        kd = k * jnp.exp(btot - b)
        sb = state.astype(BF16)
        tri = tril if fwd else triu
        outs = []
        for h in range(2):
            qm = jnp.where(masks[h], qt, 0.0).astype(BF16)
            att = jnp.where(tri, _dot_nt(qm, kt), 0.0).astype(BF16)
            outs.append(_dot(att, v[:, h * GLA_DV:(h + 1) * GLA_DV])
                        + _dot(qm, sb[:, h * GLA_DV:(h + 1) * GLA_DV]))
        o = jnp.concatenate(outs, axis=1)
        xt = jnp.concatenate([kd, jnp.broadcast_to(jnp.exp(btot), (ck, LANES))], axis=0).T
        new_state = xt[:, ck:ck + 1] * state + _dot(xt[:, :ck].astype(BF16), v)
        return o, new_state

    def finish(o, r_ref, t0, y_ref):
        t0 = pl.multiple_of(t0, ck)
        r = r_ref[0, pl.ds(t0, ck), :].astype(F32)
        on = jnp.concatenate(
            [o[:, h * GLA_DV:(h + 1) * GLA_DV]
             * lax.rsqrt(jnp.mean(o[:, h * GLA_DV:(h + 1) * GLA_DV] ** 2, axis=-1, keepdims=True) + EPS)
             for h in range(2)], axis=1)
        y_ref[0, pl.ds(t0, ck), :] = (on * g * _silu(r)).astype(BF16)

    lat4, ctx4 = (ql, kl, vl, zl), (qc, kc, vc, zc)
    zero = jnp.zeros((2 * GLA_DK, 2 * GLA_DV), F32)

    def fwd_ctx(n, s):
        o, s = step(ctx4, n * ck, s, True)
        if need_ctx:
            of_c[pl.ds(pl.multiple_of(n * ck, ck), ck), :] = o
        return s

    def fwd_lat(n, s):
        o, s = step(lat4, n * ck, s, True)
        of_l[pl.ds(pl.multiple_of(n * ck, ck), ck), :] = o
        return s

    def bwd_ctx(i, s):
        n = n_ctx - 1 - i
        o, s = step(ctx4, n * ck, s, False)
        if need_ctx:
            finish(o + of_c[pl.ds(pl.multiple_of(n * ck, ck), ck), :], rc, n * ck, yc_ref)
        return s

    def bwd_lat(i, s):
        n = n_lat - 1 - i
        o, s = step(lat4, n * ck, s, False)
        finish(o + of_l[pl.ds(pl.multiple_of(n * ck, ck), ck), :], rl, n * ck, yl_ref)
        return s

    s = lax.fori_loop(0, n_ctx, fwd_ctx, zero)
    lax.fori_loop(0, n_lat, fwd_lat, s)
    s = lax.fori_loop(0, n_ctx, bwd_ctx, zero)
    lax.fori_loop(0, n_lat, bwd_lat, s)


def _gla(pl_, pc_, w2, ba, g, need_ctx):
    b, s, _ = pl_.shape
    n_c = pc_.shape[1]
    n_pairs = GLA_HEADS // 2
    w = 2 * GLA_DV

    def stream(n):
        return [pl.BlockSpec((1, n, LANES), lambda bi, p: (bi, 0, AB_QB + p)),
                pl.BlockSpec((1, n, LANES), lambda bi, p: (bi, 0, AB_KB + p)),
                pl.BlockSpec((1, n, w), lambda bi, p: (bi, 0, AB_VB + p)),
                pl.BlockSpec((1, n, LANES), lambda bi, p: (bi, 0, AB_Z)),
                pl.BlockSpec((1, n, w), lambda bi, p: (bi, 0, AB_RB + p))]

    in_specs = stream(s) + stream(n_c) + [
        pl.BlockSpec((LANES, LANES), lambda bi, p: (0, p)),
        pl.BlockSpec((LANES, LANES), lambda bi, p: (0, n_pairs + p)),
        pl.BlockSpec((1, LANES), lambda bi, p: (0, p)),
        pl.BlockSpec((1, LANES), lambda bi, p: (0, n_pairs + p)),
        pl.BlockSpec((1, w), lambda bi, p: (0, p))]
    out_shape = [jax.ShapeDtypeStruct((b, s, GLA_VW), BF16)]
    out_specs = [pl.BlockSpec((1, s, w), lambda bi, p: (bi, 0, p))]
    scratch = [pltpu.VMEM((s, w), F32)]
    if need_ctx:
        out_shape.append(jax.ShapeDtypeStruct((b, n_c, GLA_VW), BF16))
        out_specs.append(pl.BlockSpec((1, n_c, w), lambda bi, p: (bi, 0, p)))
        scratch.append(pltpu.VMEM((n_c, w), F32))
    res = pl.pallas_call(
        functools.partial(_gla_kernel, n_lat=s // GLA_CHUNK, n_ctx=n_c // GLA_CHUNK, need_ctx=need_ctx),
        out_shape=out_shape,
        grid=(b, n_pairs),
        in_specs=in_specs,
        out_specs=out_specs,
        scratch_shapes=scratch,
        compiler_params=_params(2),
        name="gla",
    )(*([pl_] * 5), *([pc_] * 5), w2, w2, ba, ba, g.reshape(1, GLA_VW))
    return (res[0], res[1]) if need_ctx else (res[0], None)


def _gla_gate_weights(wa2, ba):
    w = jnp.zeros((LANES, 2 * GLA_KW), F32)
    w = w.at[0:GLA_RANK, 0:GLA_KW].set(wa2[0])
    w = w.at[GLA_RANK:2 * GLA_RANK, GLA_KW:].set(wa2[1])
    return w.astype(BF16), ba.reshape(1, 2 * GLA_KW).astype(F32)


def _swa_kernel(q_ref, kp_ref, kn_ref, kx_ref, vp_ref, vn_ref, vx_ref, kc_ref, vc_ref, sink_ref, o_ref):
    n = pl.program_id(1)
    blk = SWA_BLOCK
    n_keys = 3 * blk + kc_ref.shape[1]
    k_all = jnp.concatenate([kp_ref[0], kn_ref[0], kx_ref[0], kc_ref[0]], axis=0)
    v_all = jnp.concatenate([vp_ref[0], vn_ref[0], vx_ref[0], vc_ref[0]], axis=0)
    ones = jnp.ones((n_keys, LANES), BF16)
    qi = lax.broadcasted_iota(jnp.int32, (blk, n_keys), 0)
    kj = lax.broadcasted_iota(jnp.int32, (blk, n_keys), 1)
    has_prev = n > 0
    has_next = n < pl.num_programs(1) - 1
    ok = ((kj >= 3 * blk)
          | ((kj >= blk) & (kj < 2 * blk))
          | ((kj < blk) & (kj >= qi) & has_prev)
          | ((kj >= 2 * blk) & (kj < 3 * blk) & (kj - 2 * blk <= qi) & has_next))
    mask_a = _head_mask(blk, 0)
    half = SWA_GROUP // 2
    n_chunks = (SWA_KV // 2) * 2

    def scores(c):
        pair = c // 2
        tiles = [pair * SWA_GROUP + (c % 2) * half + g for g in range(half)]
        q_parts, sink_parts = [], []
        for t in tiles:
            q = q_ref[0, :, t * LANES:(t + 1) * LANES]
            zero = jnp.zeros_like(q)
            q_parts += [jnp.where(mask_a, q, zero), jnp.where(mask_a, zero, q)]
            sink_parts += [jnp.broadcast_to(sink_ref[t, h:h + 1, 0:1], (blk, 1)) for h in range(2)]
        s = _dot_nt(jnp.concatenate(q_parts, axis=0), k_all[:, pair * LANES:(pair + 1) * LANES])
        s = jnp.concatenate([jnp.where(ok, s[i * blk:(i + 1) * blk], NEG_INF) for i in range(2 * half)], axis=0)
        return tiles, pair, s, jnp.concatenate(sink_parts, axis=0)

    def finish(tiles, pair, s, sk):
        m = jnp.maximum(jnp.max(s, axis=-1, keepdims=True), sk)
        p = jnp.exp2(s - m).astype(BF16)
        o_ext = _dot(p, jnp.concatenate([v_all[:, pair * LANES:(pair + 1) * LANES], ones], axis=1))
        o = o_ext[:, :LANES] / (o_ext[:, LANES:] + jnp.exp2(sk - m))
        for i, t in enumerate(tiles):
            o_ref[0, :, t * LANES:(t + 1) * LANES] = jnp.where(
                mask_a, o[2 * i * blk:(2 * i + 1) * blk], o[(2 * i + 1) * blk:(2 * i + 2) * blk]).astype(BF16)

    pending = scores(0)
    for c in range(n_chunks):
        nxt = scores(c + 1) if c + 1 < n_chunks else None
        finish(*pending)
        pending = nxt


def _swa(pl_, pc_, sink):
    b, s, _ = pl_.shape
    n_c = pc_.shape[1]
    nb = s // SWA_BLOCK
    qw = SWA_HEADS * HEAD_DIM
    kw = SWA_KV * HEAD_DIM
    k_blk, v_blk = (C_K * LANES) // kw, (C_V * LANES) // kw

    def kv(off):
        return [pl.BlockSpec((1, SWA_BLOCK, kw), lambda bi, n: (bi, jnp.maximum(n - 1, 0), off)),
                pl.BlockSpec((1, SWA_BLOCK, kw), lambda bi, n: (bi, n, off)),
                pl.BlockSpec((1, SWA_BLOCK, kw), lambda bi, n: (bi, jnp.minimum(n + 1, nb - 1), off))]

    in_specs = ([pl.BlockSpec((1, SWA_BLOCK, qw), lambda bi, n: (bi, n, 0))] + kv(k_blk) + kv(v_blk)
                + [pl.BlockSpec((1, n_c, kw), lambda bi, n: (bi, 0, k_blk)),
                   pl.BlockSpec((1, n_c, kw), lambda bi, n: (bi, 0, v_blk)),
                   _resident(sink.shape)])
    return pl.pallas_call(
        _swa_kernel,
        out_shape=jax.ShapeDtypeStruct((b, s, qw), BF16),
        grid=(b, nb),
        in_specs=in_specs,
        out_specs=pl.BlockSpec((1, SWA_BLOCK, qw), lambda bi, n: (bi, n, 0)),
        compiler_params=_params(2),
        name="swa",
    )(pl_, pl_, pl_, pl_, pl_, pl_, pl_, pc_, pc_, sink)


def _swa_head_perm():
    perm = np.zeros(SWA_HEADS * HEAD_DIM, np.int32)
    d = np.arange(HEAD_DIM)
    for p in range(SWA_KV // 2):
        for gi in range(SWA_GROUP):
            for hh in range(2):
                perm[(p * SWA_GROUP + gi) * LANES + hh * HEAD_DIM + d] = ((2 * p + hh) * SWA_GROUP + gi) * HEAD_DIM + d
    return perm


def _rope_tables(n):
    t = jnp.arange(n)
    row = (t // GRID_W).astype(F32)
    col = (t % GRID_W).astype(F32)
    half = HEAD_DIM // 2
    inv = 1.0 / (ROPE_BASE ** (jnp.arange(0, half, 2, dtype=F32) / half))
    ang = jnp.concatenate([row[:, None] * inv, col[:, None] * inv], axis=-1)
    cos = jnp.repeat(jnp.cos(ang), 2, axis=1)
    sin = jnp.repeat(jnp.sin(ang), 2, axis=1) * jnp.tile(jnp.array([-1.0, 1.0], F32), half)
    reps = ROPE_TILE // HEAD_DIM
    return jnp.tile(cos, (1, reps)), jnp.tile(sin, (1, reps))


def kernel(x, c, ctx, c_ctx, w_mod, b_mod, g_mix_pre, g_mix_post, g_ffn_pre, g_ffn_post, w_out, w_up, conv_w,
           conv_b, w_down, w_in_ab, na_rpb, gla_wa2, gla_ba, gla_g, w_in_c, swa_sink):
    bsz, seq, d = x.shape
    n_ctx = ctx.shape[1]
    depth = w_mod.shape[0]
    tm_l = 512 if seq % 512 == 0 else 256
    tm_c = n_ctx

    c_all = jnp.zeros((8, d), F32).at[:bsz].set(c).at[bsz].set(c_ctx)
    mods = _modulation(c_all, w_mod, b_mod).reshape(depth, 8, 6, d)
    rope = _rope_tables(seq)
    perm = _swa_head_perm()

    col_scale_ab = np.ones((IN_AB,), np.float32)
    col_scale_ab[0:NA_WIDTH] = ATTN_SCALE
    col_scale_ab[3 * NA_WIDTH:3 * NA_WIDTH + GLA_KW] = GLA_DK ** -0.5

    xl, xc = x, ctx
    for i in range(depth):
        need_ctx = i < depth - 1
        j = i // 2
        mod_l = mods[i, :bsz]
        mod_c = mods[i, bsz:bsz + 1]
        if i % 2 == 0:
            w_in = jnp.pad(w_in_ab[j] * col_scale_ab, ((0, 0), (0, IN_AB_PAD - IN_AB))).astype(BF16)
            p_l = _proj(xl, mod_l, g_mix_pre[i], w_in, tm_l, 640)
            p_c = _proj(xc, mod_c, g_mix_pre[i], w_in, tm_c, 640)
            w2, ba = _gla_gate_weights(gla_wa2[j], gla_ba[j])
            y_na = _na(p_l, p_c, _na_bias(na_rpb[j]))
            y_gla, yc_gla = _gla(p_l, p_c, w2, ba, gla_g[j], need_ctx)
            ys_l = [y_na, y_gla]
            ws = [w_out[i][:NA_WIDTH].astype(BF16), w_out[i][NA_WIDTH:].astype(BF16)]
            if need_ctx:
                ys_c = [_ctx_attn(p_c, AB_QA, AB_KA, AB_VA, NA_HEADS // 2, 1), yc_gla]
        else:
            wq = w_in_c[j][:, :SWA_HEADS * HEAD_DIM][:, perm] * ATTN_SCALE
            w_in = jnp.concatenate([wq, w_in_c[j][:, SWA_HEADS * HEAD_DIM:]], axis=1).astype(BF16)
            p_l = _proj(xl, mod_l, g_mix_pre[i], w_in, tm_l, ROPE_TILE, rope, (C_V * LANES) // ROPE_TILE)
            p_c = _proj(xc, mod_c, g_mix_pre[i], w_in, tm_c, ROPE_TILE)
            sink = jnp.broadcast_to((swa_sink[j] * LOG2E)[perm[::HEAD_DIM] // HEAD_DIM].reshape(-1, 2, 1),
                                    (SWA_HEADS // 2, 2, LANES)).astype(F32)
            ys_l = [_swa(p_l, p_c, sink)]
            ws = [w_out[i][perm].astype(BF16)]
            if need_ctx:
                ys_c = [_ctx_attn(p_c, C_Q, C_K, C_V, SWA_HEADS // 2, SWA_GROUP, sink)]

        wu = jnp.concatenate([w_up[i][:, :D_FF].reshape(d, N_FF_CHUNKS, FF_CHUNK),
                              w_up[i][:, D_FF:].reshape(d, N_FF_CHUNKS, FF_CHUNK)], axis=2)
        wu = wu.transpose(1, 0, 2).astype(BF16)
        cwb = jnp.concatenate([conv_w[i], conv_b[i][None]], axis=0)
        cw = jnp.concatenate([cwb[:, :D_FF].reshape(4, N_FF_CHUNKS, FF_CHUNK),
                              cwb[:, D_FF:].reshape(4, N_FF_CHUNKS, FF_CHUNK)], axis=2).transpose(1, 0, 2)
        wd = w_down[i].reshape(N_FF_CHUNKS, FF_CHUNK, d).astype(BF16)

        xl = _outproj(ys_l, ws, xl, mod_l, g_mix_post[i], tm_l)
        xl = _ffn(xl, mod_l, g_ffn_pre[i], g_ffn_post[i], wu, cw, wd, tm_l)
        if need_ctx:
            xc = _outproj(ys_c, ws, xc, mod_c, g_mix_post[i], tm_c)
            xc = _ffn(xc, mod_c, g_ffn_pre[i], g_ffn_post[i], wu, cw, wd, tm_c)
    return xl
```

```python
import functools

import numpy as np
import jax
import jax.numpy as jnp
from jax import lax
from jax.experimental import pallas as pl
from jax.experimental.pallas import tpu as pltpu

F32 = jnp.float32
BF16 = jnp.bfloat16

LANES = 128
HEAD_DIM = 64
GRID_W = 64
D_MODEL = 1024
EPS = 1e-6
NEG_INF = -1e30
LOG2E = float(np.log2(np.e))
ATTN_SCALE = HEAD_DIM ** -0.5 * LOG2E

NA_HEADS = 8
NA_WIN_H = 8
NA_WIN_W = 16
NA_WIDTH = NA_HEADS * HEAD_DIM
NA_GROUP = 4
GLA_HEADS = 4
GLA_DK = 64
GLA_DV = 128
GLA_KW = GLA_HEADS * GLA_DK
GLA_VW = GLA_HEADS * GLA_DV
GLA_RANK = 16
GLA_TAU = 16.0
GLA_CHUNK = 64
GLA_BLOCK = 256
IN_AB = 3 * NA_WIDTH + 2 * GLA_KW + 2 * GLA_VW + 2 * GLA_RANK
IN_AB_PAD = 3200
SWA_HEADS = 16
SWA_KV = 4
SWA_GROUP = SWA_HEADS // SWA_KV
SWA_BLOCK = 128
IN_C = (SWA_HEADS + 2 * SWA_KV) * HEAD_DIM
D_FF = 2816
FF_CHUNK = 256
N_FF_CHUNKS = D_FF // FF_CHUNK
assert N_FF_CHUNKS * FF_CHUNK == D_FF and N_FF_CHUNKS % 2 == 1
HALO = 16
ROPE_BASE = 10000.0
ROPE_TILE = 256
VMEM_LIMIT = 56 * 1024 * 1024

AB_QA, AB_KA, AB_VA = 0, 4, 8
AB_QB, AB_KB = 12, 14
AB_VB, AB_RB = 8, 10
AB_Z = 24
C_Q, C_K, C_V = 0, 8, 10


def _dot(a, b):
    return jnp.dot(a, b, preferred_element_type=F32)


def _dot_nt(a, b):
    return lax.dot_general(a, b, (((1,), (1,)), ((), ())), preferred_element_type=F32)


def _rms(x, g):
    return x * lax.rsqrt(jnp.mean(x * x, axis=-1, keepdims=True) + EPS) * g


def _silu(x):
    return x / (1.0 + jnp.exp(-x))


def _params(n_grid):
    return pltpu.CompilerParams(dimension_semantics=("arbitrary",) * n_grid,
                                vmem_limit_bytes=VMEM_LIMIT)


def _resident(shape):
    nd = len(shape)
    return pl.BlockSpec(shape, lambda *_: (0,) * nd, pipeline_mode=pl.Buffered(1))


def _head_mask(rows, h):
    lane = lax.broadcasted_iota(jnp.int32, (rows, LANES), 1)
    return (lane < HEAD_DIM) if h == 0 else (lane >= HEAD_DIM)


def _mod_kernel(c_ref, w_ref, b_ref, o_ref):
    s = _silu(c_ref[...]).astype(BF16)
    o_ref[0] = _dot(s, w_ref[0].astype(BF16)) + b_ref[0]


def _modulation(c_all, w_mod, b_mod):
    depth, d, n = w_mod.shape
    tn = 1536
    return pl.pallas_call(
        _mod_kernel,
        out_shape=jax.ShapeDtypeStruct((depth, 8, n), F32),
        grid=(depth, n // tn),
        in_specs=[pl.BlockSpec((8, d), lambda i, j: (0, 0)),
                  pl.BlockSpec((1, d, tn), lambda i, j: (i, 0, j)),
                  pl.BlockSpec((1, 1, tn), lambda i, j: (i, 0, j))],
        out_specs=pl.BlockSpec((1, 8, tn), lambda i, j: (i, 0, j)),
        compiler_params=_params(2),
        name="modulation",
    )(c_all, w_mod, b_mod.reshape(depth, 1, n))


def _rope(acc, cos, sin):
    n = acc.shape[1]
    lane = lax.broadcasted_iota(jnp.int32, acc.shape, 1)
    partner = jnp.where(lane % 2 == 0, pltpu.roll(acc, n - 1, 1), pltpu.roll(acc, 1, 1))
    return acc * cos + partner * sin


def _proj_kernel(*refs, chunk, rope_chunks, shift_idx):
    if rope_chunks:
        x_ref, mod_ref, g_ref, w_ref, cos_ref, sin_ref, o_ref = refs
    else:
        x_ref, mod_ref, g_ref, w_ref, o_ref = refs
    mod = mod_ref[0]
    h = _rms(x_ref[0], g_ref[...]) * (1.0 + mod[shift_idx + 1:shift_idx + 2]) + mod[shift_idx:shift_idx + 1]
    hb = h.astype(BF16)
    for c in range(w_ref.shape[1] // chunk):
        acc = _dot(hb, w_ref[:, c * chunk:(c + 1) * chunk])
        if c < rope_chunks:
            acc = _rope(acc, cos_ref[...], sin_ref[...])
        o_ref[0, :, c * chunk:(c + 1) * chunk] = acc.astype(BF16)


def _proj(x, mod, g, w, tm, chunk, rope=None, rope_chunks=0):
    b, s, d = x.shape
    n = w.shape[1]
    per_batch = mod.shape[0] > 1
    in_specs = [pl.BlockSpec((1, tm, d), lambda bi, i: (bi, i, 0)),
                pl.BlockSpec((1, 6, d), (lambda bi, i: (bi, 0, 0)) if per_batch else (lambda bi, i: (0, 0, 0))),
                _resident((1, d)),
                _resident((d, n))]
    args = [x, mod, g.reshape(1, d), w]
    if rope_chunks:
        in_specs += [pl.BlockSpec((tm, chunk), lambda bi, i: (i, 0))] * 2
        args += list(rope)
    return pl.pallas_call(
        functools.partial(_proj_kernel, chunk=chunk, rope_chunks=rope_chunks, shift_idx=0),
        out_shape=jax.ShapeDtypeStruct((b, s, n), BF16),
        grid=(b, s // tm),
        in_specs=in_specs,
        out_specs=pl.BlockSpec((1, tm, n), lambda bi, i: (bi, i, 0)),
        compiler_params=_params(2),
        name="proj",
    )(*args)


def _outproj_kernel(*refs, n_parts):
    y_refs, w_refs = refs[:n_parts], refs[n_parts:2 * n_parts]
    x_ref, mod_ref, g_ref, o_ref = refs[2 * n_parts:]
    acc = _dot(y_refs[0][0], w_refs[0][...])
    for yr, wr in zip(y_refs[1:], w_refs[1:]):
        acc = acc + _dot(yr[0], wr[...])
    o_ref[0] = x_ref[0] + mod_ref[0][2:3] * _rms(acc, g_ref[...])


def _outproj(ys, ws, x, mod, g, tm):
    b, s, d = x.shape
    per_batch = mod.shape[0] > 1
    in_specs = [pl.BlockSpec((1, tm, y.shape[2]), lambda bi, i: (bi, i, 0)) for y in ys]
    in_specs += [_resident(w.shape) for w in ws]
    in_specs += [pl.BlockSpec((1, tm, d), lambda bi, i: (bi, i, 0)),
                 pl.BlockSpec((1, 6, d), (lambda bi, i: (bi, 0, 0)) if per_batch else (lambda bi, i: (0, 0, 0))),
                 _resident((1, d))]
    return pl.pallas_call(
        functools.partial(_outproj_kernel, n_parts=len(ys)),
        out_shape=jax.ShapeDtypeStruct((b, s, d), F32),
        grid=(b, s // tm),
        in_specs=in_specs,
        out_specs=pl.BlockSpec((1, tm, d), lambda bi, i: (bi, i, 0)),
        compiler_params=_params(2),
        name="outproj",
    )(*ys, *ws, x, mod, g.reshape(1, d))


def _ffn_kernel(x_ref, xp_ref, xn_ref, mod_ref, gpre_ref, gpost_ref, wu_ref, cw_ref, wd_ref, o_ref,
                h_ref, u_ref, acc_ref, *, tm):
    i = pl.program_id(1)
    mod = mod_ref[0]
    gpre = gpre_ref[...]

    def norm_mod(x):
        return _rms(x, gpre) * (1.0 + mod[4:5]) + mod[3:4]

    keep_prev = (i > 0).astype(F32)
    keep_next = (i < pl.num_programs(1) - 1).astype(F32)
    h_ref[0:HALO] = (norm_mod(xp_ref[0]) * keep_prev).astype(BF16)
    h_ref[HALO:HALO + tm] = norm_mod(x_ref[0]).astype(BF16)
    h_ref[HALO + tm:] = (norm_mod(xn_ref[0]) * keep_next).astype(BF16)
    acc_ref[...] = jnp.zeros_like(acc_ref)
    rows = tm + 2 * HALO

    def up(c, slot):
        h = h_ref[...]
        for part in range(2):
            col = pl.multiple_of(part * D_FF + c * FF_CHUNK, FF_CHUNK)
            u_ref[slot, :, part * FF_CHUNK:(part + 1) * FF_CHUNK] = _dot(h, wu_ref[:, pl.ds(col, FF_CHUNK)])

    def act_down(c, slot):
        u = u_ref[slot]
        cw = cw_ref[c]
        y = (pltpu.roll(u, 1, 0)[HALO:HALO + tm] * cw[0:1]
             + u[HALO:HALO + tm] * cw[1:2]
             + pltpu.roll(u, rows - 1, 0)[HALO:HALO + tm] * cw[2:3]
             + cw[3:4])
        a = (_silu(y[:, :FF_CHUNK]) * y[:, FF_CHUNK:]).astype(BF16)
        acc_ref[...] += _dot(a, wd_ref[c])

    def two_chunks(j, carry):
        c = 2 * j
        up(c + 1, 1)
        act_down(c, 0)
        up(c + 2, 0)
        act_down(c + 1, 1)
        return carry

    up(0, 0)
    lax.fori_loop(0, N_FF_CHUNKS // 2, two_chunks, 0)
    act_down(N_FF_CHUNKS - 1, 0)
    o_ref[0] = x_ref[0] + mod[5:6] * _rms(acc_ref[...], gpost_ref[...])


def _ffn(x, mod, g_pre, g_post, wu, cw, wd, tm):
    b, s, d = x.shape
    per_batch = mod.shape[0] > 1
    hb = tm // HALO
    n_halo = s // HALO
    return pl.pallas_call(
        functools.partial(_ffn_kernel, tm=tm),
        out_shape=jax.ShapeDtypeStruct((b, s, d), F32),
        grid=(b, s // tm),
        in_specs=[pl.BlockSpec((1, tm, d), lambda bi, i: (bi, i, 0)),
                  pl.BlockSpec((1, HALO, d), lambda bi, i: (bi, jnp.maximum(i * hb - 1, 0), 0)),
                  pl.BlockSpec((1, HALO, d), lambda bi, i: (bi, jnp.minimum((i + 1) * hb, n_halo - 1), 0)),
                  pl.BlockSpec((1, 6, d), (lambda bi, i: (bi, 0, 0)) if per_batch else (lambda bi, i: (0, 0, 0))),
                  _resident((1, d)), _resident((1, d)),
                  _resident(wu.shape), _resident(cw.shape), _resident(wd.shape)],
        out_specs=pl.BlockSpec((1, tm, d), lambda bi, i: (bi, i, 0)),
        scratch_shapes=[pltpu.VMEM((tm + 2 * HALO, d), BF16),
                        pltpu.VMEM((2, tm + 2 * HALO, 2 * FF_CHUNK), F32),
                        pltpu.VMEM((tm, d), F32)],
        compiler_params=_params(2),
        name="ffn",
    )(x, x, x, mod, g_pre.reshape(1, d), g_post.reshape(1, d), wu, cw, wd)


def _na_kernel(q_ref, k_ref, v_ref, kc_ref, vc_ref, bias_ref, o_ref, *, rows):
    kc = kc_ref[0]
    vc = vc_ref[0]
    win = NA_WIN_H * GRID_W
    mask_a = _head_mask(GRID_W, 0)
    vc_ext = jnp.concatenate([vc, jnp.ones_like(vc)], axis=1)
    ones_w = jnp.ones((win, LANES), BF16)

    def group(gi, carry):
        starts, scores, probs = [], [], []
        for i in range(NA_GROUP):
            r = gi * NA_GROUP + i
            r0 = jnp.clip(r - NA_WIN_H // 2, 0, rows - NA_WIN_H)
            q0 = pl.multiple_of(r * GRID_W, GRID_W)
            k0 = pl.multiple_of(r0 * GRID_W, GRID_W)
            q = q_ref[0, pl.ds(q0, GRID_W), :]
            zero = jnp.zeros_like(q)
            q2 = jnp.concatenate([jnp.where(mask_a, q, zero), jnp.where(mask_a, zero, q)], axis=0)
            s = _dot_nt(q2, k_ref[0, pl.ds(k0, win), :]) + bias_ref[0, r - r0]
            sc = _dot_nt(q2, kc)
            starts.append((q0, k0))
            scores.append((s, sc))
        for s, sc in scores:
            m = jnp.maximum(jnp.max(s, axis=-1, keepdims=True), jnp.max(sc, axis=-1, keepdims=True))
            probs.append((jnp.exp2(s - m).astype(BF16), jnp.exp2(sc - m).astype(BF16)))
        for (q0, k0), (p, pc) in zip(starts, probs):
            v_ext = jnp.concatenate([v_ref[0, pl.ds(k0, win), :], ones_w], axis=1)
            o_ext = _dot(p, v_ext) + _dot(pc, vc_ext)
            o2 = o_ext[:, :LANES] / o_ext[:, LANES:]
            o_ref[0, pl.ds(q0, GRID_W), :] = jnp.where(mask_a, o2[:GRID_W], o2[GRID_W:]).astype(BF16)
        return carry

    lax.fori_loop(0, rows // NA_GROUP, group, 0)


def _na(pl_, pc_, bias):
    b, s, _ = pl_.shape
    n_ctx = pc_.shape[1]
    n_pairs = NA_HEADS // 2
    return pl.pallas_call(
        functools.partial(_na_kernel, rows=s // GRID_W),
        out_shape=jax.ShapeDtypeStruct((b, s, NA_WIDTH), BF16),
        grid=(b, n_pairs),
        in_specs=[pl.BlockSpec((1, s, LANES), lambda bi, p: (bi, 0, AB_QA + p)),
                  pl.BlockSpec((1, s, LANES), lambda bi, p: (bi, 0, AB_KA + p)),
                  pl.BlockSpec((1, s, LANES), lambda bi, p: (bi, 0, AB_VA + p)),
                  pl.BlockSpec((1, n_ctx, LANES), lambda bi, p: (bi, 0, AB_KA + p)),
                  pl.BlockSpec((1, n_ctx, LANES), lambda bi, p: (bi, 0, AB_VA + p)),
                  pl.BlockSpec((1, NA_WIN_H, 2 * GRID_W, NA_WIN_H * GRID_W), lambda bi, p: (p, 0, 0, 0))],
        out_specs=pl.BlockSpec((1, s, LANES), lambda bi, p: (bi, 0, p)),
        compiler_params=_params(2),
        name="na",
    )(pl_, pl_, pl_, pc_, pc_, bias)


def _na_bias(rpb):
    h = rpb.shape[0]
    cidx = np.arange(GRID_W)
    c_start = np.clip(cidx - NA_WIN_W // 2, 0, GRID_W - NA_WIN_W)
    col_ok = (cidx[None, :] >= c_start[:, None]) & (cidx[None, :] < c_start[:, None] + NA_WIN_W)
    dc = np.clip(cidx[None, :] - cidx[:, None] + NA_WIN_W - 1, 0, 2 * NA_WIN_W - 2)
    bm = jnp.where(col_ok[None, None], rpb.astype(F32)[:, :, dc] * LOG2E, NEG_INF)
    dr = np.arange(NA_WIN_H)[None, :] - np.arange(NA_WIN_H)[:, None] + NA_WIN_H - 1
    t = bm[:, dr]
    t = t.transpose(0, 1, 3, 2, 4).reshape(h // 2, 2, NA_WIN_H, GRID_W, NA_WIN_H * GRID_W)
    return t.transpose(0, 2, 1, 3, 4).reshape(h // 2, NA_WIN_H, 2 * GRID_W, NA_WIN_H * GRID_W)


def _ctx_attn_kernel(*refs, has_sink):
    if has_sink:
        q_ref, k_ref, v_ref, sink_ref, o_ref = refs
    else:
        q_ref, k_ref, v_ref, o_ref = refs
    q, k, v = q_ref[0], k_ref[0], v_ref[0]
    n = q.shape[0]
    masks = [_head_mask(n, 0), _head_mask(n, 1)]
    outs = []
    for h in range(2):
        qm = jnp.where(masks[h], q, jnp.zeros_like(q))
        s = _dot_nt(qm, k)
        m = jnp.max(s, axis=-1, keepdims=True)
        if has_sink:
            sk = sink_ref[0, h:h + 1, 0:1]
            m = jnp.maximum(m, sk)
        p = jnp.exp2(s - m)
        den = jnp.sum(p, axis=-1, keepdims=True)
        if has_sink:
            den = den + jnp.exp2(sk - m)
        outs.append(_dot(p.astype(BF16), v) / den)
    o_ref[0] = jnp.where(masks[0], outs[0], outs[1]).astype(BF16)


def _ctx_attn(pc_, q_off, k_off, v_off, n_q_tiles, group, sink=None):
    b, n, _ = pc_.shape
    in_specs = [pl.BlockSpec((1, n, LANES), lambda bi, t: (bi, 0, q_off + t)),
                pl.BlockSpec((1, n, LANES), lambda bi, t: (bi, 0, k_off + t // group)),
                pl.BlockSpec((1, n, LANES), lambda bi, t: (bi, 0, v_off + t // group))]
    args = [pc_, pc_, pc_]
    if sink is not None:
        in_specs.append(pl.BlockSpec((1, 2, LANES), lambda bi, t: (t, 0, 0)))
        args.append(sink)
    return pl.pallas_call(
        functools.partial(_ctx_attn_kernel, has_sink=sink is not None),
        out_shape=jax.ShapeDtypeStruct((b, n, n_q_tiles * LANES), BF16),
        grid=(b, n_q_tiles),
        in_specs=in_specs,
        out_specs=pl.BlockSpec((1, n, LANES), lambda bi, t: (bi, 0, t)),
        compiler_params=_params(2),
        name="ctx_attn",
    )(*args)


def _gla_kernel(*refs, n_lat_blocks, need_ctx):
    (ql, kl, vl, zl, rl, qc, kc, vc, zc, rc, w2f, w2b, baf, bab, g_ref) = refs[:15]
    n_out = 2 if need_ctx else 1
    yl_ref = refs[15]
    yc_ref = refs[16] if need_ctx else None
    qt_s, kd_s, dec_s, vt_s, o_s = refs[15 + n_out:]
    blk, ck = GLA_BLOCK, GLA_CHUNK
    cpb = blk // ck
    shift = ck.bit_length() - 1
    ri = lax.broadcasted_iota(jnp.int32, (blk, blk), 0)
    ci = lax.broadcasted_iota(jnp.int32, (blk, blk), 1)
    same = lax.shift_right_logical(ri, shift) == lax.shift_right_logical(ci, shift)
    att_mask = [same & (ci <= ri), same & (ci >= ri)]
    as_bf16 = lambda m: jnp.where(m, 1.0, 0.0).astype(BF16)
    cum = [jnp.concatenate([as_bf16(att_mask[d]), as_bf16(same)], axis=0) for d in range(2)]
    head = [_head_mask(blk, 0), _head_mask(blk, 1)]
    head_ck = [_head_mask(ck, 0), _head_mask(ck, 1)]
    tok_lo = lax.broadcasted_iota(jnp.int32, (2 * GLA_DV, LANES), 1) < ck
    w2 = jnp.concatenate([w2f[...], w2b[...]], axis=1)
    ba = jnp.concatenate([baf[...], bab[...]], axis=1)
    g = g_ref[...]
    lat4, ctx4 = (ql, kl, vl, zl), (qc, kc, vc, zc)

    def phase1(blocks):
        gates = []
        for (_, _, _, z_ref), t_in, _ in blocks:
            x = _dot(z_ref[0, pl.ds(t_in, blk), :], w2) + ba
            la = (jnp.minimum(x, 0.0) - jnp.log(1.0 + jnp.exp(-jnp.abs(x)))) * (1.0 / GLA_TAU)
            hi = la.astype(BF16)
            gates.append((hi, (la - hi.astype(F32)).astype(BF16)))
        cums = []
        for hi, lo in gates:
            per_dir = []
            for d in range(2):
                c2 = _dot(cum[d], jnp.concatenate([hi[:, d * LANES:(d + 1) * LANES],
                                                   lo[:, d * LANES:(d + 1) * LANES]], axis=1))
                per_dir.append((c2[:blk, :LANES] + c2[:blk, LANES:], c2[blk:, :LANES] + c2[blk:, LANES:]))
            cums.append(per_dir)
        scores = []
        for ((q_ref, k_ref, _, _), t_in, bo), per_dir in zip(blocks, cums):
            q = q_ref[0, pl.ds(t_in, blk), :].astype(F32)
            k = k_ref[0, pl.ds(t_in, blk), :].astype(F32)
            t_out = pl.multiple_of(bo * blk, blk)
            items = []
            for d, (b, tot) in enumerate(per_dir):
                qt = q * jnp.exp(b)
                kt = (k * jnp.exp(-b)).astype(BF16)
                qt_s[d, pl.ds(t_out, blk), :] = qt.astype(BF16)
                kd_s[d, pl.ds(t_out, blk), :] = (k * jnp.exp(tot - b)).astype(BF16)
                for j in range(cpb):
                    dec_s[d, pl.ds(bo * cpb + j, 1), :] = jnp.exp(tot[j * ck:j * ck + 1])
                for h in range(2):
                    items.append((d, h, _dot_nt(jnp.where(head[h], qt, 0.0).astype(BF16), kt)))
            scores.append(items)
        for ((_, _, v_ref, _), t_in, bo), items in zip(blocks, scores):
            v = v_ref[0, pl.ds(t_in, blk), :]
            oi = [None, None]
            for d, h, s in items:
                c = _dot(jnp.where(att_mask[d], s, 0.0).astype(BF16), v[:, h * GLA_DV:(h + 1) * GLA_DV])
                oi[h] = c if oi[h] is None else oi[h] + c
            o_s[pl.ds(pl.multiple_of(bo * blk, blk), blk), :] = jnp.concatenate(oi, axis=1)
            vt_s[bo] = v.astype(F32).T.astype(BF16)

    def scan_block(bo, st, d, r_ref=None, t_in=None, y_ref=None):
        t0 = pl.multiple_of(bo * blk, blk)
        vt = vt_s[bo]
        kd = kd_s[d, pl.ds(t0, blk), :]
        order = list(range(cpb)) if d == 0 else list(range(cpb - 1, -1, -1))
        upd = {}
        for c in order:
            half = c // 2
            vt_c = vt[:, half * LANES:(half + 1) * LANES]
            vt_c = jnp.where(tok_lo if c % 2 == 0 else ~tok_lo, vt_c, jnp.zeros_like(vt_c))
            upd[c] = _dot(vt_c, kd[half * LANES:(half + 1) * LANES])
        for c in order:
            tc = pl.multiple_of(t0 + c * ck, ck)
            qt = qt_s[d, pl.ds(tc, ck), :]
            stb = st.astype(BF16)
            outs = [_dot_nt(jnp.where(head_ck[h], qt, jnp.zeros_like(qt)), stb[h * GLA_DV:(h + 1) * GLA_DV])
                    for h in range(2)]
            o_s[pl.ds(tc, ck), :] += jnp.concatenate(outs, axis=1)
            st = dec_s[d, pl.ds(bo * cpb + c, 1), :] * st + upd[c]
        if y_ref is not None:
            o = o_s[pl.ds(t0, blk), :]
            r = r_ref[0, pl.ds(t_in, blk), :].astype(F32)
            on = jnp.concatenate(
                [o[:, h * GLA_DV:(h + 1) * GLA_DV]
                 * lax.rsqrt(jnp.mean(o[:, h * GLA_DV:(h + 1) * GLA_DV] ** 2, axis=-1, keepdims=True) + EPS)
                 for h in range(2)], axis=1)
            y_ref[0, pl.ds(t_in, blk), :] = (on * g * _silu(r)).astype(BF16)
        return st

    phase1([(ctx4, 0, 0)])

    def phase1_lat(i, carry):
        phase1([(lat4, pl.multiple_of((2 * i + j) * blk, blk), 1 + 2 * i + j) for j in range(2)])
        return carry

    lax.fori_loop(0, n_lat_blocks // 2, phase1_lat, 0)

    zero = jnp.zeros((2 * GLA_DV, 2 * GLA_DK), F32)
    st = scan_block(0, zero, 0)
    lax.fori_loop(0, n_lat_blocks, lambda i, s: scan_block(1 + i, s, 0), st)
    st = scan_block(0, zero, 1, rc, 0, yc_ref)
    lax.fori_loop(
        0, n_lat_blocks,
        lambda i, s: scan_block(n_lat_blocks - i, s, 1, rl,
                                pl.multiple_of((n_lat_blocks - 1 - i) * blk, blk), yl_ref), st)


def _gla(pl_, pc_, w2, ba, g, need_ctx):
    b, s, _ = pl_.shape
    n_c = pc_.shape[1]
    n_pairs = GLA_HEADS // 2
    w = 2 * GLA_DV

    def stream(n):
        return [pl.BlockSpec((1, n, LANES), lambda bi, p: (bi, 0, AB_QB + p)),
                pl.BlockSpec((1, n, LANES), lambda bi, p: (bi, 0, AB_KB + p)),
                pl.BlockSpec((1, n, w), lambda bi, p: (bi, 0, AB_VB + p)),
                pl.BlockSpec((1, n, LANES), lambda bi, p: (bi, 0, AB_Z)),
                pl.BlockSpec((1, n, w), lambda bi, p: (bi, 0, AB_RB + p))]

    in_specs = stream(s) + stream(n_c) + [
        pl.BlockSpec((LANES, LANES), lambda bi, p: (0, p)),
        pl.BlockSpec((LANES, LANES), lambda bi, p: (0, n_pairs + p)),
        pl.BlockSpec((1, LANES), lambda bi, p: (0, p)),
        pl.BlockSpec((1, LANES), lambda bi, p: (0, n_pairs + p)),
        pl.BlockSpec((1, w), lambda bi, p: (0, p))]
    out_shape = [jax.ShapeDtypeStruct((b, s, GLA_VW), BF16)]
    out_specs = [pl.BlockSpec((1, s, w), lambda bi, p: (bi, 0, p))]
    if need_ctx:
        out_shape.append(jax.ShapeDtypeStruct((b, n_c, GLA_VW), BF16))
        out_specs.append(pl.BlockSpec((1, n_c, w), lambda bi, p: (bi, 0, p)))
    assert n_c == GLA_BLOCK and s % (2 * GLA_BLOCK) == 0
    n_lat_blocks = s // GLA_BLOCK
    t_all = n_c + s
    n_chunks = -(-(t_all // GLA_CHUNK) // 8) * 8
    scratch = [pltpu.VMEM((2, t_all, LANES), BF16),
               pltpu.VMEM((2, t_all, LANES), BF16),
               pltpu.VMEM((2, n_chunks, LANES), F32),
               pltpu.VMEM((1 + n_lat_blocks, w, GLA_BLOCK), BF16),
               pltpu.VMEM((t_all, w), F32)]
    res = pl.pallas_call(
        functools.partial(_gla_kernel, n_lat_blocks=n_lat_blocks, need_ctx=need_ctx),
        out_shape=out_shape,
        grid=(b, n_pairs),
        in_specs=in_specs,
        out_specs=out_specs,
        scratch_shapes=scratch,
        compiler_params=_params(2),
        name="gla",
    )(*([pl_] * 5), *([pc_] * 5), w2, w2, ba, ba, g.reshape(1, GLA_VW))
    return (res[0], res[1]) if need_ctx else (res[0], None)


def _gla_gate_weights(wa2, ba):
    w = jnp.zeros((LANES, 2 * GLA_KW), F32)
    w = w.at[0:GLA_RANK, 0:GLA_KW].set(wa2[0])
    w = w.at[GLA_RANK:2 * GLA_RANK, GLA_KW:].set(wa2[1])
    return w.astype(BF16), ba.reshape(1, 2 * GLA_KW).astype(F32)


def _swa_kernel(q_ref, kp_ref, kn_ref, kx_ref, vp_ref, vn_ref, vx_ref, kc_ref, vc_ref, sink_ref, o_ref):
    n = pl.program_id(1)
    blk = SWA_BLOCK
    n_keys = 3 * blk + kc_ref.shape[1]
    k_all = jnp.concatenate([kp_ref[0], kn_ref[0], kx_ref[0], kc_ref[0]], axis=0)
    v_all = jnp.concatenate([vp_ref[0], vn_ref[0], vx_ref[0], vc_ref[0]], axis=0)
    ones = jnp.ones((n_keys, LANES), BF16)
    qi = lax.broadcasted_iota(jnp.int32, (blk, n_keys), 0)
    kj = lax.broadcasted_iota(jnp.int32, (blk, n_keys), 1)
    has_prev = n > 0
    has_next = n < pl.num_programs(1) - 1
    ok = ((kj >= 3 * blk)
          | ((kj >= blk) & (kj < 2 * blk))
          | ((kj < blk) & (kj >= qi) & has_prev)
          | ((kj >= 2 * blk) & (kj < 3 * blk) & (kj - 2 * blk <= qi) & has_next))
    mask_a = _head_mask(blk, 0)
    half = SWA_GROUP // 2
    n_chunks = (SWA_KV // 2) * 2

    def scores(c):
        pair = c // 2
        tiles = [pair * SWA_GROUP + (c % 2) * half + g for g in range(half)]
        q_parts, sink_parts = [], []
        for t in tiles:
            q = q_ref[0, :, t * LANES:(t + 1) * LANES]
            zero = jnp.zeros_like(q)
            q_parts += [jnp.where(mask_a, q, zero), jnp.where(mask_a, zero, q)]
            sink_parts += [jnp.broadcast_to(sink_ref[t, h:h + 1, 0:1], (blk, 1)) for h in range(2)]
        s = _dot_nt(jnp.concatenate(q_parts, axis=0), k_all[:, pair * LANES:(pair + 1) * LANES])
        s = jnp.concatenate([jnp.where(ok, s[i * blk:(i + 1) * blk], NEG_INF) for i in range(2 * half)], axis=0)
        return tiles, pair, s, jnp.concatenate(sink_parts, axis=0)

    def finish(tiles, pair, s, sk):
        m = jnp.maximum(jnp.max(s, axis=-1, keepdims=True), sk)
        p = jnp.exp2(s - m).astype(BF16)
        o_ext = _dot(p, jnp.concatenate([v_all[:, pair * LANES:(pair + 1) * LANES], ones], axis=1))
        o = o_ext[:, :LANES] / (o_ext[:, LANES:] + jnp.exp2(sk - m))
        for i, t in enumerate(tiles):
            o_ref[0, :, t * LANES:(t + 1) * LANES] = jnp.where(
                mask_a, o[2 * i * blk:(2 * i + 1) * blk], o[(2 * i + 1) * blk:(2 * i + 2) * blk]).astype(BF16)

    pending = scores(0)
    for c in range(n_chunks):
        nxt = scores(c + 1) if c + 1 < n_chunks else None
        finish(*pending)
        pending = nxt


def _swa(pl_, pc_, sink):
    b, s, _ = pl_.shape
    n_c = pc_.shape[1]
    nb = s // SWA_BLOCK
    qw = SWA_HEADS * HEAD_DIM
    kw = SWA_KV * HEAD_DIM
    k_blk, v_blk = (C_K * LANES) // kw, (C_V * LANES) // kw

    def kv(off):
        return [pl.BlockSpec((1, SWA_BLOCK, kw), lambda bi, n: (bi, jnp.maximum(n - 1, 0), off)),
                pl.BlockSpec((1, SWA_BLOCK, kw), lambda bi, n: (bi, n, off)),
                pl.BlockSpec((1, SWA_BLOCK, kw), lambda bi, n: (bi, jnp.minimum(n + 1, nb - 1), off))]

    in_specs = ([pl.BlockSpec((1, SWA_BLOCK, qw), lambda bi, n: (bi, n, 0))] + kv(k_blk) + kv(v_blk)
                + [pl.BlockSpec((1, n_c, kw), lambda bi, n: (bi, 0, k_blk)),
                   pl.BlockSpec((1, n_c, kw), lambda bi, n: (bi, 0, v_blk)),
                   _resident(sink.shape)])
    return pl.pallas_call(
        _swa_kernel,
        out_shape=jax.ShapeDtypeStruct((b, s, qw), BF16),
        grid=(b, nb),
        in_specs=in_specs,
        out_specs=pl.BlockSpec((1, SWA_BLOCK, qw), lambda bi, n: (bi, n, 0)),
        compiler_params=_params(2),
        name="swa",
    )(pl_, pl_, pl_, pl_, pl_, pl_, pl_, pc_, pc_, sink)


def _swa_head_perm():
    perm = np.zeros(SWA_HEADS * HEAD_DIM, np.int32)
    d = np.arange(HEAD_DIM)
    for p in range(SWA_KV // 2):
        for gi in range(SWA_GROUP):
            for hh in range(2):
                perm[(p * SWA_GROUP + gi) * LANES + hh * HEAD_DIM + d] = ((2 * p + hh) * SWA_GROUP + gi) * HEAD_DIM + d
    return perm


def _rope_tables(n):
    t = jnp.arange(n)
    row = (t // GRID_W).astype(F32)
    col = (t % GRID_W).astype(F32)
    half = HEAD_DIM // 2
    inv = 1.0 / (ROPE_BASE ** (jnp.arange(0, half, 2, dtype=F32) / half))
    ang = jnp.concatenate([row[:, None] * inv, col[:, None] * inv], axis=-1)
    cos = jnp.repeat(jnp.cos(ang), 2, axis=1)
    sin = jnp.repeat(jnp.sin(ang), 2, axis=1) * jnp.tile(jnp.array([-1.0, 1.0], F32), half)
    reps = ROPE_TILE // HEAD_DIM
    return jnp.tile(cos, (1, reps)), jnp.tile(sin, (1, reps))


def kernel(x, c, ctx, c_ctx, w_mod, b_mod, g_mix_pre, g_mix_post, g_ffn_pre, g_ffn_post, w_out, w_up, conv_w,
           conv_b, w_down, w_in_ab, na_rpb, gla_wa2, gla_ba, gla_g, w_in_c, swa_sink):
    bsz, seq, d = x.shape
    n_ctx = ctx.shape[1]
    depth = w_mod.shape[0]
    tm_l = 512 if seq % 512 == 0 else 256
    tm_c = n_ctx

    c_all = jnp.zeros((8, d), F32).at[:bsz].set(c).at[bsz].set(c_ctx)
    mods = _modulation(c_all, w_mod, b_mod).reshape(depth, 8, 6, d)
    rope = _rope_tables(seq)
    perm = _swa_head_perm()

    col_scale_ab = np.ones((IN_AB,), np.float32)
    col_scale_ab[0:NA_WIDTH] = ATTN_SCALE
    col_scale_ab[3 * NA_WIDTH:3 * NA_WIDTH + GLA_KW] = GLA_DK ** -0.5

    xl, xc = x, ctx
    for i in range(depth):
        need_ctx = i < depth - 1
        j = i // 2
        mod_l = mods[i, :bsz]
        mod_c = mods[i, bsz:bsz + 1]
        if i % 2 == 0:
            w_in = jnp.pad(w_in_ab[j] * col_scale_ab, ((0, 0), (0, IN_AB_PAD - IN_AB))).astype(BF16)
            p_l = _proj(xl, mod_l, g_mix_pre[i], w_in, tm_l, 640)
            p_c = _proj(xc, mod_c, g_mix_pre[i], w_in, tm_c, 640)
            w2, ba = _gla_gate_weights(gla_wa2[j], gla_ba[j])
            y_na = _na(p_l, p_c, _na_bias(na_rpb[j]))
            y_gla, yc_gla = _gla(p_l, p_c, w2, ba, gla_g[j], need_ctx)
            ys_l = [y_na, y_gla]
            ws = [w_out[i][:NA_WIDTH].astype(BF16), w_out[i][NA_WIDTH:].astype(BF16)]
            if need_ctx:
                ys_c = [_ctx_attn(p_c, AB_QA, AB_KA, AB_VA, NA_HEADS // 2, 1), yc_gla]
        else:
            wq = w_in_c[j][:, :SWA_HEADS * HEAD_DIM][:, perm] * ATTN_SCALE
            w_in = jnp.concatenate([wq, w_in_c[j][:, SWA_HEADS * HEAD_DIM:]], axis=1).astype(BF16)
            p_l = _proj(xl, mod_l, g_mix_pre[i], w_in, tm_l, ROPE_TILE, rope, (C_V * LANES) // ROPE_TILE)
            p_c = _proj(xc, mod_c, g_mix_pre[i], w_in, tm_c, ROPE_TILE)
            sink = jnp.broadcast_to((swa_sink[j] * LOG2E)[perm[::HEAD_DIM] // HEAD_DIM].reshape(-1, 2, 1),
                                    (SWA_HEADS // 2, 2, LANES)).astype(F32)
            ys_l = [_swa(p_l, p_c, sink)]
            ws = [w_out[i][perm].astype(BF16)]
            if need_ctx:
                ys_c = [_ctx_attn(p_c, C_Q, C_K, C_V, SWA_HEADS // 2, SWA_GROUP, sink)]

        wu = w_up[i].astype(BF16)
        cwb = jnp.concatenate([conv_w[i], conv_b[i][None]], axis=0)
        cw = jnp.concatenate([cwb[:, :D_FF].reshape(4, N_FF_CHUNKS, FF_CHUNK),
                              cwb[:, D_FF:].reshape(4, N_FF_CHUNKS, FF_CHUNK)], axis=2).transpose(1, 0, 2)
        wd = w_down[i].reshape(N_FF_CHUNKS, FF_CHUNK, d).astype(BF16)

        xl = _outproj(ys_l, ws, xl, mod_l, g_mix_post[i], tm_l)
        xl = _ffn(xl, mod_l, g_ffn_pre[i], g_ffn_post[i], wu, cw, wd, tm_l)
        if need_ctx:
            xc = _outproj(ys_c, ws, xc, mod_c, g_mix_post[i], tm_c)
            xc = _ffn(xc, mod_c, g_ffn_pre[i], g_ffn_post[i], wu, cw, wd, tm_c)
    return xl
```

```python
import functools

import numpy as np
import jax
import jax.numpy as jnp
from jax import lax
from jax.experimental import pallas as pl
from jax.experimental.pallas import tpu as pltpu

F32 = jnp.float32
BF16 = jnp.bfloat16

LANES = 128
HEAD_DIM = 64
GRID_W = 64
D_MODEL = 1024
EPS = 1e-6
NEG_INF = -1e30
LOG2E = float(np.log2(np.e))
ATTN_SCALE = HEAD_DIM ** -0.5 * LOG2E

NA_HEADS = 8
NA_WIN_H = 8
NA_WIN_W = 16
NA_WIDTH = NA_HEADS * HEAD_DIM
NA_GROUP = 16
GLA_HEADS = 4
GLA_DK = 64
GLA_DV = 128
GLA_KW = GLA_HEADS * GLA_DK
GLA_VW = GLA_HEADS * GLA_DV
GLA_RANK = 16
GLA_TAU = 16.0
GLA_CHUNK = 64
GLA_BLOCK = 256
IN_AB = 3 * NA_WIDTH + 2 * GLA_KW + 2 * GLA_VW + 2 * GLA_RANK
IN_AB_PAD = 3200
SWA_HEADS = 16
SWA_KV = 4
SWA_GROUP = SWA_HEADS // SWA_KV
SWA_BLOCK = 128
SWA_QB = 2
IN_C = (SWA_HEADS + 2 * SWA_KV) * HEAD_DIM
D_FF = 2816
FF_CHUNK = 256
N_FF_CHUNKS = D_FF // FF_CHUNK
assert N_FF_CHUNKS * FF_CHUNK == D_FF
HALO = 16
ROPE_BASE = 10000.0
ROPE_TILE = 256
VMEM_LIMIT = 56 * 1024 * 1024

AB_QA, AB_KA, AB_VA = 0, 4, 8
AB_QB, AB_KB = 12, 14
AB_VB, AB_RB = 8, 10
AB_Z = 24
C_Q, C_K, C_V = 0, 8, 10


def _dot(a, b):
    return jnp.dot(a, b, preferred_element_type=F32)


def _dot_nt(a, b):
    return lax.dot_general(a, b, (((1,), (1,)), ((), ())), preferred_element_type=F32)


def _rms(x, g):
    return x * lax.rsqrt(jnp.mean(x * x, axis=-1, keepdims=True) + EPS) * g


def _silu(x):
    return x / (1.0 + jnp.exp(-x))


def _params(n_grid):
    return pltpu.CompilerParams(dimension_semantics=("arbitrary",) * n_grid,
                                vmem_limit_bytes=VMEM_LIMIT)


def _resident(shape):
    nd = len(shape)
    return pl.BlockSpec(shape, lambda *_: (0,) * nd, pipeline_mode=pl.Buffered(1))


def _head_mask(rows, h):
    lane = lax.broadcasted_iota(jnp.int32, (rows, LANES), 1)
    return (lane < HEAD_DIM) if h == 0 else (lane >= HEAD_DIM)


def _mod_kernel(c_ref, w_ref, b_ref, o_ref):
    s = _silu(c_ref[...]).astype(BF16)
    o_ref[0] = _dot(s, w_ref[0].astype(BF16)) + b_ref[0]


def _modulation(c_all, w_mod, b_mod):
    depth, d, n = w_mod.shape
    tn = 1536
    return pl.pallas_call(
        _mod_kernel,
        out_shape=jax.ShapeDtypeStruct((depth, 8, n), F32),
        grid=(depth, n // tn),
        in_specs=[pl.BlockSpec((8, d), lambda i, j: (0, 0)),
                  pl.BlockSpec((1, d, tn), lambda i, j: (i, 0, j)),
                  pl.BlockSpec((1, 1, tn), lambda i, j: (i, 0, j))],
        out_specs=pl.BlockSpec((1, 8, tn), lambda i, j: (i, 0, j)),
        compiler_params=_params(2),
        name="modulation",
    )(c_all, w_mod, b_mod.reshape(depth, 1, n))


def _rope(acc, cos, sin):
    n = acc.shape[1]
    lane = lax.broadcasted_iota(jnp.int32, acc.shape, 1)
    partner = jnp.where(lane % 2 == 0, pltpu.roll(acc, n - 1, 1), pltpu.roll(acc, 1, 1))
    return acc * cos + partner * sin


def _proj_kernel(*refs, chunk, rope_chunks, shift_idx):
    if rope_chunks:
        x_ref, mod_ref, g_ref, w_ref, cos_ref, sin_ref, o_ref = refs
    else:
        x_ref, mod_ref, g_ref, w_ref, o_ref = refs
    mod = mod_ref[0]
    h = _rms(x_ref[0], g_ref[...]) * (1.0 + mod[shift_idx + 1:shift_idx + 2]) + mod[shift_idx:shift_idx + 1]
    hb = h.astype(BF16)
    for c in range(w_ref.shape[1] // chunk):
        acc = _dot(hb, w_ref[:, c * chunk:(c + 1) * chunk])
        if c < rope_chunks:
            acc = _rope(acc, cos_ref[...], sin_ref[...])
        o_ref[0, :, c * chunk:(c + 1) * chunk] = acc.astype(BF16)


def _proj(x, mod, g, w, tm, chunk, rope=None, rope_chunks=0):
    b, s, d = x.shape
    n = w.shape[1]
    per_batch = mod.shape[0] > 1
    in_specs = [pl.BlockSpec((1, tm, d), lambda bi, i: (bi, i, 0)),
                pl.BlockSpec((1, 6, d), (lambda bi, i: (bi, 0, 0)) if per_batch else (lambda bi, i: (0, 0, 0))),
                _resident((1, d)),
                _resident((d, n))]
    args = [x, mod, g.reshape(1, d), w]
    if rope_chunks:
        in_specs += [pl.BlockSpec((tm, chunk), lambda bi, i: (i, 0))] * 2
        args += list(rope)
    return pl.pallas_call(
        functools.partial(_proj_kernel, chunk=chunk, rope_chunks=rope_chunks, shift_idx=0),
        out_shape=jax.ShapeDtypeStruct((b, s, n), BF16),
        grid=(b, s // tm),
        in_specs=in_specs,
        out_specs=pl.BlockSpec((1, tm, n), lambda bi, i: (bi, i, 0)),
        compiler_params=_params(2),
        name="proj",
    )(*args)


def _outproj_kernel(*refs, n_parts):
    y_refs, w_refs = refs[:n_parts], refs[n_parts:2 * n_parts]
    x_ref, mod_ref, g_ref, o_ref = refs[2 * n_parts:]
    acc = _dot(y_refs[0][0], w_refs[0][...])
    for yr, wr in zip(y_refs[1:], w_refs[1:]):
        acc = acc + _dot(yr[0], wr[...])
    o_ref[0] = x_ref[0] + mod_ref[0][2:3] * _rms(acc, g_ref[...])


def _outproj(ys, ws, x, mod, g, tm):
    b, s, d = x.shape
    per_batch = mod.shape[0] > 1
    in_specs = [pl.BlockSpec((1, tm, y.shape[2]), lambda bi, i: (bi, i, 0)) for y in ys]
    in_specs += [_resident(w.shape) for w in ws]
    in_specs += [pl.BlockSpec((1, tm, d), lambda bi, i: (bi, i, 0)),
                 pl.BlockSpec((1, 6, d), (lambda bi, i: (bi, 0, 0)) if per_batch else (lambda bi, i: (0, 0, 0))),
                 _resident((1, d))]
    return pl.pallas_call(
        functools.partial(_outproj_kernel, n_parts=len(ys)),
        out_shape=jax.ShapeDtypeStruct((b, s, d), F32),
        grid=(b, s // tm),
        in_specs=in_specs,
        out_specs=pl.BlockSpec((1, tm, d), lambda bi, i: (bi, i, 0)),
        compiler_params=_params(2),
        name="outproj",
    )(*ys, *ws, x, mod, g.reshape(1, d))


def _ffn_kernel(x_ref, xp_ref, xn_ref, mod_ref, gpre_ref, gpost_ref, wu_ref, cw_ref, wd_ref, o_ref,
                h_ref, u_ref, a_ref, *, tm):
    i = pl.program_id(1)
    mod = mod_ref[0]
    gpre = gpre_ref[...]

    def norm_mod(x):
        return _rms(x, gpre) * (1.0 + mod[4:5]) + mod[3:4]

    keep_prev = (i > 0).astype(F32)
    keep_next = (i < pl.num_programs(1) - 1).astype(F32)
    h_ref[0:HALO] = (norm_mod(xp_ref[0]) * keep_prev).astype(BF16)
    h_ref[HALO:HALO + tm] = norm_mod(x_ref[0]).astype(BF16)
    h_ref[HALO + tm:] = (norm_mod(xn_ref[0]) * keep_next).astype(BF16)
    rows = tm + 2 * HALO
    n_u = u_ref.shape[0]

    def up(c):
        h = h_ref[...]
        for part in range(2):
            col = part * D_FF + c * FF_CHUNK
            u_ref[c % n_u, :, part * FF_CHUNK:(part + 1) * FF_CHUNK] = _dot(h, wu_ref[:, col:col + FF_CHUNK])

    def act(c):
        u = u_ref[c % n_u]
        cw = cw_ref[c]
        y = (pltpu.roll(u, 1, 0)[HALO:HALO + tm] * cw[0:1]
             + u[HALO:HALO + tm] * cw[1:2]
             + pltpu.roll(u, rows - 1, 0)[HALO:HALO + tm] * cw[2:3]
             + cw[3:4])
        a_ref[:, c * FF_CHUNK:(c + 1) * FF_CHUNK] = (_silu(y[:, :FF_CHUNK]) * y[:, FF_CHUNK:]).astype(BF16)

    for c in range(min(n_u - 1, N_FF_CHUNKS)):
        up(c)
    for c in range(N_FF_CHUNKS):
        if c + n_u - 1 < N_FF_CHUNKS:
            up(c + n_u - 1)
        act(c)
    o_ref[0] = x_ref[0] + mod[5:6] * _rms(_dot(a_ref[...], wd_ref[...]), gpost_ref[...])


def _ffn(x, mod, g_pre, g_post, wu, cw, wd, tm):
    b, s, d = x.shape
    per_batch = mod.shape[0] > 1
    hb = tm // HALO
    n_halo = s // HALO
    return pl.pallas_call(
        functools.partial(_ffn_kernel, tm=tm),
        out_shape=jax.ShapeDtypeStruct((b, s, d), F32),
        grid=(b, s // tm),
        in_specs=[pl.BlockSpec((1, tm, d), lambda bi, i: (bi, i, 0)),
                  pl.BlockSpec((1, HALO, d), lambda bi, i: (bi, jnp.maximum(i * hb - 1, 0), 0)),
                  pl.BlockSpec((1, HALO, d), lambda bi, i: (bi, jnp.minimum((i + 1) * hb, n_halo - 1), 0)),
                  pl.BlockSpec((1, 6, d), (lambda bi, i: (bi, 0, 0)) if per_batch else (lambda bi, i: (0, 0, 0))),
                  _resident((1, d)), _resident((1, d)),
                  _resident(wu.shape), _resident(cw.shape), _resident(wd.shape)],
        out_specs=pl.BlockSpec((1, tm, d), lambda bi, i: (bi, i, 0)),
        scratch_shapes=[pltpu.VMEM((tm + 2 * HALO, d), BF16),
                        pltpu.VMEM((3, tm + 2 * HALO, 2 * FF_CHUNK), F32),
                        pltpu.VMEM((tm, D_FF), BF16)],
        compiler_params=_params(2),
        name="ffn",
    )(x, x, x, mod, g_pre.reshape(1, d), g_post.reshape(1, d), wu, cw, wd)


def _na_kernel(q_ref, k_ref, v_ref, kc_ref, vc_ref, bias_ref, o_ref, *, rows):
    kc = kc_ref[0]
    vc = vc_ref[0]
    win = NA_WIN_H * GRID_W
    mask_a = _head_mask(GRID_W, 0)
    vc_ext = jnp.concatenate([vc, jnp.ones_like(vc)], axis=1)
    ones_w = jnp.ones((win, LANES), BF16)

    def group(gi, carry):
        starts, scores, probs = [], [], []
        for i in range(NA_GROUP):
            r = gi * NA_GROUP + i
            r0 = jnp.clip(r - NA_WIN_H // 2, 0, rows - NA_WIN_H)
            q0 = pl.multiple_of(r * GRID_W, GRID_W)
            k0 = pl.multiple_of(r0 * GRID_W, GRID_W)
            q = q_ref[0, pl.ds(q0, GRID_W), :]
            zero = jnp.zeros_like(q)
            q2 = jnp.concatenate([jnp.where(mask_a, q, zero), jnp.where(mask_a, zero, q)], axis=0)
            s = _dot_nt(q2, k_ref[0, pl.ds(k0, win), :]) + bias_ref[0, r - r0]
            sc = _dot_nt(q2, kc)
            starts.append((q0, k0))
            scores.append((s, sc))
        for s, sc in scores:
            m = jnp.maximum(jnp.max(s, axis=-1, keepdims=True), jnp.max(sc, axis=-1, keepdims=True))
            probs.append((jnp.exp2(s - m).astype(BF16), jnp.exp2(sc - m).astype(BF16)))
        for (q0, k0), (p, pc) in zip(starts, probs):
            v_ext = jnp.concatenate([v_ref[0, pl.ds(k0, win), :], ones_w], axis=1)
            o_ext = _dot(p, v_ext) + _dot(pc, vc_ext)
            o2 = o_ext[:, :LANES] / o_ext[:, LANES:]
            o_ref[0, pl.ds(q0, GRID_W), :] = jnp.where(mask_a, o2[:GRID_W], o2[GRID_W:]).astype(BF16)
        return carry

    lax.fori_loop(0, rows // NA_GROUP, group, 0)


def _na(pl_, pc_, bias):
    b, s, _ = pl_.shape
    n_ctx = pc_.shape[1]
    n_pairs = NA_HEADS // 2
    return pl.pallas_call(
        functools.partial(_na_kernel, rows=s // GRID_W),
        out_shape=jax.ShapeDtypeStruct((b, s, NA_WIDTH), BF16),
        grid=(b, n_pairs),
        in_specs=[pl.BlockSpec((1, s, LANES), lambda bi, p: (bi, 0, AB_QA + p)),
                  pl.BlockSpec((1, s, LANES), lambda bi, p: (bi, 0, AB_KA + p)),
                  pl.BlockSpec((1, s, LANES), lambda bi, p: (bi, 0, AB_VA + p)),
                  pl.BlockSpec((1, n_ctx, LANES), lambda bi, p: (bi, 0, AB_KA + p)),
                  pl.BlockSpec((1, n_ctx, LANES), lambda bi, p: (bi, 0, AB_VA + p)),
                  pl.BlockSpec((1, NA_WIN_H, 2 * GRID_W, NA_WIN_H * GRID_W), lambda bi, p: (p, 0, 0, 0))],
        out_specs=pl.BlockSpec((1, s, LANES), lambda bi, p: (bi, 0, p)),
        compiler_params=_params(2),
        name="na",
    )(pl_, pl_, pl_, pc_, pc_, bias)


def _na_bias(rpb):
    h = rpb.shape[0]
    cidx = np.arange(GRID_W)
    c_start = np.clip(cidx - NA_WIN_W // 2, 0, GRID_W - NA_WIN_W)
    col_ok = (cidx[None, :] >= c_start[:, None]) & (cidx[None, :] < c_start[:, None] + NA_WIN_W)
    dc = np.clip(cidx[None, :] - cidx[:, None] + NA_WIN_W - 1, 0, 2 * NA_WIN_W - 2)
    bm = jnp.where(col_ok[None, None], rpb.astype(F32)[:, :, dc] * LOG2E, NEG_INF)
    bm = bm.transpose(0, 2, 1, 3)
    t = jnp.stack([bm[:, :, NA_WIN_H - 1 - e:2 * NA_WIN_H - 1 - e, :].reshape(h, GRID_W, NA_WIN_H * GRID_W)
                   for e in range(NA_WIN_H)], axis=1)
    t = t.reshape(h // 2, 2, NA_WIN_H, GRID_W, NA_WIN_H * GRID_W)
    return t.transpose(0, 2, 1, 3, 4).reshape(h // 2, NA_WIN_H, 2 * GRID_W, NA_WIN_H * GRID_W)


def _ctx_attn_kernel(*refs, has_sink):
    if has_sink:
        q_ref, k_ref, v_ref, sink_ref, o_ref = refs
    else:
        q_ref, k_ref, v_ref, o_ref = refs
    q, k, v = q_ref[0], k_ref[0], v_ref[0]
    n = q.shape[0]
    masks = [_head_mask(n, 0), _head_mask(n, 1)]
    outs = []
    for h in range(2):
        qm = jnp.where(masks[h], q, jnp.zeros_like(q))
        s = _dot_nt(qm, k)
        m = jnp.max(s, axis=-1, keepdims=True)
        if has_sink:
            sk = sink_ref[0, h:h + 1, 0:1]
            m = jnp.maximum(m, sk)
        p = jnp.exp2(s - m)
        den = jnp.sum(p, axis=-1, keepdims=True)
        if has_sink:
            den = den + jnp.exp2(sk - m)
        outs.append(_dot(p.astype(BF16), v) / den)
    o_ref[0] = jnp.where(masks[0], outs[0], outs[1]).astype(BF16)


def _ctx_attn(pc_, q_off, k_off, v_off, n_q_tiles, group, sink=None):
    b, n, _ = pc_.shape
    in_specs = [pl.BlockSpec((1, n, LANES), lambda bi, t: (bi, 0, q_off + t)),
                pl.BlockSpec((1, n, LANES), lambda bi, t: (bi, 0, k_off + t // group)),
                pl.BlockSpec((1, n, LANES), lambda bi, t: (bi, 0, v_off + t // group))]
    args = [pc_, pc_, pc_]
    if sink is not None:
        in_specs.append(pl.BlockSpec((1, 2, LANES), lambda bi, t: (t, 0, 0)))
        args.append(sink)
    return pl.pallas_call(
        functools.partial(_ctx_attn_kernel, has_sink=sink is not None),
        out_shape=jax.ShapeDtypeStruct((b, n, n_q_tiles * LANES), BF16),
        grid=(b, n_q_tiles),
        in_specs=in_specs,
        out_specs=pl.BlockSpec((1, n, LANES), lambda bi, t: (bi, 0, t)),
        compiler_params=_params(2),
        name="ctx_attn",
    )(*args)


def _gla_kernel(*refs, n_lat_blocks, need_ctx):
    (ql, kl, vl, zl, rl, qc, kc, vc, zc, rc, w2f, w2b, baf, bab, g_ref) = refs[:15]
    n_out = 2 if need_ctx else 1
    yl_ref = refs[15]
    yc_ref = refs[16] if need_ctx else None
    qt_s, kd_s, dec_s, vt_s, o_s = refs[15 + n_out:]
    blk, ck = GLA_BLOCK, GLA_CHUNK
    cpb = blk // ck
    shift = ck.bit_length() - 1
    ri = lax.broadcasted_iota(jnp.int32, (blk, blk), 0)
    ci = lax.broadcasted_iota(jnp.int32, (blk, blk), 1)
    same = lax.shift_right_logical(ri, shift) == lax.shift_right_logical(ci, shift)
    att_mask = [same & (ci <= ri), same & (ci >= ri)]
    as_bf16 = lambda m: jnp.where(m, 1.0, 0.0).astype(BF16)
    cum = [jnp.concatenate([as_bf16(att_mask[d]), as_bf16(same)], axis=0) for d in range(2)]
    head = [_head_mask(blk, 0), _head_mask(blk, 1)]
    head_ck = [_head_mask(ck, 0), _head_mask(ck, 1)]
    tok_lo = lax.broadcasted_iota(jnp.int32, (2 * GLA_DV, LANES), 1) < ck
    w2 = jnp.concatenate([w2f[...], w2b[...]], axis=1)
    ba = jnp.concatenate([baf[...], bab[...]], axis=1)
    g = g_ref[...]
    lat4, ctx4 = (ql, kl, vl, zl), (qc, kc, vc, zc)

    def phase1(blocks):
        gates = []
        for (_, _, _, z_ref), t_in, _ in blocks:
            x = _dot(z_ref[0, pl.ds(t_in, blk), :], w2) + ba
            la = (jnp.minimum(x, 0.0) - jnp.log(1.0 + jnp.exp(-jnp.abs(x)))) * (1.0 / GLA_TAU)
            hi = la.astype(BF16)
            gates.append((hi, (la - hi.astype(F32)).astype(BF16)))
        cums = []
        for hi, lo in gates:
            per_dir = []
            for d in range(2):
                c2 = _dot(cum[d], jnp.concatenate([hi[:, d * LANES:(d + 1) * LANES],
                                                   lo[:, d * LANES:(d + 1) * LANES]], axis=1))
                per_dir.append((c2[:blk, :LANES] + c2[:blk, LANES:], c2[blk:, :LANES] + c2[blk:, LANES:]))
            cums.append(per_dir)
        scores = []
        for ((q_ref, k_ref, _, _), t_in, bo), per_dir in zip(blocks, cums):
            q = q_ref[0, pl.ds(t_in, blk), :].astype(F32)
            k = k_ref[0, pl.ds(t_in, blk), :].astype(F32)
            t_out = pl.multiple_of(bo * blk, blk)
            items = []
            for d, (b, tot) in enumerate(per_dir):
                qt = q * jnp.exp(b)
                kt = (k * jnp.exp(-b)).astype(BF16)
                qt_s[d, pl.ds(t_out, blk), :] = qt.astype(BF16)
                kd_s[d, pl.ds(t_out, blk), :] = (k * jnp.exp(tot - b)).astype(BF16)
                for j in range(cpb):
                    dec_s[d, pl.ds(bo * cpb + j, 1), :] = jnp.exp(tot[j * ck:j * ck + 1])
                for h in range(2):
                    items.append((d, h, _dot_nt(jnp.where(head[h], qt, 0.0).astype(BF16), kt)))
            scores.append(items)
        for ((_, _, v_ref, _), t_in, bo), items in zip(blocks, scores):
            v = v_ref[0, pl.ds(t_in, blk), :]
            oi = [None, None]
            for d, h, s in items:
                c = _dot(jnp.where(att_mask[d], s, 0.0).astype(BF16), v[:, h * GLA_DV:(h + 1) * GLA_DV])
                oi[h] = c if oi[h] is None else oi[h] + c
            o_s[pl.ds(pl.multiple_of(bo * blk, blk), blk), :] = jnp.concatenate(oi, axis=1)
            vt_s[bo] = v.astype(F32).T.astype(BF16)

    def scan_block(bo, st, d, r_ref=None, t_in=None, y_ref=None):
        t0 = pl.multiple_of(bo * blk, blk)
        vt = vt_s[bo]
        kd = kd_s[d, pl.ds(t0, blk), :]
        order = list(range(cpb)) if d == 0 else list(range(cpb - 1, -1, -1))
        upd = {}
        for c in order:
            half = c // 2
            vt_c = vt[:, half * LANES:(half + 1) * LANES]
            vt_c = jnp.where(tok_lo if c % 2 == 0 else ~tok_lo, vt_c, jnp.zeros_like(vt_c))
            upd[c] = _dot(vt_c, kd[half * LANES:(half + 1) * LANES])
        for c in order:
            tc = pl.multiple_of(t0 + c * ck, ck)
            qt = qt_s[d, pl.ds(tc, ck), :]
            stb = st.astype(BF16)
            outs = [_dot_nt(jnp.where(head_ck[h], qt, jnp.zeros_like(qt)), stb[h * GLA_DV:(h + 1) * GLA_DV])
                    for h in range(2)]
            o_s[pl.ds(tc, ck), :] += jnp.concatenate(outs, axis=1)
            st = dec_s[d, pl.ds(bo * cpb + c, 1), :] * st + upd[c]
        if y_ref is not None:
            o = o_s[pl.ds(t0, blk), :]
            r = r_ref[0, pl.ds(t_in, blk), :].astype(F32)
            on = jnp.concatenate(
                [o[:, h * GLA_DV:(h + 1) * GLA_DV]
                 * lax.rsqrt(jnp.mean(o[:, h * GLA_DV:(h + 1) * GLA_DV] ** 2, axis=-1, keepdims=True) + EPS)
                 for h in range(2)], axis=1)
            y_ref[0, pl.ds(t_in, blk), :] = (on * g * _silu(r)).astype(BF16)
        return st

    phase1([(ctx4, 0, 0)])

    def phase1_lat(i, carry):
        phase1([(lat4, pl.multiple_of((2 * i + j) * blk, blk), 1 + 2 * i + j) for j in range(2)])
        return carry

    lax.fori_loop(0, n_lat_blocks // 2, phase1_lat, 0)

    zero = jnp.zeros((2 * GLA_DV, 2 * GLA_DK), F32)
    st = scan_block(0, zero, 0)
    lax.fori_loop(0, n_lat_blocks, lambda i, s: scan_block(1 + i, s, 0), st)
    st = scan_block(0, zero, 1, rc, 0, yc_ref)
    lax.fori_loop(
        0, n_lat_blocks,
        lambda i, s: scan_block(n_lat_blocks - i, s, 1, rl,
                                pl.multiple_of((n_lat_blocks - 1 - i) * blk, blk), yl_ref), st)


def _gla(pl_, pc_, w2, ba, g, need_ctx):
    b, s, _ = pl_.shape
    n_c = pc_.shape[1]
    n_pairs = GLA_HEADS // 2
    w = 2 * GLA_DV

    def stream(n):
        return [pl.BlockSpec((1, n, LANES), lambda bi, p: (bi, 0, AB_QB + p)),
                pl.BlockSpec((1, n, LANES), lambda bi, p: (bi, 0, AB_KB + p)),
                pl.BlockSpec((1, n, w), lambda bi, p: (bi, 0, AB_VB + p)),
                pl.BlockSpec((1, n, LANES), lambda bi, p: (bi, 0, AB_Z)),
                pl.BlockSpec((1, n, w), lambda bi, p: (bi, 0, AB_RB + p))]

    in_specs = stream(s) + stream(n_c) + [
        pl.BlockSpec((LANES, LANES), lambda bi, p: (0, p)),
        pl.BlockSpec((LANES, LANES), lambda bi, p: (0, n_pairs + p)),
        pl.BlockSpec((1, LANES), lambda bi, p: (0, p)),
        pl.BlockSpec((1, LANES), lambda bi, p: (0, n_pairs + p)),
        pl.BlockSpec((1, w), lambda bi, p: (0, p))]
    out_shape = [jax.ShapeDtypeStruct((b, s, GLA_VW), BF16)]
    out_specs = [pl.BlockSpec((1, s, w), lambda bi, p: (bi, 0, p))]
    if need_ctx:
        out_shape.append(jax.ShapeDtypeStruct((b, n_c, GLA_VW), BF16))
        out_specs.append(pl.BlockSpec((1, n_c, w), lambda bi, p: (bi, 0, p)))
    assert n_c == GLA_BLOCK and s % (2 * GLA_BLOCK) == 0
    n_lat_blocks = s // GLA_BLOCK
    t_all = n_c + s
    n_chunks = -(-(t_all // GLA_CHUNK) // 8) * 8
    scratch = [pltpu.VMEM((2, t_all, LANES), BF16),
               pltpu.VMEM((2, t_all, LANES), BF16),
               pltpu.VMEM((2, n_chunks, LANES), F32),
               pltpu.VMEM((1 + n_lat_blocks, w, GLA_BLOCK), BF16),
               pltpu.VMEM((t_all, w), F32)]
    res = pl.pallas_call(
        functools.partial(_gla_kernel, n_lat_blocks=n_lat_blocks, need_ctx=need_ctx),
        out_shape=out_shape,
        grid=(b, n_pairs),
        in_specs=in_specs,
        out_specs=out_specs,
        scratch_shapes=scratch,
        compiler_params=_params(2),
        name="gla",
    )(*([pl_] * 5), *([pc_] * 5), w2, w2, ba, ba, g.reshape(1, GLA_VW))
    return (res[0], res[1]) if need_ctx else (res[0], None)


def _gla_gate_weights(wa2, ba):
    w = jnp.zeros((LANES, 2 * GLA_KW), F32)
    w = w.at[0:GLA_RANK, 0:GLA_KW].set(wa2[0])
    w = w.at[GLA_RANK:2 * GLA_RANK, GLA_KW:].set(wa2[1])
    return w.astype(BF16), ba.reshape(1, 2 * GLA_KW).astype(F32)


def _swa_kernel(q_ref, kp_ref, kn_ref, kx_ref, vp_ref, vn_ref, vx_ref, kc_ref, vc_ref, sink_ref, o_ref):
    n = pl.program_id(1)
    blk = SWA_BLOCK
    n_c = kc_ref.shape[1]
    n_keys = 3 * blk + n_c
    k_lat = jnp.concatenate([kp_ref[0], kn_ref[0], kx_ref[0]], axis=0)
    v_lat = jnp.concatenate([vp_ref[0], vn_ref[0], vx_ref[0]], axis=0)
    ones = jnp.ones((n_keys, LANES), BF16)
    qi = lax.broadcasted_iota(jnp.int32, (blk, n_keys), 0)
    kj = lax.broadcasted_iota(jnp.int32, (blk, n_keys), 1)
    in_prev = (kj < blk) & (kj >= qi)
    in_next = (kj >= 2 * blk) & (kj < 3 * blk) & (kj - 2 * blk <= qi)
    always = (kj >= 3 * blk) | ((kj >= blk) & (kj < 2 * blk))
    first, last = n == 0, n == pl.num_programs(1) - 1
    mask_a = _head_mask(blk, 0)
    half = SWA_GROUP // 2
    n_pairs = SWA_KV // 2
    n_chunks = SWA_QB * n_pairs * 2

    def block_inputs(j):
        k = jnp.concatenate([k_lat[j * blk:(j + 3) * blk], kc_ref[0]], axis=0)
        v = jnp.concatenate([v_lat[j * blk:(j + 3) * blk], vc_ref[0]], axis=0)
        has_prev = jnp.logical_not(first) if j == 0 else True
        has_next = jnp.logical_not(last) if j == SWA_QB - 1 else True
        return k, v, always | (in_prev & has_prev) | (in_next & has_next)

    blocks = [block_inputs(j) for j in range(SWA_QB)]

    def scores(c):
        j, pair = c // (2 * n_pairs), (c // 2) % n_pairs
        k, _, ok = blocks[j]
        tiles = [pair * SWA_GROUP + (c % 2) * half + g for g in range(half)]
        q_parts, sink_parts = [], []
        for t in tiles:
            q = q_ref[0, j * blk:(j + 1) * blk, t * LANES:(t + 1) * LANES]
            zero = jnp.zeros_like(q)
            q_parts += [jnp.where(mask_a, q, zero), jnp.where(mask_a, zero, q)]
            sink_parts += [jnp.broadcast_to(sink_ref[t, h:h + 1, 0:1], (blk, 1)) for h in range(2)]
        s = _dot_nt(jnp.concatenate(q_parts, axis=0), k[:, pair * LANES:(pair + 1) * LANES])
        s = jnp.concatenate([jnp.where(ok, s[i * blk:(i + 1) * blk], NEG_INF) for i in range(2 * half)], axis=0)
        return j, tiles, pair, s, jnp.concatenate(sink_parts, axis=0)

    def finish(j, tiles, pair, s, sk):
        m = jnp.maximum(jnp.max(s, axis=-1, keepdims=True), sk)
        p = jnp.exp2(s - m).astype(BF16)
        o_ext = _dot(p, jnp.concatenate([blocks[j][1][:, pair * LANES:(pair + 1) * LANES], ones], axis=1))
        o = o_ext[:, :LANES] / (o_ext[:, LANES:] + jnp.exp2(sk - m))
        for i, t in enumerate(tiles):
            o_ref[0, j * blk:(j + 1) * blk, t * LANES:(t + 1) * LANES] = jnp.where(
                mask_a, o[2 * i * blk:(2 * i + 1) * blk], o[(2 * i + 1) * blk:(2 * i + 2) * blk]).astype(BF16)

    pending = scores(0)
    for c in range(n_chunks):
        nxt = scores(c + 1) if c + 1 < n_chunks else None
        finish(*pending)
        pending = nxt


def _swa(pl_, pc_, sink):
    b, s, _ = pl_.shape
    n_c = pc_.shape[1]
    nb = s // SWA_BLOCK
    step = SWA_QB * SWA_BLOCK
    assert s % step == 0
    qw = SWA_HEADS * HEAD_DIM
    kw = SWA_KV * HEAD_DIM
    k_blk, v_blk = (C_K * LANES) // kw, (C_V * LANES) // kw

    def kv(off):
        return [pl.BlockSpec((1, SWA_BLOCK, kw), lambda bi, n: (bi, jnp.maximum(n * SWA_QB - 1, 0), off)),
                pl.BlockSpec((1, step, kw), lambda bi, n: (bi, n, off)),
                pl.BlockSpec((1, SWA_BLOCK, kw), lambda bi, n: (bi, jnp.minimum((n + 1) * SWA_QB, nb - 1), off))]

    in_specs = ([pl.BlockSpec((1, step, qw), lambda bi, n: (bi, n, 0))] + kv(k_blk) + kv(v_blk)
                + [pl.BlockSpec((1, n_c, kw), lambda bi, n: (bi, 0, k_blk)),
                   pl.BlockSpec((1, n_c, kw), lambda bi, n: (bi, 0, v_blk)),
                   _resident(sink.shape)])
    return pl.pallas_call(
        _swa_kernel,
        out_shape=jax.ShapeDtypeStruct((b, s, qw), BF16),
        grid=(b, s // step),
        in_specs=in_specs,
        out_specs=pl.BlockSpec((1, step, qw), lambda bi, n: (bi, n, 0)),
        compiler_params=_params(2),
        name="swa",
    )(pl_, pl_, pl_, pl_, pl_, pl_, pl_, pc_, pc_, sink)


def _swa_head_perm():
    perm = np.zeros(SWA_HEADS * HEAD_DIM, np.int32)
    d = np.arange(HEAD_DIM)
    for p in range(SWA_KV // 2):
        for gi in range(SWA_GROUP):
            for hh in range(2):
                perm[(p * SWA_GROUP + gi) * LANES + hh * HEAD_DIM + d] = ((2 * p + hh) * SWA_GROUP + gi) * HEAD_DIM + d
    return perm


def _rope_tables(n):
    t = jnp.arange(n)
    row = (t // GRID_W).astype(F32)
    col = (t % GRID_W).astype(F32)
    half = HEAD_DIM // 2
    inv = 1.0 / (ROPE_BASE ** (jnp.arange(0, half, 2, dtype=F32) / half))
    ang = jnp.concatenate([row[:, None] * inv, col[:, None] * inv], axis=-1)
    cos = jnp.repeat(jnp.cos(ang), 2, axis=1)
    sin = jnp.repeat(jnp.sin(ang), 2, axis=1) * jnp.tile(jnp.array([-1.0, 1.0], F32), half)
    reps = ROPE_TILE // HEAD_DIM
    return jnp.tile(cos, (1, reps)), jnp.tile(sin, (1, reps))


def kernel(x, c, ctx, c_ctx, w_mod, b_mod, g_mix_pre, g_mix_post, g_ffn_pre, g_ffn_post, w_out, w_up, conv_w,
           conv_b, w_down, w_in_ab, na_rpb, gla_wa2, gla_ba, gla_g, w_in_c, swa_sink):
    bsz, seq, d = x.shape
    n_ctx = ctx.shape[1]
    depth = w_mod.shape[0]
    tm_l = 512 if seq % 512 == 0 else 256
    tm_c = n_ctx

    c_all = jnp.zeros((8, d), F32).at[:bsz].set(c).at[bsz].set(c_ctx)
    mods = _modulation(c_all, w_mod, b_mod).reshape(depth, 8, 6, d)
    rope = _rope_tables(seq)
    perm = _swa_head_perm()

    col_scale_ab = np.ones((IN_AB,), np.float32)
    col_scale_ab[0:NA_WIDTH] = ATTN_SCALE
    col_scale_ab[3 * NA_WIDTH:3 * NA_WIDTH + GLA_KW] = GLA_DK ** -0.5

    xl, xc = x, ctx
    for i in range(depth):
        need_ctx = i < depth - 1
        j = i // 2
        mod_l = mods[i, :bsz]
        mod_c = mods[i, bsz:bsz + 1]
        if i % 2 == 0:
            w_in = jnp.pad(w_in_ab[j] * col_scale_ab, ((0, 0), (0, IN_AB_PAD - IN_AB))).astype(BF16)
            p_l = _proj(xl, mod_l, g_mix_pre[i], w_in, tm_l, 640)
            p_c = _proj(xc, mod_c, g_mix_pre[i], w_in, tm_c, 640)
            w2, ba = _gla_gate_weights(gla_wa2[j], gla_ba[j])
            y_na = _na(p_l, p_c, _na_bias(na_rpb[j]))
            y_gla, yc_gla = _gla(p_l, p_c, w2, ba, gla_g[j], need_ctx)
            ys_l = [y_na, y_gla]
            ws = [w_out[i][:NA_WIDTH].astype(BF16), w_out[i][NA_WIDTH:].astype(BF16)]
            if need_ctx:
                ys_c = [_ctx_attn(p_c, AB_QA, AB_KA, AB_VA, NA_HEADS // 2, 1), yc_gla]
        else:
            wq = w_in_c[j][:, :SWA_HEADS * HEAD_DIM][:, perm] * ATTN_SCALE
            w_in = jnp.concatenate([wq, w_in_c[j][:, SWA_HEADS * HEAD_DIM:]], axis=1).astype(BF16)
            p_l = _proj(xl, mod_l, g_mix_pre[i], w_in, tm_l, ROPE_TILE, rope, (C_V * LANES) // ROPE_TILE)
            p_c = _proj(xc, mod_c, g_mix_pre[i], w_in, tm_c, ROPE_TILE)
            sink = jnp.broadcast_to((swa_sink[j] * LOG2E)[perm[::HEAD_DIM] // HEAD_DIM].reshape(-1, 2, 1),
                                    (SWA_HEADS // 2, 2, LANES)).astype(F32)
            ys_l = [_swa(p_l, p_c, sink)]
            ws = [w_out[i][perm].astype(BF16)]
            if need_ctx:
                ys_c = [_ctx_attn(p_c, C_Q, C_K, C_V, SWA_HEADS // 2, SWA_GROUP, sink)]

        wu = w_up[i].astype(BF16)
        cwb = jnp.concatenate([conv_w[i], conv_b[i][None]], axis=0)
        cw = jnp.concatenate([cwb[:, :D_FF].reshape(4, N_FF_CHUNKS, FF_CHUNK),
                              cwb[:, D_FF:].reshape(4, N_FF_CHUNKS, FF_CHUNK)], axis=2).transpose(1, 0, 2)
        wd = w_down[i].astype(BF16)

        xl = _outproj(ys_l, ws, xl, mod_l, g_mix_post[i], tm_l)
        xl = _ffn(xl, mod_l, g_ffn_pre[i], g_ffn_post[i], wu, cw, wd, tm_l)
        if need_ctx:
            xc = _outproj(ys_c, ws, xc, mod_c, g_mix_post[i], tm_c)
            xc = _ffn(xc, mod_c, g_ffn_pre[i], g_ffn_post[i], wu, cw, wd, tm_c)
    return xl
```

```python
import functools

import numpy as np
import jax
import jax.numpy as jnp
from jax import lax
from jax.experimental import pallas as pl
from jax.experimental.pallas import tpu as pltpu

F32 = jnp.float32
BF16 = jnp.bfloat16

LANES = 128
HEAD_DIM = 64
GRID_W = 64
D_MODEL = 1024
EPS = 1e-6
NEG_INF = -1e30
LOG2E = float(np.log2(np.e))
ATTN_SCALE = HEAD_DIM ** -0.5 * LOG2E

NA_HEADS = 8
NA_WIN_H = 8
NA_WIN_W = 16
NA_WIDTH = NA_HEADS * HEAD_DIM
NA_GROUP = 16
GLA_HEADS = 4
GLA_DK = 64
GLA_DV = 128
GLA_KW = GLA_HEADS * GLA_DK
GLA_VW = GLA_HEADS * GLA_DV
GLA_RANK = 16
GLA_TAU = 16.0
GLA_CHUNK = 64
GLA_BLOCK = 256
IN_AB = 3 * NA_WIDTH + 2 * GLA_KW + 2 * GLA_VW + 2 * GLA_RANK
IN_AB_PAD = 3200
SWA_HEADS = 16
SWA_KV = 4
SWA_GROUP = SWA_HEADS // SWA_KV
SWA_BLOCK = 128
SWA_QB = 2
IN_C = (SWA_HEADS + 2 * SWA_KV) * HEAD_DIM
D_FF = 2816
FF_CHUNK = 256
N_FF_CHUNKS = D_FF // FF_CHUNK
assert N_FF_CHUNKS * FF_CHUNK == D_FF
HALO = 16
ROPE_BASE = 10000.0
ROPE_TILE = 256
VMEM_LIMIT = 56 * 1024 * 1024

AB_QA, AB_KA, AB_VA = 0, 4, 8
AB_QB, AB_KB = 12, 14
AB_VB, AB_RB = 8, 10
AB_Z = 24
C_Q, C_K, C_V = 0, 8, 10


def _dot(a, b):
    return jnp.dot(a, b, preferred_element_type=F32)


def _dot_nt(a, b):
    return lax.dot_general(a, b, (((1,), (1,)), ((), ())), preferred_element_type=F32)


def _rms(x, g):
    return x * lax.rsqrt(jnp.mean(x * x, axis=-1, keepdims=True) + EPS) * g


def _silu(x):
    return x / (1.0 + jnp.exp(-x))


def _params(n_grid):
    return pltpu.CompilerParams(dimension_semantics=("arbitrary",) * n_grid,
                                vmem_limit_bytes=VMEM_LIMIT)


def _resident(shape):
    nd = len(shape)
    return pl.BlockSpec(shape, lambda *_: (0,) * nd, pipeline_mode=pl.Buffered(1))


def _layer_block(arr, layer, rows=None, row_block=0):
    shape = list(arr.shape[1:])
    if rows is not None:
        shape[0] = rows
    tail = (0,) * (len(shape) - 1)
    return pl.BlockSpec((None, *shape), lambda *_: (layer, row_block) + tail, pipeline_mode=pl.Buffered(1))


def _head_mask(rows, h):
    lane = lax.broadcasted_iota(jnp.int32, (rows, LANES), 1)
    return (lane < HEAD_DIM) if h == 0 else (lane >= HEAD_DIM)


def _mod_kernel(c_ref, w_ref, b_ref, o_ref):
    s = _silu(c_ref[...]).astype(BF16)
    o_ref[0] = _dot(s, w_ref[0].astype(BF16)) + b_ref[0]


def _modulation(c_all, w_mod, b_mod):
    depth, d, n = w_mod.shape
    tn = 1536
    return pl.pallas_call(
        _mod_kernel,
        out_shape=jax.ShapeDtypeStruct((depth, 8, n), F32),
        grid=(depth, n // tn),
        in_specs=[pl.BlockSpec((8, d), lambda i, j: (0, 0)),
                  pl.BlockSpec((1, d, tn), lambda i, j: (i, 0, j)),
                  pl.BlockSpec((1, 1, tn), lambda i, j: (i, 0, j))],
        out_specs=pl.BlockSpec((1, 8, tn), lambda i, j: (i, 0, j)),
        compiler_params=_params(2),
        name="modulation",
    )(c_all, w_mod, b_mod.reshape(depth, 1, n))


def _rope(acc, cos, sin):
    n = acc.shape[1]
    lane = lax.broadcasted_iota(jnp.int32, acc.shape, 1)
    partner = jnp.where(lane % 2 == 0, pltpu.roll(acc, n - 1, 1), pltpu.roll(acc, 1, 1))
    return acc * cos + partner * sin


def _proj_kernel(*refs, chunk, rope_chunks, shift_idx):
    if rope_chunks:
        x_ref, mod_ref, g_ref, w_ref, cos_ref, sin_ref, o_ref = refs
    else:
        x_ref, mod_ref, g_ref, w_ref, o_ref = refs
    mod = mod_ref[0]
    h = _rms(x_ref[0], g_ref[...]) * (1.0 + mod[shift_idx + 1:shift_idx + 2]) + mod[shift_idx:shift_idx + 1]
    hb = h.astype(BF16)
    for c in range(w_ref.shape[1] // chunk):
        acc = _dot(hb, w_ref[:, c * chunk:(c + 1) * chunk])
        if c < rope_chunks:
            acc = _rope(acc, cos_ref[...], sin_ref[...])
        o_ref[0, :, c * chunk:(c + 1) * chunk] = acc.astype(BF16)


def _proj(x, mod, g, w, layer, tm, chunk, rope=None, rope_chunks=0):
    b, s, d = x.shape
    n = w.shape[2]
    per_batch = mod.shape[0] > 1
    in_specs = [pl.BlockSpec((1, tm, d), lambda bi, i: (bi, i, 0)),
                pl.BlockSpec((1, 6, d), (lambda bi, i: (bi, 0, 0)) if per_batch else (lambda bi, i: (0, 0, 0))),
                _resident((1, d)),
                _layer_block(w, layer)]
    args = [x, mod, g.reshape(1, d), w]
    if rope_chunks:
        in_specs += [pl.BlockSpec((tm, chunk), lambda bi, i: (i, 0))] * 2
        args += list(rope)
    return pl.pallas_call(
        functools.partial(_proj_kernel, chunk=chunk, rope_chunks=rope_chunks, shift_idx=0),
        out_shape=jax.ShapeDtypeStruct((b, s, n), BF16),
        grid=(b, s // tm),
        in_specs=in_specs,
        out_specs=pl.BlockSpec((1, tm, n), lambda bi, i: (bi, i, 0)),
        compiler_params=_params(2),
        name="proj",
    )(*args)


def _outproj_kernel(*refs, n_parts):
    y_refs, w_refs = refs[:n_parts], refs[n_parts:2 * n_parts]
    x_ref, mod_ref, g_ref, o_ref = refs[2 * n_parts:]
    acc = _dot(y_refs[0][0], w_refs[0][...])
    for yr, wr in zip(y_refs[1:], w_refs[1:]):
        acc = acc + _dot(yr[0], wr[...])
    o_ref[0] = x_ref[0] + mod_ref[0][2:3] * _rms(acc, g_ref[...])


def _outproj(ys, w, layer, x, mod, g, tm):
    b, s, d = x.shape
    per_batch = mod.shape[0] > 1
    in_specs = [pl.BlockSpec((1, tm, y.shape[2]), lambda bi, i: (bi, i, 0)) for y in ys]
    in_specs += [_layer_block(w, layer, y.shape[2], k) for k, y in enumerate(ys)]
    in_specs += [pl.BlockSpec((1, tm, d), lambda bi, i: (bi, i, 0)),
                 pl.BlockSpec((1, 6, d), (lambda bi, i: (bi, 0, 0)) if per_batch else (lambda bi, i: (0, 0, 0))),
                 _resident((1, d))]
    return pl.pallas_call(
        functools.partial(_outproj_kernel, n_parts=len(ys)),
        out_shape=jax.ShapeDtypeStruct((b, s, d), F32),
        grid=(b, s // tm),
        in_specs=in_specs,
        out_specs=pl.BlockSpec((1, tm, d), lambda bi, i: (bi, i, 0)),
        compiler_params=_params(2),
        name="outproj",
    )(*ys, *([w] * len(ys)), x, mod, g.reshape(1, d))


def _ffn_kernel(x_ref, xp_ref, xn_ref, mod_ref, gpre_ref, gpost_ref, wu_ref, cw_ref, wd_ref, o_ref,
                h_ref, u_ref, a_ref, *, tm):
    i = pl.program_id(1)
    mod = mod_ref[0]
    gpre = gpre_ref[...]

    def norm_mod(x):
        return _rms(x, gpre) * (1.0 + mod[4:5]) + mod[3:4]

    keep_prev = (i > 0).astype(F32)
    keep_next = (i < pl.num_programs(1) - 1).astype(F32)
    h_ref[0:HALO] = (norm_mod(xp_ref[0]) * keep_prev).astype(BF16)
    h_ref[HALO:HALO + tm] = norm_mod(x_ref[0]).astype(BF16)
    h_ref[HALO + tm:] = (norm_mod(xn_ref[0]) * keep_next).astype(BF16)
    rows = tm + 2 * HALO
    n_u = u_ref.shape[0]

    def up(c):
        h = h_ref[...]
        for part in range(2):
            col = part * D_FF + c * FF_CHUNK
            u_ref[c % n_u, :, part * FF_CHUNK:(part + 1) * FF_CHUNK] = _dot(h, wu_ref[:, col:col + FF_CHUNK])

    def act(c):
        u = u_ref[c % n_u]
        cw = cw_ref[c]
        y = (pltpu.roll(u, 1, 0)[HALO:HALO + tm] * cw[0:1]
             + u[HALO:HALO + tm] * cw[1:2]
             + pltpu.roll(u, rows - 1, 0)[HALO:HALO + tm] * cw[2:3]
             + cw[3:4])
        a_ref[:, c * FF_CHUNK:(c + 1) * FF_CHUNK] = (_silu(y[:, :FF_CHUNK]) * y[:, FF_CHUNK:]).astype(BF16)

    for c in range(min(n_u - 1, N_FF_CHUNKS)):
        up(c)
    for c in range(N_FF_CHUNKS):
        if c + n_u - 1 < N_FF_CHUNKS:
            up(c + n_u - 1)
        act(c)
    o_ref[0] = x_ref[0] + mod[5:6] * _rms(_dot(a_ref[...], wd_ref[...]), gpost_ref[...])


def _ffn(x, mod, g_pre, g_post, wu, cw, wd, layer, tm):
    b, s, d = x.shape
    per_batch = mod.shape[0] > 1
    hb = tm // HALO
    n_halo = s // HALO
    return pl.pallas_call(
        functools.partial(_ffn_kernel, tm=tm),
        out_shape=jax.ShapeDtypeStruct((b, s, d), F32),
        grid=(b, s // tm),
        in_specs=[pl.BlockSpec((1, tm, d), lambda bi, i: (bi, i, 0)),
                  pl.BlockSpec((1, HALO, d), lambda bi, i: (bi, jnp.maximum(i * hb - 1, 0), 0)),
                  pl.BlockSpec((1, HALO, d), lambda bi, i: (bi, jnp.minimum((i + 1) * hb, n_halo - 1), 0)),
                  pl.BlockSpec((1, 6, d), (lambda bi, i: (bi, 0, 0)) if per_batch else (lambda bi, i: (0, 0, 0))),
                  _resident((1, d)), _resident((1, d)),
                  _layer_block(wu, layer), _layer_block(cw, layer), _layer_block(wd, layer)],
        out_specs=pl.BlockSpec((1, tm, d), lambda bi, i: (bi, i, 0)),
        scratch_shapes=[pltpu.VMEM((tm + 2 * HALO, d), BF16),
                        pltpu.VMEM((3, tm + 2 * HALO, 2 * FF_CHUNK), F32),
                        pltpu.VMEM((tm, D_FF), BF16)],
        compiler_params=_params(2),
        name="ffn",
    )(x, x, x, mod, g_pre.reshape(1, d), g_post.reshape(1, d), wu, cw, wd)


def _na_kernel(q_ref, k_ref, v_ref, kc_ref, vc_ref, bias_ref, o_ref, *, rows):
    kc = kc_ref[0]
    vc = vc_ref[0]
    win = NA_WIN_H * GRID_W
    mask_a = _head_mask(GRID_W, 0)
    vc_ext = jnp.concatenate([vc, jnp.ones_like(vc)], axis=1)
    ones_w = jnp.ones((win, LANES), BF16)

    def group(gi, carry):
        starts, scores, probs = [], [], []
        for i in range(NA_GROUP):
            r = gi * NA_GROUP + i
            r0 = jnp.clip(r - NA_WIN_H // 2, 0, rows - NA_WIN_H)
            q0 = pl.multiple_of(r * GRID_W, GRID_W)
            k0 = pl.multiple_of(r0 * GRID_W, GRID_W)
            q = q_ref[0, pl.ds(q0, GRID_W), :]
            zero = jnp.zeros_like(q)
            q2 = jnp.concatenate([jnp.where(mask_a, q, zero), jnp.where(mask_a, zero, q)], axis=0)
            d0 = r0 - r + NA_WIN_H - 1
            bias = jnp.concatenate(
                [jnp.concatenate([bias_ref[0, hh, d0 + 2 * a] for a in range(NA_WIN_H // 2)], axis=1)
                 for hh in range(2)], axis=0)
            s = _dot_nt(q2, k_ref[0, pl.ds(k0, win), :]) + bias
            sc = _dot_nt(q2, kc)
            starts.append((q0, k0))
            scores.append((s, sc))
        for s, sc in scores:
            m = jnp.maximum(jnp.max(s, axis=-1, keepdims=True), jnp.max(sc, axis=-1, keepdims=True))
            probs.append((jnp.exp2(s - m).astype(BF16), jnp.exp2(sc - m).astype(BF16)))
        for (q0, k0), (p, pc) in zip(starts, probs):
            v_ext = jnp.concatenate([v_ref[0, pl.ds(k0, win), :], ones_w], axis=1)
            o_ext = _dot(p, v_ext) + _dot(pc, vc_ext)
            o2 = o_ext[:, :LANES] / o_ext[:, LANES:]
            o_ref[0, pl.ds(q0, GRID_W), :] = jnp.where(mask_a, o2[:GRID_W], o2[GRID_W:]).astype(BF16)
        return carry

    lax.fori_loop(0, rows // NA_GROUP, group, 0)


def _na(pl_, pc_, bias):
    b, s, _ = pl_.shape
    n_ctx = pc_.shape[1]
    n_pairs = NA_HEADS // 2
    return pl.pallas_call(
        functools.partial(_na_kernel, rows=s // GRID_W),
        out_shape=jax.ShapeDtypeStruct((b, s, NA_WIDTH), BF16),
        grid=(b, n_pairs),
        in_specs=[pl.BlockSpec((1, s, LANES), lambda bi, p: (bi, 0, AB_QA + p)),
                  pl.BlockSpec((1, s, LANES), lambda bi, p: (bi, 0, AB_KA + p)),
                  pl.BlockSpec((1, s, LANES), lambda bi, p: (bi, 0, AB_VA + p)),
                  pl.BlockSpec((1, n_ctx, LANES), lambda bi, p: (bi, 0, AB_KA + p)),
                  pl.BlockSpec((1, n_ctx, LANES), lambda bi, p: (bi, 0, AB_VA + p)),
                  pl.BlockSpec((1,) + bias.shape[1:], lambda bi, p: (p, 0, 0, 0, 0))],
        out_specs=pl.BlockSpec((1, s, LANES), lambda bi, p: (bi, 0, p)),
        compiler_params=_params(2),
        name="na",
    )(pl_, pl_, pl_, pc_, pc_, bias)


def _na_bias(rpb):
    h = rpb.shape[0]
    cidx = np.arange(GRID_W)
    c_start = np.clip(cidx - NA_WIN_W // 2, 0, GRID_W - NA_WIN_W)
    col_ok = (cidx[None, :] >= c_start[:, None]) & (cidx[None, :] < c_start[:, None] + NA_WIN_W)
    lo, hi = GRID_W - NA_WIN_W, GRID_W - NA_WIN_W
    rp = rpb.astype(F32) * LOG2E
    ext = jnp.concatenate([jnp.repeat(rp[..., :1], lo, axis=-1), rp, jnp.repeat(rp[..., -1:], hi, axis=-1)], axis=-1)
    bm = jnp.stack([ext[..., GRID_W - 1 - q:2 * GRID_W - 1 - q] for q in range(GRID_W)], axis=2)
    bm = jnp.where(col_ok[None, None], bm, NEG_INF)
    pairs = jnp.concatenate([bm[:, :-1], bm[:, 1:]], axis=-1)
    return pairs.reshape((h // 2, 2) + pairs.shape[1:])


def _ctx_attn_kernel(*refs, has_sink):
    if has_sink:
        q_ref, k_ref, v_ref, sink_ref, o_ref = refs
    else:
        q_ref, k_ref, v_ref, o_ref = refs
    q, k, v = q_ref[0], k_ref[0], v_ref[0]
    n = q.shape[0]
    masks = [_head_mask(n, 0), _head_mask(n, 1)]
    outs = []
    for h in range(2):
        qm = jnp.where(masks[h], q, jnp.zeros_like(q))
        s = _dot_nt(qm, k)
        m = jnp.max(s, axis=-1, keepdims=True)
        if has_sink:
            sk = sink_ref[0, h:h + 1, 0:1]
            m = jnp.maximum(m, sk)
        p = jnp.exp2(s - m)
        den = jnp.sum(p, axis=-1, keepdims=True)
        if has_sink:
            den = den + jnp.exp2(sk - m)
        outs.append(_dot(p.astype(BF16), v) / den)
    o_ref[0] = jnp.where(masks[0], outs[0], outs[1]).astype(BF16)


def _ctx_attn(pc_, q_off, k_off, v_off, n_q_tiles, group, sink=None):
    b, n, _ = pc_.shape
    in_specs = [pl.BlockSpec((1, n, LANES), lambda bi, t: (bi, 0, q_off + t)),
                pl.BlockSpec((1, n, LANES), lambda bi, t: (bi, 0, k_off + t // group)),
                pl.BlockSpec((1, n, LANES), lambda bi, t: (bi, 0, v_off + t // group))]
    args = [pc_, pc_, pc_]
    if sink is not None:
        in_specs.append(pl.BlockSpec((1, 2, LANES), lambda bi, t: (t, 0, 0)))
        args.append(sink)
    return pl.pallas_call(
        functools.partial(_ctx_attn_kernel, has_sink=sink is not None),
        out_shape=jax.ShapeDtypeStruct((b, n, n_q_tiles * LANES), BF16),
        grid=(b, n_q_tiles),
        in_specs=in_specs,
        out_specs=pl.BlockSpec((1, n, LANES), lambda bi, t: (bi, 0, t)),
        compiler_params=_params(2),
        name="ctx_attn",
    )(*args)


def _gla_kernel(*refs, n_lat_blocks, need_ctx):
    (ql, kl, vl, zl, rl, qc, kc, vc, zc, rc, w2f, w2b, baf, bab, g_ref) = refs[:15]
    n_out = 2 if need_ctx else 1
    yl_ref = refs[15]
    yc_ref = refs[16] if need_ctx else None
    qt_s, kd_s, dec_s, vt_s, o_s = refs[15 + n_out:]
    blk, ck = GLA_BLOCK, GLA_CHUNK
    cpb = blk // ck
    shift = ck.bit_length() - 1
    ri = lax.broadcasted_iota(jnp.int32, (blk, blk), 0)
    ci = lax.broadcasted_iota(jnp.int32, (blk, blk), 1)
    same = lax.shift_right_logical(ri, shift) == lax.shift_right_logical(ci, shift)
    att_mask = [same & (ci <= ri), same & (ci >= ri)]
    as_bf16 = lambda m: jnp.where(m, 1.0, 0.0).astype(BF16)
    cum = [jnp.concatenate([as_bf16(att_mask[d]), as_bf16(same)], axis=0) for d in range(2)]
    head = [_head_mask(blk, 0), _head_mask(blk, 1)]
    head_ck = [_head_mask(ck, 0), _head_mask(ck, 1)]
    tok_lo = lax.broadcasted_iota(jnp.int32, (2 * GLA_DV, LANES), 1) < ck
    w2 = jnp.concatenate([w2f[...], w2b[...]], axis=1)
    ba = jnp.concatenate([baf[...], bab[...]], axis=1)
    g = g_ref[...]
    lat4, ctx4 = (ql, kl, vl, zl), (qc, kc, vc, zc)

    def phase1(blocks):
        gates = []
        for (_, _, _, z_ref), t_in, _ in blocks:
            x = _dot(z_ref[0, pl.ds(t_in, blk), :], w2) + ba
            la = (jnp.minimum(x, 0.0) - jnp.log(1.0 + jnp.exp(-jnp.abs(x)))) * (1.0 / GLA_TAU)
            hi = la.astype(BF16)
            gates.append((hi, (la - hi.astype(F32)).astype(BF16)))
        cums = []
        for hi, lo in gates:
            per_dir = []
            for d in range(2):
                c2 = _dot(cum[d], jnp.concatenate([hi[:, d * LANES:(d + 1) * LANES],
                                                   lo[:, d * LANES:(d + 1) * LANES]], axis=1))
                per_dir.append((c2[:blk, :LANES] + c2[:blk, LANES:], c2[blk:, :LANES] + c2[blk:, LANES:]))
            cums.append(per_dir)
        scores = []
        for ((q_ref, k_ref, _, _), t_in, bo), per_dir in zip(blocks, cums):
            q = q_ref[0, pl.ds(t_in, blk), :].astype(F32)
            k = k_ref[0, pl.ds(t_in, blk), :].astype(F32)
            t_out = pl.multiple_of(bo * blk, blk)
            items = []
            for d, (b, tot) in enumerate(per_dir):
                qt = q * jnp.exp(b)
                kt = (k * jnp.exp(-b)).astype(BF16)
                qt_s[d, pl.ds(t_out, blk), :] = qt.astype(BF16)
                kd_s[d, pl.ds(t_out, blk), :] = (k * jnp.exp(tot - b)).astype(BF16)
                for j in range(cpb):
                    dec_s[d, pl.ds(bo * cpb + j, 1), :] = jnp.exp(tot[j * ck:j * ck + 1])
                for h in range(2):
                    items.append((d, h, _dot_nt(jnp.where(head[h], qt, 0.0).astype(BF16), kt)))
            scores.append(items)
        for ((_, _, v_ref, _), t_in, bo), items in zip(blocks, scores):
            v = v_ref[0, pl.ds(t_in, blk), :]
            oi = [None, None]
            for d, h, s in items:
                c = _dot(jnp.where(att_mask[d], s, 0.0).astype(BF16), v[:, h * GLA_DV:(h + 1) * GLA_DV])
                oi[h] = c if oi[h] is None else oi[h] + c
            o_s[pl.ds(pl.multiple_of(bo * blk, blk), blk), :] = jnp.concatenate(oi, axis=1)
            vt_s[bo] = v.astype(F32).T.astype(BF16)

    def scan_block(bo, st, d, r_ref=None, t_in=None, y_ref=None):
        t0 = pl.multiple_of(bo * blk, blk)
        vt = vt_s[bo]
        kd = kd_s[d, pl.ds(t0, blk), :]
        order = list(range(cpb)) if d == 0 else list(range(cpb - 1, -1, -1))
        upd = {}
        for c in order:
            half = c // 2
            vt_c = vt[:, half * LANES:(half + 1) * LANES]
            vt_c = jnp.where(tok_lo if c % 2 == 0 else ~tok_lo, vt_c, jnp.zeros_like(vt_c))
            upd[c] = _dot(vt_c, kd[half * LANES:(half + 1) * LANES])
        for c in order:
            tc = pl.multiple_of(t0 + c * ck, ck)
            qt = qt_s[d, pl.ds(tc, ck), :]
            stb = st.astype(BF16)
            outs = [_dot_nt(jnp.where(head_ck[h], qt, jnp.zeros_like(qt)), stb[h * GLA_DV:(h + 1) * GLA_DV])
                    for h in range(2)]
            o_s[pl.ds(tc, ck), :] += jnp.concatenate(outs, axis=1)
            st = dec_s[d, pl.ds(bo * cpb + c, 1), :] * st + upd[c]
        if y_ref is not None:
            o = o_s[pl.ds(t0, blk), :]
            r = r_ref[0, pl.ds(t_in, blk), :].astype(F32)
            on = jnp.concatenate(
                [o[:, h * GLA_DV:(h + 1) * GLA_DV]
                 * lax.rsqrt(jnp.mean(o[:, h * GLA_DV:(h + 1) * GLA_DV] ** 2, axis=-1, keepdims=True) + EPS)
                 for h in range(2)], axis=1)
            y_ref[0, pl.ds(t_in, blk), :] = (on * g * _silu(r)).astype(BF16)
        return st

    phase1([(ctx4, 0, 0)])

    def phase1_lat(i, carry):
        phase1([(lat4, pl.multiple_of((2 * i + j) * blk, blk), 1 + 2 * i + j) for j in range(2)])
        return carry

    lax.fori_loop(0, n_lat_blocks // 2, phase1_lat, 0)

    zero = jnp.zeros((2 * GLA_DV, 2 * GLA_DK), F32)
    st = scan_block(0, zero, 0)
    lax.fori_loop(0, n_lat_blocks, lambda i, s: scan_block(1 + i, s, 0), st)
    st = scan_block(0, zero, 1, rc, 0, yc_ref)
    lax.fori_loop(
        0, n_lat_blocks,
        lambda i, s: scan_block(n_lat_blocks - i, s, 1, rl,
                                pl.multiple_of((n_lat_blocks - 1 - i) * blk, blk), yl_ref), st)


def _gla(pl_, pc_, w2, ba, g, need_ctx):
    b, s, _ = pl_.shape
    n_c = pc_.shape[1]
    n_pairs = GLA_HEADS // 2
    w = 2 * GLA_DV

    def stream(n):
        return [pl.BlockSpec((1, n, LANES), lambda bi, p: (bi, 0, AB_QB + p)),
                pl.BlockSpec((1, n, LANES), lambda bi, p: (bi, 0, AB_KB + p)),
                pl.BlockSpec((1, n, w), lambda bi, p: (bi, 0, AB_VB + p)),
                pl.BlockSpec((1, n, LANES), lambda bi, p: (bi, 0, AB_Z)),
                pl.BlockSpec((1, n, w), lambda bi, p: (bi, 0, AB_RB + p))]

    in_specs = stream(s) + stream(n_c) + [
        pl.BlockSpec((LANES, LANES), lambda bi, p: (0, p)),
        pl.BlockSpec((LANES, LANES), lambda bi, p: (0, n_pairs + p)),
        pl.BlockSpec((1, LANES), lambda bi, p: (0, p)),
        pl.BlockSpec((1, LANES), lambda bi, p: (0, n_pairs + p)),
        pl.BlockSpec((1, w), lambda bi, p: (0, p))]
    out_shape = [jax.ShapeDtypeStruct((b, s, GLA_VW), BF16)]
    out_specs = [pl.BlockSpec((1, s, w), lambda bi, p: (bi, 0, p))]
    if need_ctx:
        out_shape.append(jax.ShapeDtypeStruct((b, n_c, GLA_VW), BF16))
        out_specs.append(pl.BlockSpec((1, n_c, w), lambda bi, p: (bi, 0, p)))
    assert n_c == GLA_BLOCK and s % (2 * GLA_BLOCK) == 0
    n_lat_blocks = s // GLA_BLOCK
    t_all = n_c + s
    n_chunks = -(-(t_all // GLA_CHUNK) // 8) * 8
    scratch = [pltpu.VMEM((2, t_all, LANES), BF16),
               pltpu.VMEM((2, t_all, LANES), BF16),
               pltpu.VMEM((2, n_chunks, LANES), F32),
               pltpu.VMEM((1 + n_lat_blocks, w, GLA_BLOCK), BF16),
               pltpu.VMEM((t_all, w), F32)]
    res = pl.pallas_call(
        functools.partial(_gla_kernel, n_lat_blocks=n_lat_blocks, need_ctx=need_ctx),
        out_shape=out_shape,
        grid=(b, n_pairs),
        in_specs=in_specs,
        out_specs=out_specs,
        scratch_shapes=scratch,
        compiler_params=_params(2),
        name="gla",
    )(*([pl_] * 5), *([pc_] * 5), w2, w2, ba, ba, g.reshape(1, GLA_VW))
    return (res[0], res[1]) if need_ctx else (res[0], None)


def _gla_gate_weights(wa2, ba):
    w = jnp.zeros((LANES, 2 * GLA_KW), F32)
    w = w.at[0:GLA_RANK, 0:GLA_KW].set(wa2[0])
    w = w.at[GLA_RANK:2 * GLA_RANK, GLA_KW:].set(wa2[1])
    return w.astype(BF16), ba.reshape(1, 2 * GLA_KW).astype(F32)


def _swa_kernel(q_ref, kp_ref, kn_ref, kx_ref, vp_ref, vn_ref, vx_ref, kc_ref, vc_ref, sink_ref, o_ref):
    n = pl.program_id(1)
    blk = SWA_BLOCK
    n_c = kc_ref.shape[1]
    n_keys = 3 * blk + n_c
    k_lat = jnp.concatenate([kp_ref[0], kn_ref[0], kx_ref[0]], axis=0)
    v_lat = jnp.concatenate([vp_ref[0], vn_ref[0], vx_ref[0]], axis=0)
    ones = jnp.ones((n_keys, LANES), BF16)
    qi = lax.broadcasted_iota(jnp.int32, (blk, n_keys), 0)
    kj = lax.broadcasted_iota(jnp.int32, (blk, n_keys), 1)
    in_prev = (kj < blk) & (kj >= qi)
    in_next = (kj >= 2 * blk) & (kj < 3 * blk) & (kj - 2 * blk <= qi)
    always = (kj >= 3 * blk) | ((kj >= blk) & (kj < 2 * blk))
    first, last = n == 0, n == pl.num_programs(1) - 1
    mask_a = _head_mask(blk, 0)
    half = SWA_GROUP // 2
    n_pairs = SWA_KV // 2
    n_chunks = SWA_QB * n_pairs * 2

    def block_inputs(j):
        k = jnp.concatenate([k_lat[j * blk:(j + 3) * blk], kc_ref[0]], axis=0)
        v = jnp.concatenate([v_lat[j * blk:(j + 3) * blk], vc_ref[0]], axis=0)
        has_prev = jnp.logical_not(first) if j == 0 else True
        has_next = jnp.logical_not(last) if j == SWA_QB - 1 else True
        return k, v, always | (in_prev & has_prev) | (in_next & has_next)

    blocks = [block_inputs(j) for j in range(SWA_QB)]

    def scores(c):
        j, pair = c // (2 * n_pairs), (c // 2) % n_pairs
        k, _, ok = blocks[j]
        tiles = [pair * SWA_GROUP + (c % 2) * half + g for g in range(half)]
        q_parts, sink_parts = [], []
        for t in tiles:
            q = q_ref[0, j * blk:(j + 1) * blk, t * LANES:(t + 1) * LANES]
            zero = jnp.zeros_like(q)
            q_parts += [jnp.where(mask_a, q, zero), jnp.where(mask_a, zero, q)]
            sink_parts += [jnp.broadcast_to(sink_ref[t, h:h + 1, 0:1], (blk, 1)) for h in range(2)]
        s = _dot_nt(jnp.concatenate(q_parts, axis=0), k[:, pair * LANES:(pair + 1) * LANES])
        s = jnp.concatenate([jnp.where(ok, s[i * blk:(i + 1) * blk], NEG_INF) for i in range(2 * half)], axis=0)
        return j, tiles, pair, s, jnp.concatenate(sink_parts, axis=0)

    def finish(j, tiles, pair, s, sk):
        m = jnp.maximum(jnp.max(s, axis=-1, keepdims=True), sk)
        p = jnp.exp2(s - m).astype(BF16)
        o_ext = _dot(p, jnp.concatenate([blocks[j][1][:, pair * LANES:(pair + 1) * LANES], ones], axis=1))
        o = o_ext[:, :LANES] / (o_ext[:, LANES:] + jnp.exp2(sk - m))
        for i, t in enumerate(tiles):
            o_ref[0, j * blk:(j + 1) * blk, t * LANES:(t + 1) * LANES] = jnp.where(
                mask_a, o[2 * i * blk:(2 * i + 1) * blk], o[(2 * i + 1) * blk:(2 * i + 2) * blk]).astype(BF16)

    pending = scores(0)
    for c in range(n_chunks):
        nxt = scores(c + 1) if c + 1 < n_chunks else None
        finish(*pending)
        pending = nxt


def _swa(pl_, pc_, sink):
    b, s, _ = pl_.shape
    n_c = pc_.shape[1]
    nb = s // SWA_BLOCK
    step = SWA_QB * SWA_BLOCK
    assert s % step == 0
    qw = SWA_HEADS * HEAD_DIM
    kw = SWA_KV * HEAD_DIM
    k_blk, v_blk = (C_K * LANES) // kw, (C_V * LANES) // kw

    def kv(off):
        return [pl.BlockSpec((1, SWA_BLOCK, kw), lambda bi, n: (bi, jnp.maximum(n * SWA_QB - 1, 0), off)),
                pl.BlockSpec((1, step, kw), lambda bi, n: (bi, n, off)),
                pl.BlockSpec((1, SWA_BLOCK, kw), lambda bi, n: (bi, jnp.minimum((n + 1) * SWA_QB, nb - 1), off))]

    in_specs = ([pl.BlockSpec((1, step, qw), lambda bi, n: (bi, n, 0))] + kv(k_blk) + kv(v_blk)
                + [pl.BlockSpec((1, n_c, kw), lambda bi, n: (bi, 0, k_blk)),
                   pl.BlockSpec((1, n_c, kw), lambda bi, n: (bi, 0, v_blk)),
                   _resident(sink.shape)])
    return pl.pallas_call(
        _swa_kernel,
        out_shape=jax.ShapeDtypeStruct((b, s, qw), BF16),
        grid=(b, s // step),
        in_specs=in_specs,
        out_specs=pl.BlockSpec((1, step, qw), lambda bi, n: (bi, n, 0)),
        compiler_params=_params(2),
        name="swa",
    )(pl_, pl_, pl_, pl_, pl_, pl_, pl_, pc_, pc_, sink)


def _swa_pair_heads(a, axis):
    shape = a.shape
    a = a.reshape(shape[:axis] + (SWA_KV // 2, 2, SWA_GROUP, -1) + shape[axis + 1:])
    return jnp.swapaxes(a, axis + 1, axis + 2).reshape(shape)


def _rope_tables(n):
    t = jnp.arange(n)
    row = (t // GRID_W).astype(F32)
    col = (t % GRID_W).astype(F32)
    half = HEAD_DIM // 2
    inv = 1.0 / (ROPE_BASE ** (jnp.arange(0, half, 2, dtype=F32) / half))
    ang = jnp.concatenate([row[:, None] * inv, col[:, None] * inv], axis=-1)
    cos = jnp.repeat(jnp.cos(ang), 2, axis=1)
    sin = jnp.repeat(jnp.sin(ang), 2, axis=1) * jnp.tile(jnp.array([-1.0, 1.0], F32), half)
    reps = ROPE_TILE // HEAD_DIM
    return jnp.tile(cos, (1, reps)), jnp.tile(sin, (1, reps))


def kernel(x, c, ctx, c_ctx, w_mod, b_mod, g_mix_pre, g_mix_post, g_ffn_pre, g_ffn_post, w_out, w_up, conv_w,
           conv_b, w_down, w_in_ab, na_rpb, gla_wa2, gla_ba, gla_g, w_in_c, swa_sink):
    bsz, seq, d = x.shape
    n_ctx = ctx.shape[1]
    depth = w_mod.shape[0]
    tm_l = 512 if seq % 512 == 0 else 256
    tm_c = n_ctx

    c_all = jnp.zeros((8, d), F32).at[:bsz].set(c).at[bsz].set(c_ctx)
    mods = _modulation(c_all, w_mod, b_mod).reshape(depth, 8, 6, d)
    rope = _rope_tables(seq)

    col_scale_ab = np.ones((IN_AB,), np.float32)
    col_scale_ab[0:NA_WIDTH] = ATTN_SCALE
    col_scale_ab[3 * NA_WIDTH:3 * NA_WIDTH + GLA_KW] = GLA_DK ** -0.5

    n_q = SWA_HEADS * HEAD_DIM
    w_ab = jnp.pad(w_in_ab * col_scale_ab, ((0, 0), (0, 0), (0, IN_AB_PAD - IN_AB))).astype(BF16)
    w_c = jnp.concatenate([_swa_pair_heads(w_in_c[:, :, :n_q], 2) * ATTN_SCALE, w_in_c[:, :, n_q:]],
                          axis=2).astype(BF16)
    w_o = w_out.astype(BF16)
    w_o_swa = _swa_pair_heads(w_out[1::2], 1).astype(BF16)
    wu = w_up.astype(BF16)
    wd = w_down.astype(BF16)
    cwb = jnp.concatenate([conv_w, conv_b[:, None]], axis=1)
    cw = jnp.concatenate([cwb[:, :, :D_FF].reshape(depth, 4, N_FF_CHUNKS, FF_CHUNK),
                          cwb[:, :, D_FF:].reshape(depth, 4, N_FF_CHUNKS, FF_CHUNK)], axis=3).transpose(0, 2, 1, 3)

    xl, xc = x, ctx
    for i in range(depth):
        need_ctx = i < depth - 1
        j = i // 2
        mod_l = mods[i, :bsz]
        mod_c = mods[i, bsz:bsz + 1]
        if i % 2 == 0:
            p_l = _proj(xl, mod_l, g_mix_pre[i], w_ab, j, tm_l, 640)
            p_c = _proj(xc, mod_c, g_mix_pre[i], w_ab, j, tm_c, 640)
            w2, ba = _gla_gate_weights(gla_wa2[j], gla_ba[j])
            y_na = _na(p_l, p_c, _na_bias(na_rpb[j]))
            y_gla, yc_gla = _gla(p_l, p_c, w2, ba, gla_g[j], need_ctx)
            ys_l = [y_na, y_gla]
            w_post, l_post = w_o, i
            if need_ctx:
                ys_c = [_ctx_attn(p_c, AB_QA, AB_KA, AB_VA, NA_HEADS // 2, 1), yc_gla]
        else:
            p_l = _proj(xl, mod_l, g_mix_pre[i], w_c, j, tm_l, ROPE_TILE, rope, (C_V * LANES) // ROPE_TILE)
            p_c = _proj(xc, mod_c, g_mix_pre[i], w_c, j, tm_c, ROPE_TILE)
            sink = jnp.broadcast_to(_swa_pair_heads(swa_sink[j] * LOG2E, 0).reshape(-1, 2, 1),
                                    (SWA_HEADS // 2, 2, LANES)).astype(F32)
            ys_l = [_swa(p_l, p_c, sink)]
            w_post, l_post = w_o_swa, j
            if need_ctx:
                ys_c = [_ctx_attn(p_c, C_Q, C_K, C_V, SWA_HEADS // 2, SWA_GROUP, sink)]

        xl = _outproj(ys_l, w_post, l_post, xl, mod_l, g_mix_post[i], tm_l)
        xl = _ffn(xl, mod_l, g_ffn_pre[i], g_ffn_post[i], wu, cw, wd, i, tm_l)
        if need_ctx:
            xc = _outproj(ys_c, w_post, l_post, xc, mod_c, g_mix_post[i], tm_c)
            xc = _ffn(xc, mod_c, g_ffn_pre[i], g_ffn_post[i], wu, cw, wd, i, tm_c)
    return xl
```

```python
import functools

import numpy as np
import jax
import jax.numpy as jnp
from jax import lax
from jax.experimental import pallas as pl
from jax.experimental.pallas import tpu as pltpu

F32 = jnp.float32
BF16 = jnp.bfloat16

LANES = 128
HEAD_DIM = 64
GRID_W = 64
D_MODEL = 1024
EPS = 1e-6
NEG_INF = -1e30
LOG2E = float(np.log2(np.e))
ATTN_SCALE = HEAD_DIM ** -0.5 * LOG2E

NA_HEADS = 8
NA_WIN_H = 8
NA_WIN_W = 16
NA_WIDTH = NA_HEADS * HEAD_DIM
NA_GROUP = 16
GLA_HEADS = 4
GLA_DK = 64
GLA_DV = 128
GLA_KW = GLA_HEADS * GLA_DK
GLA_VW = GLA_HEADS * GLA_DV
GLA_RANK = 16
GLA_TAU = 16.0
GLA_CHUNK = 64
GLA_BLOCK = 256
IN_AB = 3 * NA_WIDTH + 2 * GLA_KW + 2 * GLA_VW + 2 * GLA_RANK
IN_AB_PAD = 3200
SWA_HEADS = 16
SWA_KV = 4
SWA_GROUP = SWA_HEADS // SWA_KV
SWA_BLOCK = 128
SWA_QB = 2
IN_C = (SWA_HEADS + 2 * SWA_KV) * HEAD_DIM
D_FF = 2816
FF_CHUNK = 256
N_FF_CHUNKS = D_FF // FF_CHUNK
assert N_FF_CHUNKS * FF_CHUNK == D_FF
HALO = 16
OUTPROJ_SUB = 256
ROPE_BASE = 10000.0
ROPE_TILE = 256
VMEM_LIMIT = 56 * 1024 * 1024

AB_QA, AB_KA, AB_VA = 0, 4, 8
AB_QB, AB_KB = 12, 14
AB_VB, AB_RB = 8, 10
AB_Z = 24
C_Q, C_K, C_V = 0, 8, 10


def _dot(a, b):
    return jnp.dot(a, b, preferred_element_type=F32)


def _dot_nt(a, b):
    return lax.dot_general(a, b, (((1,), (1,)), ((), ())), preferred_element_type=F32)


def _rms(x, g):
    return x * lax.rsqrt(jnp.mean(x * x, axis=-1, keepdims=True) + EPS) * g


def _silu(x):
    return x / (1.0 + jnp.exp(-x))


def _params(n_grid):
    return pltpu.CompilerParams(dimension_semantics=("arbitrary",) * n_grid,
                                vmem_limit_bytes=VMEM_LIMIT)


def _resident(shape):
    nd = len(shape)
    return pl.BlockSpec(shape, lambda *_: (0,) * nd, pipeline_mode=pl.Buffered(1))


def _layer_block(arr, layer, rows=None, row_block=0):
    shape = list(arr.shape[1:])
    if rows is not None:
        shape[0] = rows
    tail = (0,) * (len(shape) - 1)
    return pl.BlockSpec((None, *shape), lambda *_: (layer, row_block) + tail, pipeline_mode=pl.Buffered(1))


def _head_mask(rows, h):
    lane = lax.broadcasted_iota(jnp.int32, (rows, LANES), 1)
    return (lane < HEAD_DIM) if h == 0 else (lane >= HEAD_DIM)


def _mod_kernel(c_ref, w_ref, b_ref, o_ref):
    s = _silu(c_ref[...]).astype(BF16)
    o_ref[0] = _dot(s, w_ref[0].astype(BF16)) + b_ref[0]


def _modulation(c_all, w_mod, b_mod):
    depth, d, n = w_mod.shape
    tn = 1536
    return pl.pallas_call(
        _mod_kernel,
        out_shape=jax.ShapeDtypeStruct((depth, 8, n), F32),
        grid=(depth, n // tn),
        in_specs=[pl.BlockSpec((8, d), lambda i, j: (0, 0)),
                  pl.BlockSpec((1, d, tn), lambda i, j: (i, 0, j)),
                  pl.BlockSpec((1, 1, tn), lambda i, j: (i, 0, j))],
        out_specs=pl.BlockSpec((1, 8, tn), lambda i, j: (i, 0, j)),
        compiler_params=_params(2),
        name="modulation",
    )(c_all, w_mod, b_mod.reshape(depth, 1, n))


def _rope(acc, cos, sin):
    n = acc.shape[1]
    lane = lax.broadcasted_iota(jnp.int32, acc.shape, 1)
    partner = jnp.where(lane % 2 == 0, pltpu.roll(acc, n - 1, 1), pltpu.roll(acc, 1, 1))
    return acc * cos + partner * sin


def _proj_kernel(*refs, chunk, rope_chunks, shift_idx):
    if rope_chunks:
        x_ref, mod_ref, g_ref, w_ref, cos_ref, sin_ref, o_ref = refs
    else:
        x_ref, mod_ref, g_ref, w_ref, o_ref = refs
    mod = mod_ref[0]
    h = _rms(x_ref[0], g_ref[...]) * (1.0 + mod[shift_idx + 1:shift_idx + 2]) + mod[shift_idx:shift_idx + 1]
    hb = h.astype(BF16)
    for c in range(w_ref.shape[1] // chunk):
        acc = _dot(hb, w_ref[:, c * chunk:(c + 1) * chunk])
        if c < rope_chunks:
            acc = _rope(acc, cos_ref[...], sin_ref[...])
        o_ref[0, :, c * chunk:(c + 1) * chunk] = acc.astype(BF16)


def _proj(x, mod, g, w, layer, tm, chunk, rope=None, rope_chunks=0):
    b, s, d = x.shape
    n = w.shape[2]
    per_batch = mod.shape[0] > 1
    in_specs = [pl.BlockSpec((1, tm, d), lambda bi, i: (bi, i, 0)),
                pl.BlockSpec((1, 6, d), (lambda bi, i: (bi, 0, 0)) if per_batch else (lambda bi, i: (0, 0, 0))),
                _resident((1, d)),
                _layer_block(w, layer)]
    args = [x, mod, g.reshape(1, d), w]
    if rope_chunks:
        in_specs += [pl.BlockSpec((tm, chunk), lambda bi, i: (i, 0))] * 2
        args += list(rope)
    return pl.pallas_call(
        functools.partial(_proj_kernel, chunk=chunk, rope_chunks=rope_chunks, shift_idx=0),
        out_shape=jax.ShapeDtypeStruct((b, s, n), BF16),
        grid=(b, s // tm),
        in_specs=in_specs,
        out_specs=pl.BlockSpec((1, tm, n), lambda bi, i: (bi, i, 0)),
        compiler_params=_params(2),
        name="proj",
    )(*args)


def _outproj_kernel(*refs, n_parts):
    y_refs, w_refs = refs[:n_parts], refs[n_parts:2 * n_parts]
    x_ref, mod_ref, g_ref, o_ref = refs[2 * n_parts:]
    tm = x_ref.shape[1]
    sub = min(tm, OUTPROJ_SUB)
    gate, g = mod_ref[0][2:3], g_ref[...]
    accs = []
    for r0 in range(0, tm, sub):
        acc = _dot(y_refs[0][0, r0:r0 + sub], w_refs[0][...])
        for yr, wr in zip(y_refs[1:], w_refs[1:]):
            acc = acc + _dot(yr[0, r0:r0 + sub], wr[...])
        accs.append(acc)
    for k, acc in enumerate(accs):
        o_ref[0, k * sub:(k + 1) * sub] = x_ref[0, k * sub:(k + 1) * sub] + gate * _rms(acc, g)


def _outproj(ys, w, layer, x, mod, g, tm):
    b, s, d = x.shape
    per_batch = mod.shape[0] > 1
    in_specs = [pl.BlockSpec((1, tm, y.shape[2]), lambda bi, i: (bi, i, 0)) for y in ys]
    in_specs += [_layer_block(w, layer, y.shape[2], k) for k, y in enumerate(ys)]
    in_specs += [pl.BlockSpec((1, tm, d), lambda bi, i: (bi, i, 0)),
                 pl.BlockSpec((1, 6, d), (lambda bi, i: (bi, 0, 0)) if per_batch else (lambda bi, i: (0, 0, 0))),
                 _resident((1, d))]
    return pl.pallas_call(
        functools.partial(_outproj_kernel, n_parts=len(ys)),
        out_shape=jax.ShapeDtypeStruct((b, s, d), F32),
        grid=(b, s // tm),
        in_specs=in_specs,
        out_specs=pl.BlockSpec((1, tm, d), lambda bi, i: (bi, i, 0)),
        compiler_params=_params(2),
        name="outproj",
    )(*ys, *([w] * len(ys)), x, mod, g.reshape(1, d))


def _ffn_kernel(x_ref, xp_ref, xn_ref, mod_ref, gpre_ref, gpost_ref, wu_ref, cw_ref, wd_ref, o_ref,
                h_ref, u_ref, a_ref, *, tm):
    i = pl.program_id(1)
    mod = mod_ref[0]
    gpre = gpre_ref[...]

    def norm_mod(x):
        return _rms(x, gpre) * (1.0 + mod[4:5]) + mod[3:4]

    keep_prev = (i > 0).astype(F32)
    keep_next = (i < pl.num_programs(1) - 1).astype(F32)
    h_ref[0:HALO] = (norm_mod(xp_ref[0]) * keep_prev).astype(BF16)
    h_ref[HALO:HALO + tm] = norm_mod(x_ref[0]).astype(BF16)
    h_ref[HALO + tm:] = (norm_mod(xn_ref[0]) * keep_next).astype(BF16)
    rows = tm + 2 * HALO
    n_u = u_ref.shape[0]

    def up(c):
        h = h_ref[...]
        for part in range(2):
            col = part * D_FF + c * FF_CHUNK
            u_ref[c % n_u, :, part * FF_CHUNK:(part + 1) * FF_CHUNK] = _dot(h, wu_ref[:, col:col + FF_CHUNK])

    def act(c):
        u = u_ref[c % n_u]
        cw = cw_ref[c]
        y = (pltpu.roll(u, 1, 0)[HALO:HALO + tm] * cw[0:1]
             + u[HALO:HALO + tm] * cw[1:2]
             + pltpu.roll(u, rows - 1, 0)[HALO:HALO + tm] * cw[2:3]
             + cw[3:4])
        a_ref[:, c * FF_CHUNK:(c + 1) * FF_CHUNK] = (_silu(y[:, :FF_CHUNK]) * y[:, FF_CHUNK:]).astype(BF16)

    for c in range(min(n_u - 1, N_FF_CHUNKS)):
        up(c)
    for c in range(N_FF_CHUNKS):
        if c + n_u - 1 < N_FF_CHUNKS:
            up(c + n_u - 1)
        act(c)
    o_ref[0] = x_ref[0] + mod[5:6] * _rms(_dot(a_ref[...], wd_ref[...]), gpost_ref[...])


def _ffn(x, mod, g_pre, g_post, wu, cw, wd, layer, tm):
    b, s, d = x.shape
    per_batch = mod.shape[0] > 1
    hb = tm // HALO
    n_halo = s // HALO
    return pl.pallas_call(
        functools.partial(_ffn_kernel, tm=tm),
        out_shape=jax.ShapeDtypeStruct((b, s, d), F32),
        grid=(b, s // tm),
        in_specs=[pl.BlockSpec((1, tm, d), lambda bi, i: (bi, i, 0)),
                  pl.BlockSpec((1, HALO, d), lambda bi, i: (bi, jnp.maximum(i * hb - 1, 0), 0)),
                  pl.BlockSpec((1, HALO, d), lambda bi, i: (bi, jnp.minimum((i + 1) * hb, n_halo - 1), 0)),
                  pl.BlockSpec((1, 6, d), (lambda bi, i: (bi, 0, 0)) if per_batch else (lambda bi, i: (0, 0, 0))),
                  _resident((1, d)), _resident((1, d)),
                  _layer_block(wu, layer), _layer_block(cw, layer), _layer_block(wd, layer)],
        out_specs=pl.BlockSpec((1, tm, d), lambda bi, i: (bi, i, 0)),
        scratch_shapes=[pltpu.VMEM((tm + 2 * HALO, d), BF16),
                        pltpu.VMEM((3, tm + 2 * HALO, 2 * FF_CHUNK), F32),
                        pltpu.VMEM((tm, D_FF), BF16)],
        compiler_params=_params(2),
        name="ffn",
    )(x, x, x, mod, g_pre.reshape(1, d), g_post.reshape(1, d), wu, cw, wd)


def _na_kernel(q_ref, k_ref, v_ref, kc_ref, vc_ref, bias_ref, o_ref, *, rows):
    kc = kc_ref[0]
    vc = vc_ref[0]
    win = NA_WIN_H * GRID_W
    mask_a = _head_mask(GRID_W, 0)
    vc_ext = jnp.concatenate([vc, jnp.ones_like(vc)], axis=1)
    ones_w = jnp.ones((win, LANES), BF16)

    def group(gi, carry):
        starts, scores, probs = [], [], []
        for i in range(NA_GROUP):
            r = gi * NA_GROUP + i
            r0 = jnp.clip(r - NA_WIN_H // 2, 0, rows - NA_WIN_H)
            q0 = pl.multiple_of(r * GRID_W, GRID_W)
            k0 = pl.multiple_of(r0 * GRID_W, GRID_W)
            q = q_ref[0, pl.ds(q0, GRID_W), :]
            zero = jnp.zeros_like(q)
            q2 = jnp.concatenate([jnp.where(mask_a, q, zero), jnp.where(mask_a, zero, q)], axis=0)
            d0 = r0 - r + NA_WIN_H - 1
            bias = jnp.concatenate(
                [jnp.concatenate([bias_ref[0, hh, d0 + 2 * a] for a in range(NA_WIN_H // 2)], axis=1)
                 for hh in range(2)], axis=0)
            s = _dot_nt(q2, k_ref[0, pl.ds(k0, win), :]) + bias
            sc = _dot_nt(q2, kc)
            starts.append((q0, k0))
            scores.append((s, sc))
        for s, sc in scores:
            m = jnp.maximum(jnp.max(s, axis=-1, keepdims=True), jnp.max(sc, axis=-1, keepdims=True))
            probs.append((jnp.exp2(s - m).astype(BF16), jnp.exp2(sc - m).astype(BF16)))
        for (q0, k0), (p, pc) in zip(starts, probs):
            v_ext = jnp.concatenate([v_ref[0, pl.ds(k0, win), :], ones_w], axis=1)
            o_ext = _dot(p, v_ext) + _dot(pc, vc_ext)
            o2 = o_ext[:, :LANES] / o_ext[:, LANES:]
            o_ref[0, pl.ds(q0, GRID_W), :] = jnp.where(mask_a, o2[:GRID_W], o2[GRID_W:]).astype(BF16)
        return carry

    lax.fori_loop(0, rows // NA_GROUP, group, 0)


def _na(pl_, pc_, bias):
    b, s, _ = pl_.shape
    n_ctx = pc_.shape[1]
    n_pairs = NA_HEADS // 2
    return pl.pallas_call(
        functools.partial(_na_kernel, rows=s // GRID_W),
        out_shape=jax.ShapeDtypeStruct((b, s, NA_WIDTH), BF16),
        grid=(b, n_pairs),
        in_specs=[pl.BlockSpec((1, s, LANES), lambda bi, p: (bi, 0, AB_QA + p)),
                  pl.BlockSpec((1, s, LANES), lambda bi, p: (bi, 0, AB_KA + p)),
                  pl.BlockSpec((1, s, LANES), lambda bi, p: (bi, 0, AB_VA + p)),
                  pl.BlockSpec((1, n_ctx, LANES), lambda bi, p: (bi, 0, AB_KA + p)),
                  pl.BlockSpec((1, n_ctx, LANES), lambda bi, p: (bi, 0, AB_VA + p)),
                  pl.BlockSpec((1,) + bias.shape[1:], lambda bi, p: (p, 0, 0, 0, 0))],
        out_specs=pl.BlockSpec((1, s, LANES), lambda bi, p: (bi, 0, p)),
        compiler_params=_params(2),
        name="na",
    )(pl_, pl_, pl_, pc_, pc_, bias)


def _na_bias(rpb):
    h = rpb.shape[0]
    cidx = np.arange(GRID_W)
    c_start = np.clip(cidx - NA_WIN_W // 2, 0, GRID_W - NA_WIN_W)
    col_ok = (cidx[None, :] >= c_start[:, None]) & (cidx[None, :] < c_start[:, None] + NA_WIN_W)
    lo, hi = GRID_W - NA_WIN_W, GRID_W - NA_WIN_W
    rp = rpb.astype(F32) * LOG2E
    ext = jnp.concatenate([jnp.repeat(rp[..., :1], lo, axis=-1), rp, jnp.repeat(rp[..., -1:], hi, axis=-1)], axis=-1)
    bm = jnp.stack([ext[..., GRID_W - 1 - q:2 * GRID_W - 1 - q] for q in range(GRID_W)], axis=2)
    bm = jnp.where(col_ok[None, None], bm, NEG_INF)
    pairs = jnp.concatenate([bm[:, :-1], bm[:, 1:]], axis=-1)
    return pairs.reshape((h // 2, 2) + pairs.shape[1:])


def _ctx_attn_kernel(*refs, has_sink):
    if has_sink:
        q_ref, k_ref, v_ref, sink_ref, o_ref = refs
    else:
        q_ref, k_ref, v_ref, o_ref = refs
    q, k, v = q_ref[0], k_ref[0], v_ref[0]
    n = q.shape[0]
    masks = [_head_mask(n, 0), _head_mask(n, 1)]
    outs = []
    for h in range(2):
        qm = jnp.where(masks[h], q, jnp.zeros_like(q))
        s = _dot_nt(qm, k)
        m = jnp.max(s, axis=-1, keepdims=True)
        if has_sink:
            sk = sink_ref[0, h:h + 1, 0:1]
            m = jnp.maximum(m, sk)
        p = jnp.exp2(s - m)
        den = jnp.sum(p, axis=-1, keepdims=True)
        if has_sink:
            den = den + jnp.exp2(sk - m)
        outs.append(_dot(p.astype(BF16), v) / den)
    o_ref[0] = jnp.where(masks[0], outs[0], outs[1]).astype(BF16)


def _ctx_attn(pc_, q_off, k_off, v_off, n_q_tiles, group, sink=None):
    b, n, _ = pc_.shape
    in_specs = [pl.BlockSpec((1, n, LANES), lambda bi, t: (bi, 0, q_off + t)),
                pl.BlockSpec((1, n, LANES), lambda bi, t: (bi, 0, k_off + t // group)),
                pl.BlockSpec((1, n, LANES), lambda bi, t: (bi, 0, v_off + t // group))]
    args = [pc_, pc_, pc_]
    if sink is not None:
        in_specs.append(pl.BlockSpec((1, 2, LANES), lambda bi, t: (t, 0, 0)))
        args.append(sink)
    return pl.pallas_call(
        functools.partial(_ctx_attn_kernel, has_sink=sink is not None),
        out_shape=jax.ShapeDtypeStruct((b, n, n_q_tiles * LANES), BF16),
        grid=(b, n_q_tiles),
        in_specs=in_specs,
        out_specs=pl.BlockSpec((1, n, LANES), lambda bi, t: (bi, 0, t)),
        compiler_params=_params(2),
        name="ctx_attn",
    )(*args)


def _gla_kernel(*refs, n_lat_blocks, need_ctx):
    (ql, kl, vl, zl, rl, qc, kc, vc, zc, rc, w2f, w2b, baf, bab, g_ref) = refs[:15]
    n_out = 2 if need_ctx else 1
    yl_ref = refs[15]
    yc_ref = refs[16] if need_ctx else None
    qt_s, kd_s, dec_s, vt_s, o_s = refs[15 + n_out:]
    blk, ck = GLA_BLOCK, GLA_CHUNK
    cpb = blk // ck
    shift = ck.bit_length() - 1
    ri = lax.broadcasted_iota(jnp.int32, (blk, blk), 0)
    ci = lax.broadcasted_iota(jnp.int32, (blk, blk), 1)
    same = lax.shift_right_logical(ri, shift) == lax.shift_right_logical(ci, shift)
    att_mask = [same & (ci <= ri), same & (ci >= ri)]
    as_bf16 = lambda m: jnp.where(m, 1.0, 0.0).astype(BF16)
    cum = [jnp.concatenate([as_bf16(att_mask[d]), as_bf16(same)], axis=0) for d in range(2)]
    head = [_head_mask(blk, 0), _head_mask(blk, 1)]
    head_ck = [_head_mask(ck, 0), _head_mask(ck, 1)]
    tok_lo = lax.broadcasted_iota(jnp.int32, (2 * GLA_DV, LANES), 1) < ck
    w2 = jnp.concatenate([w2f[...], w2b[...]], axis=1)
    ba = jnp.concatenate([baf[...], bab[...]], axis=1)
    g = g_ref[...]
    lat4, ctx4 = (ql, kl, vl, zl), (qc, kc, vc, zc)

    def phase1(blocks):
        gates = []
        for (_, _, _, z_ref), t_in, _ in blocks:
            x = _dot(z_ref[0, pl.ds(t_in, blk), :], w2) + ba
            la = (jnp.minimum(x, 0.0) - jnp.log(1.0 + jnp.exp(-jnp.abs(x)))) * (1.0 / GLA_TAU)
            hi = la.astype(BF16)
            gates.append((hi, (la - hi.astype(F32)).astype(BF16)))
        cums = []
        for hi, lo in gates:
            per_dir = []
            for d in range(2):
                c2 = _dot(cum[d], jnp.concatenate([hi[:, d * LANES:(d + 1) * LANES],
                                                   lo[:, d * LANES:(d + 1) * LANES]], axis=1))
                per_dir.append((c2[:blk, :LANES] + c2[:blk, LANES:], c2[blk:, :LANES] + c2[blk:, LANES:]))
            cums.append(per_dir)
        scores = []
        for ((q_ref, k_ref, _, _), t_in, bo), per_dir in zip(blocks, cums):
            q = q_ref[0, pl.ds(t_in, blk), :].astype(F32)
            k = k_ref[0, pl.ds(t_in, blk), :].astype(F32)
            t_out = pl.multiple_of(bo * blk, blk)
            items = []
            for d, (b, tot) in enumerate(per_dir):
                qt = q * jnp.exp(b)
                kt = (k * jnp.exp(-b)).astype(BF16)
                qt_s[d, pl.ds(t_out, blk), :] = qt.astype(BF16)
                kd_s[d, pl.ds(t_out, blk), :] = (k * jnp.exp(tot - b)).astype(BF16)
                for j in range(cpb):
                    dec_s[d, pl.ds(bo * cpb + j, 1), :] = jnp.exp(tot[j * ck:j * ck + 1])
                for h in range(2):
                    items.append((d, h, _dot_nt(jnp.where(head[h], qt, 0.0).astype(BF16), kt)))
            scores.append(items)
        for ((_, _, v_ref, _), t_in, bo), items in zip(blocks, scores):
            v = v_ref[0, pl.ds(t_in, blk), :]
            oi = [None, None]
            for d, h, s in items:
                c = _dot(jnp.where(att_mask[d], s, 0.0).astype(BF16), v[:, h * GLA_DV:(h + 1) * GLA_DV])
                oi[h] = c if oi[h] is None else oi[h] + c
            o_s[pl.ds(pl.multiple_of(bo * blk, blk), blk), :] = jnp.concatenate(oi, axis=1)
            vt_s[bo] = v.astype(F32).T.astype(BF16)

    orders = [list(range(cpb)), list(range(cpb - 1, -1, -1))]

    def scan_pair(bos, sts):
        upd = [{}, {}]
        for d in range(2):
            vt = vt_s[bos[d]]
            kd = kd_s[d, pl.ds(pl.multiple_of(bos[d] * blk, blk), blk), :]
            for c in orders[d]:
                half = c // 2
                vt_c = vt[:, half * LANES:(half + 1) * LANES]
                vt_c = jnp.where(tok_lo if c % 2 == 0 else ~tok_lo, vt_c, jnp.zeros_like(vt_c))
                upd[d][c] = _dot(vt_c, kd[half * LANES:(half + 1) * LANES])
        sts = list(sts)
        for step in range(cpb):
            for d in range(2):
                c = orders[d][step]
                tc = pl.multiple_of(bos[d] * blk + c * ck, ck)
                qt = qt_s[d, pl.ds(tc, ck), :]
                stb = sts[d].astype(BF16)
                outs = [_dot_nt(jnp.where(head_ck[h], qt, jnp.zeros_like(qt)), stb[h * GLA_DV:(h + 1) * GLA_DV])
                        for h in range(2)]
                o_s[pl.ds(tc, ck), :] += jnp.concatenate(outs, axis=1)
                sts[d] = dec_s[d, pl.ds(bos[d] * cpb + c, 1), :] * sts[d] + upd[d][c]
        return tuple(sts)

    def finalize(bo, r_ref, t_in, y_ref):
        o = o_s[pl.ds(pl.multiple_of(bo * blk, blk), blk), :]
        r = r_ref[0, pl.ds(t_in, blk), :].astype(F32)
        on = jnp.concatenate(
            [o[:, h * GLA_DV:(h + 1) * GLA_DV]
             * lax.rsqrt(jnp.mean(o[:, h * GLA_DV:(h + 1) * GLA_DV] ** 2, axis=-1, keepdims=True) + EPS)
             for h in range(2)], axis=1)
        y_ref[0, pl.ds(t_in, blk), :] = (on * g * _silu(r)).astype(BF16)

    phase1([(ctx4, 0, 0)])

    def phase1_lat(i, carry):
        phase1([(lat4, pl.multiple_of((2 * i + j) * blk, blk), 1 + 2 * i + j) for j in range(2)])
        return carry

    lax.fori_loop(0, n_lat_blocks // 2, phase1_lat, 0)

    zero = jnp.zeros((2 * GLA_DV, 2 * GLA_DK), F32)
    sts = scan_pair((0, 0), (zero, zero))
    lax.fori_loop(0, n_lat_blocks, lambda i, s: scan_pair((1 + i, n_lat_blocks - i), s), sts)

    if need_ctx:
        finalize(0, rc, 0, yc_ref)

    def finalize_lat(i, carry):
        for j in range(2):
            finalize(1 + 2 * i + j, rl, pl.multiple_of((2 * i + j) * blk, blk), yl_ref)
        return carry

    lax.fori_loop(0, n_lat_blocks // 2, finalize_lat, 0)


def _gla(pl_, pc_, w2, ba, g, need_ctx):
    b, s, _ = pl_.shape
    n_c = pc_.shape[1]
    n_pairs = GLA_HEADS // 2
    w = 2 * GLA_DV

    def stream(n):
        return [pl.BlockSpec((1, n, LANES), lambda bi, p: (bi, 0, AB_QB + p)),
                pl.BlockSpec((1, n, LANES), lambda bi, p: (bi, 0, AB_KB + p)),
                pl.BlockSpec((1, n, w), lambda bi, p: (bi, 0, AB_VB + p)),
                pl.BlockSpec((1, n, LANES), lambda bi, p: (bi, 0, AB_Z)),
                pl.BlockSpec((1, n, w), lambda bi, p: (bi, 0, AB_RB + p))]

    in_specs = stream(s) + stream(n_c) + [
        pl.BlockSpec((LANES, LANES), lambda bi, p: (0, p)),
        pl.BlockSpec((LANES, LANES), lambda bi, p: (0, n_pairs + p)),
        pl.BlockSpec((1, LANES), lambda bi, p: (0, p)),
        pl.BlockSpec((1, LANES), lambda bi, p: (0, n_pairs + p)),
        pl.BlockSpec((1, w), lambda bi, p: (0, p))]
    out_shape = [jax.ShapeDtypeStruct((b, s, GLA_VW), BF16)]
    out_specs = [pl.BlockSpec((1, s, w), lambda bi, p: (bi, 0, p))]
    if need_ctx:
        out_shape.append(jax.ShapeDtypeStruct((b, n_c, GLA_VW), BF16))
        out_specs.append(pl.BlockSpec((1, n_c, w), lambda bi, p: (bi, 0, p)))
    assert n_c == GLA_BLOCK and s % (2 * GLA_BLOCK) == 0
    n_lat_blocks = s // GLA_BLOCK
    t_all = n_c + s
    n_chunks = -(-(t_all // GLA_CHUNK) // 8) * 8
    scratch = [pltpu.VMEM((2, t_all, LANES), BF16),
               pltpu.VMEM((2, t_all, LANES), BF16),
               pltpu.VMEM((2, n_chunks, LANES), F32),
               pltpu.VMEM((1 + n_lat_blocks, w, GLA_BLOCK), BF16),
               pltpu.VMEM((t_all, w), F32)]
    res = pl.pallas_call(
        functools.partial(_gla_kernel, n_lat_blocks=n_lat_blocks, need_ctx=need_ctx),
        out_shape=out_shape,
        grid=(b, n_pairs),
        in_specs=in_specs,
        out_specs=out_specs,
        scratch_shapes=scratch,
        compiler_params=_params(2),
        name="gla",
    )(*([pl_] * 5), *([pc_] * 5), w2, w2, ba, ba, g.reshape(1, GLA_VW))
    return (res[0], res[1]) if need_ctx else (res[0], None)


def _gla_gate_weights(wa2, ba):
    w = jnp.zeros((LANES, 2 * GLA_KW), F32)
    w = w.at[0:GLA_RANK, 0:GLA_KW].set(wa2[0])
    w = w.at[GLA_RANK:2 * GLA_RANK, GLA_KW:].set(wa2[1])
    return w.astype(BF16), ba.reshape(1, 2 * GLA_KW).astype(F32)


def _swa_kernel(q_ref, kp_ref, kn_ref, kx_ref, vp_ref, vn_ref, vx_ref, kc_ref, vc_ref, sink_ref, o_ref):
    n = pl.program_id(1)
    blk = SWA_BLOCK
    n_c = kc_ref.shape[1]
    n_keys = 3 * blk + n_c
    k_lat = jnp.concatenate([kp_ref[0], kn_ref[0], kx_ref[0]], axis=0)
    v_lat = jnp.concatenate([vp_ref[0], vn_ref[0], vx_ref[0]], axis=0)
    qi = lax.broadcasted_iota(jnp.int32, (blk, n_keys), 0)
    kj = lax.broadcasted_iota(jnp.int32, (blk, n_keys), 1)
    in_prev = (kj < blk) & (kj >= qi)
    in_next = (kj >= 2 * blk) & (kj < 3 * blk) & (kj - 2 * blk <= qi)
    always = (kj >= 3 * blk) | ((kj >= blk) & (kj < 2 * blk))
    first, last = n == 0, n == pl.num_programs(1) - 1
    ones = jnp.ones((n_keys, LANES), BF16)
    mask_a = _head_mask(blk, 0)
    half = SWA_GROUP // 2
    n_pairs = SWA_KV // 2
    n_chunks = SWA_QB * n_pairs * 2

    def block_inputs(j):
        k = jnp.concatenate([k_lat[j * blk:(j + 3) * blk], kc_ref[0]], axis=0)
        v = jnp.concatenate([v_lat[j * blk:(j + 3) * blk], vc_ref[0]], axis=0)
        has_prev = jnp.logical_not(first) if j == 0 else True
        has_next = jnp.logical_not(last) if j == SWA_QB - 1 else True
        return k, v, always | (in_prev & has_prev) | (in_next & has_next)

    blocks = [block_inputs(j) for j in range(SWA_QB)]

    def scores(c):
        j, pair = c // (2 * n_pairs), (c // 2) % n_pairs
        k, _, ok = blocks[j]
        tiles = [pair * SWA_GROUP + (c % 2) * half + g for g in range(half)]
        q_parts, sink_parts = [], []
        for t in tiles:
            q = q_ref[0, j * blk:(j + 1) * blk, t * LANES:(t + 1) * LANES]
            zero = jnp.zeros_like(q)
            q_parts += [jnp.where(mask_a, q, zero), jnp.where(mask_a, zero, q)]
            sink_parts += [jnp.broadcast_to(sink_ref[t, h:h + 1, 0:1], (blk, 1)) for h in range(2)]
        s = _dot_nt(jnp.concatenate(q_parts, axis=0), k[:, pair * LANES:(pair + 1) * LANES])
        s = jnp.concatenate([jnp.where(ok, s[i * blk:(i + 1) * blk], NEG_INF) for i in range(2 * half)], axis=0)
        return j, tiles, pair, s, jnp.concatenate(sink_parts, axis=0)

    def finish(j, tiles, pair, s, sk):
        m = jnp.maximum(jnp.max(s, axis=-1, keepdims=True), sk)
        p = jnp.exp2(s - m).astype(BF16)
        o_ext = _dot(p, jnp.concatenate([blocks[j][1][:, pair * LANES:(pair + 1) * LANES], ones], axis=1))
        o = o_ext[:, :LANES] / (o_ext[:, LANES:] + jnp.exp2(sk - m))
        for i, t in enumerate(tiles):
            o_ref[0, j * blk:(j + 1) * blk, t * LANES:(t + 1) * LANES] = jnp.where(
                mask_a, o[2 * i * blk:(2 * i + 1) * blk], o[(2 * i + 1) * blk:(2 * i + 2) * blk]).astype(BF16)

    pending = scores(0)
    for c in range(n_chunks):
        nxt = scores(c + 1) if c + 1 < n_chunks else None
        finish(*pending)
        pending = nxt


def _swa(pl_, pc_, sink):
    b, s, _ = pl_.shape
    n_c = pc_.shape[1]
    nb = s // SWA_BLOCK
    step = SWA_QB * SWA_BLOCK
    assert s % step == 0
    qw = SWA_HEADS * HEAD_DIM
    kw = SWA_KV * HEAD_DIM
    k_blk, v_blk = (C_K * LANES) // kw, (C_V * LANES) // kw

    def kv(off):
        return [pl.BlockSpec((1, SWA_BLOCK, kw), lambda bi, n: (bi, jnp.maximum(n * SWA_QB - 1, 0), off)),
                pl.BlockSpec((1, step, kw), lambda bi, n: (bi, n, off)),
                pl.BlockSpec((1, SWA_BLOCK, kw), lambda bi, n: (bi, jnp.minimum((n + 1) * SWA_QB, nb - 1), off))]

    in_specs = ([pl.BlockSpec((1, step, qw), lambda bi, n: (bi, n, 0))] + kv(k_blk) + kv(v_blk)
                + [pl.BlockSpec((1, n_c, kw), lambda bi, n: (bi, 0, k_blk)),
                   pl.BlockSpec((1, n_c, kw), lambda bi, n: (bi, 0, v_blk)),
                   _resident(sink.shape)])
    return pl.pallas_call(
        _swa_kernel,
        out_shape=jax.ShapeDtypeStruct((b, s, qw), BF16),
        grid=(b, s // step),
        in_specs=in_specs,
        out_specs=pl.BlockSpec((1, step, qw), lambda bi, n: (bi, n, 0)),
        compiler_params=_params(2),
        name="swa",
    )(pl_, pl_, pl_, pl_, pl_, pl_, pl_, pc_, pc_, sink)


def _swa_pair_heads(a, axis):
    shape = a.shape
    a = a.reshape(shape[:axis] + (SWA_KV // 2, 2, SWA_GROUP, -1) + shape[axis + 1:])
    return jnp.swapaxes(a, axis + 1, axis + 2).reshape(shape)


def _rope_tables(n):
    t = jnp.arange(n)
    row = (t // GRID_W).astype(F32)
    col = (t % GRID_W).astype(F32)
    half = HEAD_DIM // 2
    inv = 1.0 / (ROPE_BASE ** (jnp.arange(0, half, 2, dtype=F32) / half))
    ang = jnp.concatenate([row[:, None] * inv, col[:, None] * inv], axis=-1)
    cos = jnp.repeat(jnp.cos(ang), 2, axis=1)
    sin = jnp.repeat(jnp.sin(ang), 2, axis=1) * jnp.tile(jnp.array([-1.0, 1.0], F32), half)
    reps = ROPE_TILE // HEAD_DIM
    return jnp.tile(cos, (1, reps)), jnp.tile(sin, (1, reps))


def kernel(x, c, ctx, c_ctx, w_mod, b_mod, g_mix_pre, g_mix_post, g_ffn_pre, g_ffn_post, w_out, w_up, conv_w,
           conv_b, w_down, w_in_ab, na_rpb, gla_wa2, gla_ba, gla_g, w_in_c, swa_sink):
    bsz, seq, d = x.shape
    n_ctx = ctx.shape[1]
    depth = w_mod.shape[0]
    tm_l = 512 if seq % 512 == 0 else 256
    tm_c = n_ctx
    tm_o = 1024 if seq % 1024 == 0 else tm_l

    c_all = jnp.zeros((8, d), F32).at[:bsz].set(c).at[bsz].set(c_ctx)
    mods = _modulation(c_all, w_mod, b_mod).reshape(depth, 8, 6, d)
    rope = _rope_tables(seq)

    col_scale_ab = np.ones((IN_AB,), np.float32)
    col_scale_ab[0:NA_WIDTH] = ATTN_SCALE
    col_scale_ab[3 * NA_WIDTH:3 * NA_WIDTH + GLA_KW] = GLA_DK ** -0.5

    n_q = SWA_HEADS * HEAD_DIM
    w_ab = jnp.pad(w_in_ab * col_scale_ab, ((0, 0), (0, 0), (0, IN_AB_PAD - IN_AB))).astype(BF16)
    w_c = jnp.concatenate([_swa_pair_heads(w_in_c[:, :, :n_q], 2) * ATTN_SCALE, w_in_c[:, :, n_q:]],
                          axis=2).astype(BF16)
    w_o = w_out.astype(BF16)
    w_o_swa = _swa_pair_heads(w_out[1::2], 1).astype(BF16)
    wu = w_up.astype(BF16)
    wd = w_down.astype(BF16)
    cwb = jnp.concatenate([conv_w, conv_b[:, None]], axis=1)
    cw = jnp.concatenate([cwb[:, :, :D_FF].reshape(depth, 4, N_FF_CHUNKS, FF_CHUNK),
                          cwb[:, :, D_FF:].reshape(depth, 4, N_FF_CHUNKS, FF_CHUNK)], axis=3).transpose(0, 2, 1, 3)

    xl, xc = x, ctx
    for i in range(depth):
        need_ctx = i < depth - 1
        j = i // 2
        mod_l = mods[i, :bsz]
        mod_c = mods[i, bsz:bsz + 1]
        if i % 2 == 0:
            p_l = _proj(xl, mod_l, g_mix_pre[i], w_ab, j, tm_l, 640)
            p_c = _proj(xc, mod_c, g_mix_pre[i], w_ab, j, tm_c, 640)
            w2, ba = _gla_gate_weights(gla_wa2[j], gla_ba[j])
            y_na = _na(p_l, p_c, _na_bias(na_rpb[j]))
            y_gla, yc_gla = _gla(p_l, p_c, w2, ba, gla_g[j], need_ctx)
            ys_l = [y_na, y_gla]
            w_post, l_post = w_o, i
            if need_ctx:
                ys_c = [_ctx_attn(p_c, AB_QA, AB_KA, AB_VA, NA_HEADS // 2, 1), yc_gla]
        else:
            p_l = _proj(xl, mod_l, g_mix_pre[i], w_c, j, tm_l, ROPE_TILE, rope, (C_V * LANES) // ROPE_TILE)
            p_c = _proj(xc, mod_c, g_mix_pre[i], w_c, j, tm_c, ROPE_TILE)
            sink = jnp.broadcast_to(_swa_pair_heads(swa_sink[j] * LOG2E, 0).reshape(-1, 2, 1),
                                    (SWA_HEADS // 2, 2, LANES)).astype(F32)
            ys_l = [_swa(p_l, p_c, sink)]
            w_post, l_post = w_o_swa, j
            if need_ctx:
                ys_c = [_ctx_attn(p_c, C_Q, C_K, C_V, SWA_HEADS // 2, SWA_GROUP, sink)]

        xl = _outproj(ys_l, w_post, l_post, xl, mod_l, g_mix_post[i], tm_o)
        xl = _ffn(xl, mod_l, g_ffn_pre[i], g_ffn_post[i], wu, cw, wd, i, tm_l)
        if need_ctx:
            xc = _outproj(ys_c, w_post, l_post, xc, mod_c, g_mix_post[i], tm_c)
            xc = _ffn(xc, mod_c, g_ffn_pre[i], g_ffn_post[i], wu, cw, wd, i, tm_c)
    return xl
```

```python
import functools

import numpy as np
import jax
import jax.numpy as jnp
from jax import lax
from jax.experimental import pallas as pl
from jax.experimental.pallas import tpu as pltpu

F32 = jnp.float32
BF16 = jnp.bfloat16

LANES = 128
HEAD_DIM = 64
GRID_W = 64
D_MODEL = 1024
EPS = 1e-6
NEG_INF = -1e30
LOG2E = float(np.log2(np.e))
ATTN_SCALE = HEAD_DIM ** -0.5 * LOG2E

NA_HEADS = 8
NA_WIN_H = 8
NA_WIN_W = 16
NA_WIDTH = NA_HEADS * HEAD_DIM
NA_GROUP = 16
GLA_HEADS = 4
GLA_DK = 64
GLA_DV = 128
GLA_KW = GLA_HEADS * GLA_DK
GLA_VW = GLA_HEADS * GLA_DV
GLA_RANK = 16
GLA_TAU = 16.0
GLA_CHUNK = 64
GLA_BLOCK = 256
IN_AB = 3 * NA_WIDTH + 2 * GLA_KW + 2 * GLA_VW + 2 * GLA_RANK
IN_AB_PAD = 3200
SWA_HEADS = 16
SWA_KV = 4
SWA_GROUP = SWA_HEADS // SWA_KV
SWA_BLOCK = 128
SWA_QB = 2
IN_C = (SWA_HEADS + 2 * SWA_KV) * HEAD_DIM
D_FF = 2816
FF_CHUNK = 256
N_FF_CHUNKS = D_FF // FF_CHUNK
assert N_FF_CHUNKS * FF_CHUNK == D_FF
HALO = 16
OUTPROJ_SUB = 256
PROJ_SUB = 256
ROPE_BASE = 10000.0
ROPE_TILE = 256
VMEM_LIMIT = 56 * 1024 * 1024

AB_QA, AB_KA, AB_VA = 0, 4, 8
AB_QB, AB_KB = 12, 14
AB_VB, AB_RB = 8, 10
AB_Z = 24
C_Q, C_K, C_V = 0, 8, 10


def _dot(a, b):
    return jnp.dot(a, b, preferred_element_type=F32)


def _dot_nt(a, b):
    return lax.dot_general(a, b, (((1,), (1,)), ((), ())), preferred_element_type=F32)


def _rms(x, g):
    return x * lax.rsqrt(jnp.mean(x * x, axis=-1, keepdims=True) + EPS) * g


def _silu(x):
    return x / (1.0 + jnp.exp(-x))


def _params(n_grid):
    return pltpu.CompilerParams(dimension_semantics=("arbitrary",) * n_grid,
                                vmem_limit_bytes=VMEM_LIMIT)


def _resident(shape):
    nd = len(shape)
    return pl.BlockSpec(shape, lambda *_: (0,) * nd, pipeline_mode=pl.Buffered(1))


def _layer_block(arr, layer, rows=None, row_block=0):
    shape = list(arr.shape[1:])
    if rows is not None:
        shape[0] = rows
    tail = (0,) * (len(shape) - 1)
    return pl.BlockSpec((None, *shape), lambda *_: (layer, row_block) + tail, pipeline_mode=pl.Buffered(1))


def _head_mask(rows, h):
    lane = lax.broadcasted_iota(jnp.int32, (rows, LANES), 1)
    return (lane < HEAD_DIM) if h == 0 else (lane >= HEAD_DIM)


def _mod_kernel(c_ref, w_ref, b_ref, o_ref):
    s = _silu(c_ref[...]).astype(BF16)
    o_ref[0] = _dot(s, w_ref[0].astype(BF16)) + b_ref[0]


def _modulation(c_all, w_mod, b_mod):
    depth, d, n = w_mod.shape
    tn = 1536
    return pl.pallas_call(
        _mod_kernel,
        out_shape=jax.ShapeDtypeStruct((depth, 8, n), F32),
        grid=(depth, n // tn),
        in_specs=[pl.BlockSpec((8, d), lambda i, j: (0, 0)),
                  pl.BlockSpec((1, d, tn), lambda i, j: (i, 0, j)),
                  pl.BlockSpec((1, 1, tn), lambda i, j: (i, 0, j))],
        out_specs=pl.BlockSpec((1, 8, tn), lambda i, j: (i, 0, j)),
        compiler_params=_params(2),
        name="modulation",
    )(c_all, w_mod, b_mod.reshape(depth, 1, n))


def _rope(acc, cos, sin):
    n = acc.shape[1]
    lane = lax.broadcasted_iota(jnp.int32, acc.shape, 1)
    partner = jnp.where(lane % 2 == 0, pltpu.roll(acc, n - 1, 1), pltpu.roll(acc, 1, 1))
    return acc * cos + partner * sin


def _proj_kernel(*refs, chunk, rope_chunks, shift_idx):
    if rope_chunks:
        x_ref, mod_ref, g_ref, w_ref, cos_ref, sin_ref, o_ref = refs
    else:
        x_ref, mod_ref, g_ref, w_ref, o_ref = refs
    mod = mod_ref[0]
    tm = x_ref.shape[1]
    sub = min(tm, PROJ_SUB)
    for r0 in range(0, tm, sub):
        h = (_rms(x_ref[0, r0:r0 + sub], g_ref[...]) * (1.0 + mod[shift_idx + 1:shift_idx + 2])
             + mod[shift_idx:shift_idx + 1])
        hb = h.astype(BF16)
        for c in range(w_ref.shape[1] // chunk):
            acc = _dot(hb, w_ref[:, c * chunk:(c + 1) * chunk])
            if c < rope_chunks:
                acc = _rope(acc, cos_ref[r0:r0 + sub], sin_ref[r0:r0 + sub])
            o_ref[0, r0:r0 + sub, c * chunk:(c + 1) * chunk] = acc.astype(BF16)


def _proj(x, mod, g, w, layer, tm, chunk, rope=None, rope_chunks=0):
    b, s, d = x.shape
    n = w.shape[2]
    per_batch = mod.shape[0] > 1
    in_specs = [pl.BlockSpec((1, tm, d), lambda bi, i: (bi, i, 0)),
                pl.BlockSpec((1, 6, d), (lambda bi, i: (bi, 0, 0)) if per_batch else (lambda bi, i: (0, 0, 0))),
                _resident((1, d)),
                _layer_block(w, layer)]
    args = [x, mod, g.reshape(1, d), w]
    if rope_chunks:
        in_specs += [pl.BlockSpec((tm, chunk), lambda bi, i: (i, 0))] * 2
        args += list(rope)
    return pl.pallas_call(
        functools.partial(_proj_kernel, chunk=chunk, rope_chunks=rope_chunks, shift_idx=0),
        out_shape=jax.ShapeDtypeStruct((b, s, n), BF16),
        grid=(b, s // tm),
        in_specs=in_specs,
        out_specs=pl.BlockSpec((1, tm, n), lambda bi, i: (bi, i, 0)),
        compiler_params=_params(2),
        name="proj",
    )(*args)


def _outproj_kernel(*refs, n_parts):
    y_refs, w_refs = refs[:n_parts], refs[n_parts:2 * n_parts]
    x_ref, mod_ref, g_ref, o_ref = refs[2 * n_parts:]
    tm = x_ref.shape[1]
    sub = min(tm, OUTPROJ_SUB)
    gate, g = mod_ref[0][2:3], g_ref[...]
    accs = []
    for r0 in range(0, tm, sub):
        acc = _dot(y_refs[0][0, r0:r0 + sub], w_refs[0][...])
        for yr, wr in zip(y_refs[1:], w_refs[1:]):
            acc = acc + _dot(yr[0, r0:r0 + sub], wr[...])
        accs.append(acc)
    for k, acc in enumerate(accs):
        o_ref[0, k * sub:(k + 1) * sub] = x_ref[0, k * sub:(k + 1) * sub] + gate * _rms(acc, g)


def _outproj(ys, w, layer, x, mod, g, tm):
    b, s, d = x.shape
    per_batch = mod.shape[0] > 1
    in_specs = [pl.BlockSpec((1, tm, y.shape[2]), lambda bi, i: (bi, i, 0)) for y in ys]
    in_specs += [_layer_block(w, layer, y.shape[2], k) for k, y in enumerate(ys)]
    in_specs += [pl.BlockSpec((1, tm, d), lambda bi, i: (bi, i, 0)),
                 pl.BlockSpec((1, 6, d), (lambda bi, i: (bi, 0, 0)) if per_batch else (lambda bi, i: (0, 0, 0))),
                 _resident((1, d))]
    return pl.pallas_call(
        functools.partial(_outproj_kernel, n_parts=len(ys)),
        out_shape=jax.ShapeDtypeStruct((b, s, d), F32),
        grid=(b, s // tm),
        in_specs=in_specs,
        out_specs=pl.BlockSpec((1, tm, d), lambda bi, i: (bi, i, 0)),
        compiler_params=_params(2),
        name="outproj",
    )(*ys, *([w] * len(ys)), x, mod, g.reshape(1, d))


def _ffn_kernel(x_ref, xp_ref, xn_ref, mod_ref, gpre_ref, gpost_ref, wu_ref, cw_ref, wd_ref, o_ref,
                h_ref, u_ref, a_ref, *, tm):
    i = pl.program_id(1)
    mod = mod_ref[0]
    gpre = gpre_ref[...]

    def norm_mod(x):
        return _rms(x, gpre) * (1.0 + mod[4:5]) + mod[3:4]

    keep_prev = (i > 0).astype(F32)
    keep_next = (i < pl.num_programs(1) - 1).astype(F32)
    rows = tm + 2 * HALO
    half = tm // 2
    mid = HALO + half
    n_u = u_ref.shape[0]

    def up(c, r0=0, r1=rows):
        h = h_ref[r0:r1]
        for part in range(2):
            col = part * D_FF + c * FF_CHUNK
            u_ref[c % n_u, r0:r1, part * FF_CHUNK:(part + 1) * FF_CHUNK] = _dot(h, wu_ref[:, col:col + FF_CHUNK])

    h_ref[0:HALO] = (norm_mod(xp_ref[0]) * keep_prev).astype(BF16)
    h_ref[HALO:mid] = norm_mod(x_ref[0, :half]).astype(BF16)
    up(0, 0, mid)
    h_ref[mid:HALO + tm] = norm_mod(x_ref[0, half:]).astype(BF16)
    h_ref[HALO + tm:] = (norm_mod(xn_ref[0]) * keep_next).astype(BF16)
    up(0, mid, rows)

    def act(c):
        u = u_ref[c % n_u]
        cw = cw_ref[c]
        y = (pltpu.roll(u, 1, 0)[HALO:HALO + tm] * cw[0:1]
             + u[HALO:HALO + tm] * cw[1:2]
             + pltpu.roll(u, rows - 1, 0)[HALO:HALO + tm] * cw[2:3]
             + cw[3:4])
        a_ref[:, c * FF_CHUNK:(c + 1) * FF_CHUNK] = (_silu(y[:, :FF_CHUNK]) * y[:, FF_CHUNK:]).astype(BF16)

    for c in range(1, min(n_u - 1, N_FF_CHUNKS)):
        up(c)
    for c in range(N_FF_CHUNKS):
        if c + n_u - 1 < N_FF_CHUNKS:
            up(c + n_u - 1)
        act(c)
    accs = [_dot(a_ref[r0:r0 + half], wd_ref[...]) for r0 in (0, half)]
    for r0, acc in zip((0, half), accs):
        o_ref[0, r0:r0 + half] = x_ref[0, r0:r0 + half] + mod[5:6] * _rms(acc, gpost_ref[...])


def _ffn(x, mod, g_pre, g_post, wu, cw, wd, layer, tm):
    b, s, d = x.shape
    per_batch = mod.shape[0] > 1
    hb = tm // HALO
    n_halo = s // HALO
    return pl.pallas_call(
        functools.partial(_ffn_kernel, tm=tm),
        out_shape=jax.ShapeDtypeStruct((b, s, d), F32),
        grid=(b, s // tm),
        in_specs=[pl.BlockSpec((1, tm, d), lambda bi, i: (bi, i, 0)),
                  pl.BlockSpec((1, HALO, d), lambda bi, i: (bi, jnp.maximum(i * hb - 1, 0), 0)),
                  pl.BlockSpec((1, HALO, d), lambda bi, i: (bi, jnp.minimum((i + 1) * hb, n_halo - 1), 0)),
                  pl.BlockSpec((1, 6, d), (lambda bi, i: (bi, 0, 0)) if per_batch else (lambda bi, i: (0, 0, 0))),
                  _resident((1, d)), _resident((1, d)),
                  _layer_block(wu, layer), _layer_block(cw, layer), _layer_block(wd, layer)],
        out_specs=pl.BlockSpec((1, tm, d), lambda bi, i: (bi, i, 0)),
        scratch_shapes=[pltpu.VMEM((tm + 2 * HALO, d), BF16),
                        pltpu.VMEM((3, tm + 2 * HALO, 2 * FF_CHUNK), F32),
                        pltpu.VMEM((tm, D_FF), BF16)],
        compiler_params=_params(2),
        name="ffn",
    )(x, x, x, mod, g_pre.reshape(1, d), g_post.reshape(1, d), wu, cw, wd)


def _na_kernel(q_ref, k_ref, v_ref, kc_ref, vc_ref, bias_ref, o_ref, *, rows):
    kc = kc_ref[0]
    vc = vc_ref[0]
    win = NA_WIN_H * GRID_W
    mask_a = _head_mask(GRID_W, 0)
    vc_ext = jnp.concatenate([vc, jnp.ones_like(vc)], axis=1)
    ones_w = jnp.ones((win, LANES), BF16)

    def group(gi, carry):
        starts, scores, probs = [], [], []
        for i in range(NA_GROUP):
            r = gi * NA_GROUP + i
            r0 = jnp.clip(r - NA_WIN_H // 2, 0, rows - NA_WIN_H)
            q0 = pl.multiple_of(r * GRID_W, GRID_W)
            k0 = pl.multiple_of(r0 * GRID_W, GRID_W)
            q = q_ref[0, pl.ds(q0, GRID_W), :]
            zero = jnp.zeros_like(q)
            q2 = jnp.concatenate([jnp.where(mask_a, q, zero), jnp.where(mask_a, zero, q)], axis=0)
            d0 = r0 - r + NA_WIN_H - 1
            bias = jnp.concatenate(
                [jnp.concatenate([bias_ref[0, hh, d0 + 2 * a] for a in range(NA_WIN_H // 2)], axis=1)
                 for hh in range(2)], axis=0)
            s = _dot_nt(q2, k_ref[0, pl.ds(k0, win), :]) + bias
            sc = _dot_nt(q2, kc)
            starts.append((q0, k0))
            scores.append((s, sc))
        for s, sc in scores:
            m = jnp.maximum(jnp.max(s, axis=-1, keepdims=True), jnp.max(sc, axis=-1, keepdims=True))
            probs.append((jnp.exp2(s - m).astype(BF16), jnp.exp2(sc - m).astype(BF16)))
        for (q0, k0), (p, pc) in zip(starts, probs):
            v_ext = jnp.concatenate([v_ref[0, pl.ds(k0, win), :], ones_w], axis=1)
            o_ext = _dot(p, v_ext) + _dot(pc, vc_ext)
            o2 = o_ext[:, :LANES] / o_ext[:, LANES:]
            o_ref[0, pl.ds(q0, GRID_W), :] = jnp.where(mask_a, o2[:GRID_W], o2[GRID_W:]).astype(BF16)
        return carry

    lax.fori_loop(0, rows // NA_GROUP, group, 0)


def _na(pl_, pc_, bias):
    b, s, _ = pl_.shape
    n_ctx = pc_.shape[1]
    n_pairs = NA_HEADS // 2
    return pl.pallas_call(
        functools.partial(_na_kernel, rows=s // GRID_W),
        out_shape=jax.ShapeDtypeStruct((b, s, NA_WIDTH), BF16),
        grid=(b, n_pairs),
        in_specs=[pl.BlockSpec((1, s, LANES), lambda bi, p: (bi, 0, AB_QA + p)),
                  pl.BlockSpec((1, s, LANES), lambda bi, p: (bi, 0, AB_KA + p)),
                  pl.BlockSpec((1, s, LANES), lambda bi, p: (bi, 0, AB_VA + p)),
                  pl.BlockSpec((1, n_ctx, LANES), lambda bi, p: (bi, 0, AB_KA + p)),
                  pl.BlockSpec((1, n_ctx, LANES), lambda bi, p: (bi, 0, AB_VA + p)),
                  pl.BlockSpec((1,) + bias.shape[1:], lambda bi, p: (p, 0, 0, 0, 0))],
        out_specs=pl.BlockSpec((1, s, LANES), lambda bi, p: (bi, 0, p)),
        compiler_params=_params(2),
        name="na",
    )(pl_, pl_, pl_, pc_, pc_, bias)


def _na_bias(rpb):
    h = rpb.shape[0]
    cidx = np.arange(GRID_W)
    c_start = np.clip(cidx - NA_WIN_W // 2, 0, GRID_W - NA_WIN_W)
    col_ok = (cidx[None, :] >= c_start[:, None]) & (cidx[None, :] < c_start[:, None] + NA_WIN_W)
    lo, hi = GRID_W - NA_WIN_W, GRID_W - NA_WIN_W
    rp = rpb.astype(F32) * LOG2E
    ext = jnp.concatenate([jnp.repeat(rp[..., :1], lo, axis=-1), rp, jnp.repeat(rp[..., -1:], hi, axis=-1)], axis=-1)
    bm = jnp.stack([ext[..., GRID_W - 1 - q:2 * GRID_W - 1 - q] for q in range(GRID_W)], axis=2)
    bm = jnp.where(col_ok[None, None], bm, NEG_INF)
    pairs = jnp.concatenate([bm[:, :-1], bm[:, 1:]], axis=-1)
    return pairs.reshape((h // 2, 2) + pairs.shape[1:])


def _ctx_attn_kernel(*refs, has_sink):
    if has_sink:
        q_ref, k_ref, v_ref, sink_ref, o_ref = refs
    else:
        q_ref, k_ref, v_ref, o_ref = refs
    q, k, v = q_ref[0], k_ref[0], v_ref[0]
    n = q.shape[0]
    masks = [_head_mask(n, 0), _head_mask(n, 1)]
    outs = []
    for h in range(2):
        qm = jnp.where(masks[h], q, jnp.zeros_like(q))
        s = _dot_nt(qm, k)
        m = jnp.max(s, axis=-1, keepdims=True)
        if has_sink:
            sk = sink_ref[0, h:h + 1, 0:1]
            m = jnp.maximum(m, sk)
        p = jnp.exp2(s - m)
        den = jnp.sum(p, axis=-1, keepdims=True)
        if has_sink:
            den = den + jnp.exp2(sk - m)
        outs.append(_dot(p.astype(BF16), v) / den)
    o_ref[0] = jnp.where(masks[0], outs[0], outs[1]).astype(BF16)


def _ctx_attn(pc_, q_off, k_off, v_off, n_q_tiles, group, sink=None):
    b, n, _ = pc_.shape
    in_specs = [pl.BlockSpec((1, n, LANES), lambda bi, t: (bi, 0, q_off + t)),
                pl.BlockSpec((1, n, LANES), lambda bi, t: (bi, 0, k_off + t // group)),
                pl.BlockSpec((1, n, LANES), lambda bi, t: (bi, 0, v_off + t // group))]
    args = [pc_, pc_, pc_]
    if sink is not None:
        in_specs.append(pl.BlockSpec((1, 2, LANES), lambda bi, t: (t, 0, 0)))
        args.append(sink)
    return pl.pallas_call(
        functools.partial(_ctx_attn_kernel, has_sink=sink is not None),
        out_shape=jax.ShapeDtypeStruct((b, n, n_q_tiles * LANES), BF16),
        grid=(b, n_q_tiles),
        in_specs=in_specs,
        out_specs=pl.BlockSpec((1, n, LANES), lambda bi, t: (bi, 0, t)),
        compiler_params=_params(2),
        name="ctx_attn",
    )(*args)


def _gla_kernel(*refs, n_lat_blocks, need_ctx):
    (ql, kl, vl, zl, rl, qc, kc, vc, zc, rc, w2f, w2b, baf, bab, g_ref) = refs[:15]
    n_out = 2 if need_ctx else 1
    yl_ref = refs[15]
    yc_ref = refs[16] if need_ctx else None
    qt_s, kd_s, dec_s, vt_s, o_s = refs[15 + n_out:]
    blk, ck = GLA_BLOCK, GLA_CHUNK
    cpb = blk // ck
    shift = ck.bit_length() - 1
    ri = lax.broadcasted_iota(jnp.int32, (blk, blk), 0)
    ci = lax.broadcasted_iota(jnp.int32, (blk, blk), 1)
    same = lax.shift_right_logical(ri, shift) == lax.shift_right_logical(ci, shift)
    att_mask = [same & (ci <= ri), same & (ci >= ri)]
    as_bf16 = lambda m: jnp.where(m, 1.0, 0.0).astype(BF16)
    cum = [jnp.concatenate([as_bf16(att_mask[d]), as_bf16(same)], axis=0) for d in range(2)]
    head = [_head_mask(blk, 0), _head_mask(blk, 1)]
    head_ck = [_head_mask(ck, 0), _head_mask(ck, 1)]
    tok_lo = lax.broadcasted_iota(jnp.int32, (2 * GLA_DV, LANES), 1) < ck
    w2 = jnp.concatenate([w2f[...], w2b[...]], axis=1)
    ba = jnp.concatenate([baf[...], bab[...]], axis=1)
    g = g_ref[...]
    lat4, ctx4 = (ql, kl, vl, zl), (qc, kc, vc, zc)

    def phase1(blocks):
        gates = []
        for (_, _, _, z_ref), t_in, _ in blocks:
            x = _dot(z_ref[0, pl.ds(t_in, blk), :], w2) + ba
            la = (jnp.minimum(x, 0.0) - jnp.log(1.0 + jnp.exp(-jnp.abs(x)))) * (1.0 / GLA_TAU)
            hi = la.astype(BF16)
            gates.append((hi, (la - hi.astype(F32)).astype(BF16)))
        cums = []
        for hi, lo in gates:
            per_dir = []
            for d in range(2):
                c2 = _dot(cum[d], jnp.concatenate([hi[:, d * LANES:(d + 1) * LANES],
                                                   lo[:, d * LANES:(d + 1) * LANES]], axis=1))
                per_dir.append((c2[:blk, :LANES] + c2[:blk, LANES:], c2[blk:, :LANES] + c2[blk:, LANES:]))
            cums.append(per_dir)
        scores = []
        for ((q_ref, k_ref, _, _), t_in, bo), per_dir in zip(blocks, cums):
            q = q_ref[0, pl.ds(t_in, blk), :].astype(F32)
            k = k_ref[0, pl.ds(t_in, blk), :].astype(F32)
            t_out = pl.multiple_of(bo * blk, blk)
            items = []
            for d, (b, tot) in enumerate(per_dir):
                qt = q * jnp.exp(b)
                kt = (k * jnp.exp(-b)).astype(BF16)
                qt_s[d, pl.ds(t_out, blk), :] = qt.astype(BF16)
                kd_s[d, pl.ds(t_out, blk), :] = (k * jnp.exp(tot - b)).astype(BF16)
                for j in range(cpb):
                    dec_s[d, pl.ds(bo * cpb + j, 1), :] = jnp.exp(tot[j * ck:j * ck + 1])
                for h in range(2):
                    items.append((d, h, _dot_nt(jnp.where(head[h], qt, 0.0).astype(BF16), kt)))
            scores.append(items)
        for ((_, _, v_ref, _), t_in, bo), items in zip(blocks, scores):
            v = v_ref[0, pl.ds(t_in, blk), :]
            oi = [None, None]
            for d, h, s in items:
                c = _dot(jnp.where(att_mask[d], s, 0.0).astype(BF16), v[:, h * GLA_DV:(h + 1) * GLA_DV])
                oi[h] = c if oi[h] is None else oi[h] + c
            o_s[pl.ds(pl.multiple_of(bo * blk, blk), blk), :] = jnp.concatenate(oi, axis=1)
            vt_s[bo] = v.astype(F32).T.astype(BF16)

    orders = [list(range(cpb)), list(range(cpb - 1, -1, -1))]

    def scan_pair(bos, sts):
        upd = [{}, {}]
        for d in range(2):
            vt = vt_s[bos[d]]
            kd = kd_s[d, pl.ds(pl.multiple_of(bos[d] * blk, blk), blk), :]
            for c in orders[d]:
                half = c // 2
                vt_c = vt[:, half * LANES:(half + 1) * LANES]
                vt_c = jnp.where(tok_lo if c % 2 == 0 else ~tok_lo, vt_c, jnp.zeros_like(vt_c))
                upd[d][c] = _dot(vt_c, kd[half * LANES:(half + 1) * LANES])
        sts = list(sts)
        for step in range(cpb):
            for d in range(2):
                c = orders[d][step]
                tc = pl.multiple_of(bos[d] * blk + c * ck, ck)
                qt = qt_s[d, pl.ds(tc, ck), :]
                stb = sts[d].astype(BF16)
                outs = [_dot_nt(jnp.where(head_ck[h], qt, jnp.zeros_like(qt)), stb[h * GLA_DV:(h + 1) * GLA_DV])
                        for h in range(2)]
                o_s[pl.ds(tc, ck), :] += jnp.concatenate(outs, axis=1)
                sts[d] = dec_s[d, pl.ds(bos[d] * cpb + c, 1), :] * sts[d] + upd[d][c]
        return tuple(sts)

    def finalize(bo, r_ref, t_in, y_ref):
        o = o_s[pl.ds(pl.multiple_of(bo * blk, blk), blk), :]
        r = r_ref[0, pl.ds(t_in, blk), :].astype(F32)
        on = jnp.concatenate(
            [o[:, h * GLA_DV:(h + 1) * GLA_DV]
             * lax.rsqrt(jnp.mean(o[:, h * GLA_DV:(h + 1) * GLA_DV] ** 2, axis=-1, keepdims=True) + EPS)
             for h in range(2)], axis=1)
        y_ref[0, pl.ds(t_in, blk), :] = (on * g * _silu(r)).astype(BF16)

    phase1([(ctx4, 0, 0)])

    def phase1_lat(i, carry):
        phase1([(lat4, pl.multiple_of((2 * i + j) * blk, blk), 1 + 2 * i + j) for j in range(2)])
        return carry

    lax.fori_loop(0, n_lat_blocks // 2, phase1_lat, 0)

    zero = jnp.zeros((2 * GLA_DV, 2 * GLA_DK), F32)
    sts = scan_pair((0, 0), (zero, zero))
    lax.fori_loop(0, n_lat_blocks, lambda i, s: scan_pair((1 + i, n_lat_blocks - i), s), sts)

    if need_ctx:
        finalize(0, rc, 0, yc_ref)

    def finalize_lat(i, carry):
        for j in range(2):
            finalize(1 + 2 * i + j, rl, pl.multiple_of((2 * i + j) * blk, blk), yl_ref)
        return carry

    lax.fori_loop(0, n_lat_blocks // 2, finalize_lat, 0)


def _gla(pl_, pc_, w2, ba, g, need_ctx):
    b, s, _ = pl_.shape
    n_c = pc_.shape[1]
    n_pairs = GLA_HEADS // 2
    w = 2 * GLA_DV

    def stream(n):
        return [pl.BlockSpec((1, n, LANES), lambda bi, p: (bi, 0, AB_QB + p)),
                pl.BlockSpec((1, n, LANES), lambda bi, p: (bi, 0, AB_KB + p)),
                pl.BlockSpec((1, n, w), lambda bi, p: (bi, 0, AB_VB + p)),
                pl.BlockSpec((1, n, LANES), lambda bi, p: (bi, 0, AB_Z)),
                pl.BlockSpec((1, n, w), lambda bi, p: (bi, 0, AB_RB + p))]

    in_specs = stream(s) + stream(n_c) + [
        pl.BlockSpec((LANES, LANES), lambda bi, p: (0, p)),
        pl.BlockSpec((LANES, LANES), lambda bi, p: (0, n_pairs + p)),
        pl.BlockSpec((1, LANES), lambda bi, p: (0, p)),
        pl.BlockSpec((1, LANES), lambda bi, p: (0, n_pairs + p)),
        pl.BlockSpec((1, w), lambda bi, p: (0, p))]
    out_shape = [jax.ShapeDtypeStruct((b, s, GLA_VW), BF16)]
    out_specs = [pl.BlockSpec((1, s, w), lambda bi, p: (bi, 0, p))]
    if need_ctx:
        out_shape.append(jax.ShapeDtypeStruct((b, n_c, GLA_VW), BF16))
        out_specs.append(pl.BlockSpec((1, n_c, w), lambda bi, p: (bi, 0, p)))
    assert n_c == GLA_BLOCK and s % (2 * GLA_BLOCK) == 0
    n_lat_blocks = s // GLA_BLOCK
    t_all = n_c + s
    n_chunks = -(-(t_all // GLA_CHUNK) // 8) * 8
    scratch = [pltpu.VMEM((2, t_all, LANES), BF16),
               pltpu.VMEM((2, t_all, LANES), BF16),
               pltpu.VMEM((2, n_chunks, LANES), F32),
               pltpu.VMEM((1 + n_lat_blocks, w, GLA_BLOCK), BF16),
               pltpu.VMEM((t_all, w), F32)]
    res = pl.pallas_call(
        functools.partial(_gla_kernel, n_lat_blocks=n_lat_blocks, need_ctx=need_ctx),
        out_shape=out_shape,
        grid=(b, n_pairs),
        in_specs=in_specs,
        out_specs=out_specs,
        scratch_shapes=scratch,
        compiler_params=_params(2),
        name="gla",
    )(*([pl_] * 5), *([pc_] * 5), w2, w2, ba, ba, g.reshape(1, GLA_VW))
    return (res[0], res[1]) if need_ctx else (res[0], None)


def _gla_gate_weights(wa2, ba):
    w = jnp.zeros((LANES, 2 * GLA_KW), F32)
    w = w.at[0:GLA_RANK, 0:GLA_KW].set(wa2[0])
    w = w.at[GLA_RANK:2 * GLA_RANK, GLA_KW:].set(wa2[1])
    return w.astype(BF16), ba.reshape(1, 2 * GLA_KW).astype(F32)


def _swa_kernel(q_ref, kp_ref, kn_ref, kx_ref, vp_ref, vn_ref, vx_ref, kc_ref, vc_ref, sink_ref, o_ref):
    n = pl.program_id(1)
    blk = SWA_BLOCK
    n_c = kc_ref.shape[1]
    n_keys = 3 * blk + n_c
    k_lat = jnp.concatenate([kp_ref[0], kn_ref[0], kx_ref[0]], axis=0)
    v_lat = jnp.concatenate([vp_ref[0], vn_ref[0], vx_ref[0]], axis=0)
    qi = lax.broadcasted_iota(jnp.int32, (blk, n_keys), 0)
    kj = lax.broadcasted_iota(jnp.int32, (blk, n_keys), 1)
    in_prev = (kj < blk) & (kj >= qi)
    in_next = (kj >= 2 * blk) & (kj < 3 * blk) & (kj - 2 * blk <= qi)
    always = (kj >= 3 * blk) | ((kj >= blk) & (kj < 2 * blk))
    first, last = n == 0, n == pl.num_programs(1) - 1
    ones = jnp.ones((n_keys, LANES), BF16)
    mask_a = _head_mask(blk, 0)
    half = SWA_GROUP // 2
    n_pairs = SWA_KV // 2
    n_chunks = SWA_QB * n_pairs * 2

    def block_inputs(j):
        k = jnp.concatenate([k_lat[j * blk:(j + 3) * blk], kc_ref[0]], axis=0)
        v = jnp.concatenate([v_lat[j * blk:(j + 3) * blk], vc_ref[0]], axis=0)
        has_prev = jnp.logical_not(first) if j == 0 else True
        has_next = jnp.logical_not(last) if j == SWA_QB - 1 else True
        return k, v, always | (in_prev & has_prev) | (in_next & has_next)

    blocks = [block_inputs(j) for j in range(SWA_QB)]

    def scores(c):
        j, pair = c // (2 * n_pairs), (c // 2) % n_pairs
        k, _, ok = blocks[j]
        tiles = [pair * SWA_GROUP + (c % 2) * half + g for g in range(half)]
        q_parts, sink_parts = [], []
        for t in tiles:
            q = q_ref[0, j * blk:(j + 1) * blk, t * LANES:(t + 1) * LANES]
            zero = jnp.zeros_like(q)
            q_parts += [jnp.where(mask_a, q, zero), jnp.where(mask_a, zero, q)]
            sink_parts += [jnp.broadcast_to(sink_ref[t, h:h + 1, 0:1], (blk, 1)) for h in range(2)]
        s = _dot_nt(jnp.concatenate(q_parts, axis=0), k[:, pair * LANES:(pair + 1) * LANES])
        s = jnp.concatenate([jnp.where(ok, s[i * blk:(i + 1) * blk], NEG_INF) for i in range(2 * half)], axis=0)
        return j, tiles, pair, s, jnp.concatenate(sink_parts, axis=0)

    def finish(j, tiles, pair, s, sk):
        m = jnp.maximum(jnp.max(s, axis=-1, keepdims=True), sk)
        p = jnp.exp2(s - m).astype(BF16)
        o_ext = _dot(p, jnp.concatenate([blocks[j][1][:, pair * LANES:(pair + 1) * LANES], ones], axis=1))
        o = o_ext[:, :LANES] / (o_ext[:, LANES:] + jnp.exp2(sk - m))
        for i, t in enumerate(tiles):
            o_ref[0, j * blk:(j + 1) * blk, t * LANES:(t + 1) * LANES] = jnp.where(
                mask_a, o[2 * i * blk:(2 * i + 1) * blk], o[(2 * i + 1) * blk:(2 * i + 2) * blk]).astype(BF16)

    pending = scores(0)
    for c in range(n_chunks):
        nxt = scores(c + 1) if c + 1 < n_chunks else None
        finish(*pending)
        pending = nxt


def _swa(pl_, pc_, sink):
    b, s, _ = pl_.shape
    n_c = pc_.shape[1]
    nb = s // SWA_BLOCK
    step = SWA_QB * SWA_BLOCK
    assert s % step == 0
    qw = SWA_HEADS * HEAD_DIM
    kw = SWA_KV * HEAD_DIM
    k_blk, v_blk = (C_K * LANES) // kw, (C_V * LANES) // kw

    def kv(off):
        return [pl.BlockSpec((1, SWA_BLOCK, kw), lambda bi, n: (bi, jnp.maximum(n * SWA_QB - 1, 0), off)),
                pl.BlockSpec((1, step, kw), lambda bi, n: (bi, n, off)),
                pl.BlockSpec((1, SWA_BLOCK, kw), lambda bi, n: (bi, jnp.minimum((n + 1) * SWA_QB, nb - 1), off))]

    in_specs = ([pl.BlockSpec((1, step, qw), lambda bi, n: (bi, n, 0))] + kv(k_blk) + kv(v_blk)
                + [pl.BlockSpec((1, n_c, kw), lambda bi, n: (bi, 0, k_blk)),
                   pl.BlockSpec((1, n_c, kw), lambda bi, n: (bi, 0, v_blk)),
                   _resident(sink.shape)])
    return pl.pallas_call(
        _swa_kernel,
        out_shape=jax.ShapeDtypeStruct((b, s, qw), BF16),
        grid=(b, s // step),
        in_specs=in_specs,
        out_specs=pl.BlockSpec((1, step, qw), lambda bi, n: (bi, n, 0)),
        compiler_params=_params(2),
        name="swa",
    )(pl_, pl_, pl_, pl_, pl_, pl_, pl_, pc_, pc_, sink)


def _swa_pair_heads(a, axis):
    shape = a.shape
    a = a.reshape(shape[:axis] + (SWA_KV // 2, 2, SWA_GROUP, -1) + shape[axis + 1:])
    return jnp.swapaxes(a, axis + 1, axis + 2).reshape(shape)


def _rope_tables(n):
    t = jnp.arange(n)
    row = (t // GRID_W).astype(F32)
    col = (t % GRID_W).astype(F32)
    half = HEAD_DIM // 2
    inv = 1.0 / (ROPE_BASE ** (jnp.arange(0, half, 2, dtype=F32) / half))
    ang = jnp.concatenate([row[:, None] * inv, col[:, None] * inv], axis=-1)
    cos = jnp.repeat(jnp.cos(ang), 2, axis=1)
    sin = jnp.repeat(jnp.sin(ang), 2, axis=1) * jnp.tile(jnp.array([-1.0, 1.0], F32), half)
    reps = ROPE_TILE // HEAD_DIM
    return jnp.tile(cos, (1, reps)), jnp.tile(sin, (1, reps))


def kernel(x, c, ctx, c_ctx, w_mod, b_mod, g_mix_pre, g_mix_post, g_ffn_pre, g_ffn_post, w_out, w_up, conv_w,
           conv_b, w_down, w_in_ab, na_rpb, gla_wa2, gla_ba, gla_g, w_in_c, swa_sink):
    bsz, seq, d = x.shape
    n_ctx = ctx.shape[1]
    depth = w_mod.shape[0]
    tm_l = 512 if seq % 512 == 0 else 256
    tm_c = n_ctx
    tm_o = 1024 if seq % 1024 == 0 else tm_l

    c_all = jnp.zeros((8, d), F32).at[:bsz].set(c).at[bsz].set(c_ctx)
    mods = _modulation(c_all, w_mod, b_mod).reshape(depth, 8, 6, d)
    rope = _rope_tables(seq)

    col_scale_ab = np.ones((IN_AB,), np.float32)
    col_scale_ab[0:NA_WIDTH] = ATTN_SCALE
    col_scale_ab[3 * NA_WIDTH:3 * NA_WIDTH + GLA_KW] = GLA_DK ** -0.5

    n_q = SWA_HEADS * HEAD_DIM
    w_ab = jnp.pad(w_in_ab * col_scale_ab, ((0, 0), (0, 0), (0, IN_AB_PAD - IN_AB))).astype(BF16)
    w_c = jnp.concatenate([_swa_pair_heads(w_in_c[:, :, :n_q], 2) * ATTN_SCALE, w_in_c[:, :, n_q:]],
                          axis=2).astype(BF16)
    w_o = w_out.astype(BF16)
    head_rows = np.arange(n_q).reshape(SWA_KV // 2, 2, SWA_GROUP, HEAD_DIM).swapaxes(1, 2).reshape(-1)
    w_o_swa = jnp.take(w_o[1::2], head_rows, axis=1)
    wu = w_up.astype(BF16)
    wd = w_down.astype(BF16)
    cwb = jnp.concatenate([conv_w, conv_b[:, None]], axis=1)
    cw = jnp.concatenate([cwb[:, :, :D_FF].reshape(depth, 4, N_FF_CHUNKS, FF_CHUNK),
                          cwb[:, :, D_FF:].reshape(depth, 4, N_FF_CHUNKS, FF_CHUNK)], axis=3).transpose(0, 2, 1, 3)

    xl, xc = x, ctx
    for i in range(depth):
        need_ctx = i < depth - 1
        j = i // 2
        mod_l = mods[i, :bsz]
        mod_c = mods[i, bsz:bsz + 1]
        if i % 2 == 0:
            p_l = _proj(xl, mod_l, g_mix_pre[i], w_ab, j, tm_l, 640)
            p_c = _proj(xc, mod_c, g_mix_pre[i], w_ab, j, tm_c, 640)
            w2, ba = _gla_gate_weights(gla_wa2[j], gla_ba[j])
            y_na = _na(p_l, p_c, _na_bias(na_rpb[j]))
            y_gla, yc_gla = _gla(p_l, p_c, w2, ba, gla_g[j], need_ctx)
            ys_l = [y_na, y_gla]
            w_post, l_post = w_o, i
            if need_ctx:
                ys_c = [_ctx_attn(p_c, AB_QA, AB_KA, AB_VA, NA_HEADS // 2, 1), yc_gla]
        else:
            p_l = _proj(xl, mod_l, g_mix_pre[i], w_c, j, tm_l, ROPE_TILE, rope, (C_V * LANES) // ROPE_TILE)
            p_c = _proj(xc, mod_c, g_mix_pre[i], w_c, j, tm_c, ROPE_TILE)
            sink = jnp.broadcast_to(_swa_pair_heads(swa_sink[j] * LOG2E, 0).reshape(-1, 2, 1),
                                    (SWA_HEADS // 2, 2, LANES)).astype(F32)
            ys_l = [_swa(p_l, p_c, sink)]
            w_post, l_post = w_o_swa, j
            if need_ctx:
                ys_c = [_ctx_attn(p_c, C_Q, C_K, C_V, SWA_HEADS // 2, SWA_GROUP, sink)]

        xl = _outproj(ys_l, w_post, l_post, xl, mod_l, g_mix_post[i], tm_o)
        xl = _ffn(xl, mod_l, g_ffn_pre[i], g_ffn_post[i], wu, cw, wd, i, tm_l)
        if need_ctx:
            xc = _outproj(ys_c, w_post, l_post, xc, mod_c, g_mix_post[i], tm_c)
            xc = _ffn(xc, mod_c, g_ffn_pre[i], g_ffn_post[i], wu, cw, wd, i, tm_c)
    return xl
```

```python
import functools

import numpy as np
import jax
import jax.numpy as jnp
from jax import lax
from jax.experimental import pallas as pl
from jax.experimental.pallas import tpu as pltpu

F32 = jnp.float32
BF16 = jnp.bfloat16

LANES = 128
HEAD_DIM = 64
GRID_W = 64
D_MODEL = 1024
EPS = 1e-6
NEG_INF = -1e30
LOG2E = float(np.log2(np.e))
ATTN_SCALE = HEAD_DIM ** -0.5 * LOG2E

NA_HEADS = 8
NA_WIN_H = 8
NA_WIN_W = 16
NA_WIDTH = NA_HEADS * HEAD_DIM
NA_GROUP = 16
GLA_HEADS = 4
GLA_DK = 64
GLA_DV = 128
GLA_KW = GLA_HEADS * GLA_DK
GLA_VW = GLA_HEADS * GLA_DV
GLA_RANK = 16
GLA_TAU = 16.0
GLA_CHUNK = 64
GLA_BLOCK = 256
IN_AB = 3 * NA_WIDTH + 2 * GLA_KW + 2 * GLA_VW + 2 * GLA_RANK
IN_AB_PAD = 3200
SWA_HEADS = 16
SWA_KV = 4
SWA_GROUP = SWA_HEADS // SWA_KV
SWA_BLOCK = 128
SWA_QB = 2
IN_C = (SWA_HEADS + 2 * SWA_KV) * HEAD_DIM
D_FF = 2816
FF_CHUNK = 256
N_FF_CHUNKS = D_FF // FF_CHUNK
assert N_FF_CHUNKS * FF_CHUNK == D_FF
HALO = 16
OUTPROJ_SUB = 256
PROJ_SUB = 256
ROPE_BASE = 10000.0
ROPE_TILE = 256
VMEM_LIMIT = 56 * 1024 * 1024

AB_QA, AB_KA, AB_VA = 0, 4, 8
AB_QB, AB_KB = 12, 14
AB_VB, AB_RB = 8, 10
AB_Z = 24
C_Q, C_K, C_V = 0, 8, 10


def _dot(a, b):
    return jnp.dot(a, b, preferred_element_type=F32)


def _dot_nt(a, b):
    return lax.dot_general(a, b, (((1,), (1,)), ((), ())), preferred_element_type=F32)


def _rms(x, g):
    return x * lax.rsqrt(jnp.mean(x * x, axis=-1, keepdims=True) + EPS) * g


def _silu(x):
    return x / (1.0 + jnp.exp(-x))


def _params(n_grid):
    return pltpu.CompilerParams(dimension_semantics=("arbitrary",) * n_grid,
                                vmem_limit_bytes=VMEM_LIMIT)


def _resident(shape):
    nd = len(shape)
    return pl.BlockSpec(shape, lambda *_: (0,) * nd, pipeline_mode=pl.Buffered(1))


def _layer_block(arr, layer, rows=None, row_block=0):
    shape = list(arr.shape[1:])
    if rows is not None:
        shape[0] = rows
    tail = (0,) * (len(shape) - 1)
    return pl.BlockSpec((None, *shape), lambda *_: (layer, row_block) + tail, pipeline_mode=pl.Buffered(1))


def _head_mask(rows, h):
    lane = lax.broadcasted_iota(jnp.int32, (rows, LANES), 1)
    return (lane < HEAD_DIM) if h == 0 else (lane >= HEAD_DIM)


def _mod_kernel(c_ref, w_ref, b_ref, o_ref):
    s = _silu(c_ref[...]).astype(BF16)
    o_ref[0] = _dot(s, w_ref[0].astype(BF16)) + b_ref[0]


def _modulation(c_all, w_mod, b_mod):
    depth, d, n = w_mod.shape
    tn = 1536
    return pl.pallas_call(
        _mod_kernel,
        out_shape=jax.ShapeDtypeStruct((depth, 8, n), F32),
        grid=(depth, n // tn),
        in_specs=[pl.BlockSpec((8, d), lambda i, j: (0, 0)),
                  pl.BlockSpec((1, d, tn), lambda i, j: (i, 0, j)),
                  pl.BlockSpec((1, 1, tn), lambda i, j: (i, 0, j))],
        out_specs=pl.BlockSpec((1, 8, tn), lambda i, j: (i, 0, j)),
        compiler_params=_params(2),
        name="modulation",
    )(c_all, w_mod, b_mod.reshape(depth, 1, n))


def _rope(acc, cos, sin):
    n = acc.shape[1]
    lane = lax.broadcasted_iota(jnp.int32, acc.shape, 1)
    partner = jnp.where(lane % 2 == 0, pltpu.roll(acc, n - 1, 1), pltpu.roll(acc, 1, 1))
    return acc * cos + partner * sin


def _proj_kernel(*refs, chunk, rope_chunks, shift_idx):
    if rope_chunks:
        x_ref, mod_ref, g_ref, w_ref, cos_ref, sin_ref, o_ref = refs
    else:
        x_ref, mod_ref, g_ref, w_ref, o_ref = refs
    mod = mod_ref[0]
    tm = x_ref.shape[1]
    sub = min(tm, PROJ_SUB)
    for r0 in range(0, tm, sub):
        h = (_rms(x_ref[0, r0:r0 + sub], g_ref[...]) * (1.0 + mod[shift_idx + 1:shift_idx + 2])
             + mod[shift_idx:shift_idx + 1])
        hb = h.astype(BF16)
        for c in range(w_ref.shape[1] // chunk):
            acc = _dot(hb, w_ref[:, c * chunk:(c + 1) * chunk])
            if c < rope_chunks:
                acc = _rope(acc, cos_ref[r0:r0 + sub], sin_ref[r0:r0 + sub])
            o_ref[0, r0:r0 + sub, c * chunk:(c + 1) * chunk] = acc.astype(BF16)


def _proj(x, mod, g, w, layer, tm, chunk, rope=None, rope_chunks=0):
    b, s, d = x.shape
    n = w.shape[2]
    per_batch = mod.shape[0] > 1
    in_specs = [pl.BlockSpec((1, tm, d), lambda bi, i: (bi, i, 0)),
                pl.BlockSpec((1, 6, d), (lambda bi, i: (bi, 0, 0)) if per_batch else (lambda bi, i: (0, 0, 0))),
                _resident((1, d)),
                _layer_block(w, layer)]
    args = [x, mod, g.reshape(1, d), w]
    if rope_chunks:
        in_specs += [pl.BlockSpec((tm, chunk), lambda bi, i: (i, 0))] * 2
        args += list(rope)
    return pl.pallas_call(
        functools.partial(_proj_kernel, chunk=chunk, rope_chunks=rope_chunks, shift_idx=0),
        out_shape=jax.ShapeDtypeStruct((b, s, n), BF16),
        grid=(b, s // tm),
        in_specs=in_specs,
        out_specs=pl.BlockSpec((1, tm, n), lambda bi, i: (bi, i, 0)),
        compiler_params=_params(2),
        name="proj",
    )(*args)


def _outproj_kernel(*refs, n_parts):
    y_refs, w_refs = refs[:n_parts], refs[n_parts:2 * n_parts]
    x_ref, mod_ref, g_ref, o_ref = refs[2 * n_parts:]
    tm = x_ref.shape[1]
    sub = min(tm, OUTPROJ_SUB)
    gate, g = mod_ref[0][2:3], g_ref[...]
    accs = []
    for r0 in range(0, tm, sub):
        acc = _dot(y_refs[0][0, r0:r0 + sub], w_refs[0][...])
        for yr, wr in zip(y_refs[1:], w_refs[1:]):
            acc = acc + _dot(yr[0, r0:r0 + sub], wr[...])
        accs.append(acc)
    for k, acc in enumerate(accs):
        o_ref[0, k * sub:(k + 1) * sub] = x_ref[0, k * sub:(k + 1) * sub] + gate * _rms(acc, g)


def _outproj(ys, w, layer, x, mod, g, tm):
    b, s, d = x.shape
    per_batch = mod.shape[0] > 1
    in_specs = [pl.BlockSpec((1, tm, y.shape[2]), lambda bi, i: (bi, i, 0)) for y in ys]
    in_specs += [_layer_block(w, layer, y.shape[2], k) for k, y in enumerate(ys)]
    in_specs += [pl.BlockSpec((1, tm, d), lambda bi, i: (bi, i, 0)),
                 pl.BlockSpec((1, 6, d), (lambda bi, i: (bi, 0, 0)) if per_batch else (lambda bi, i: (0, 0, 0))),
                 _resident((1, d))]
    return pl.pallas_call(
        functools.partial(_outproj_kernel, n_parts=len(ys)),
        out_shape=jax.ShapeDtypeStruct((b, s, d), F32),
        grid=(b, s // tm),
        in_specs=in_specs,
        out_specs=pl.BlockSpec((1, tm, d), lambda bi, i: (bi, i, 0)),
        compiler_params=_params(2),
        name="outproj",
    )(*ys, *([w] * len(ys)), x, mod, g.reshape(1, d))


def _ffn_kernel(x_ref, xp_ref, xn_ref, mod_ref, gpre_ref, gpost_ref, wu_ref, cw_ref, wd_ref, o_ref,
                h_ref, u_ref, a_ref, *, tm):
    i = pl.program_id(1)
    mod = mod_ref[0]
    gpre = gpre_ref[...]

    def norm_mod(x):
        return _rms(x, gpre) * (1.0 + mod[4:5]) + mod[3:4]

    keep_prev = (i > 0).astype(F32)
    keep_next = (i < pl.num_programs(1) - 1).astype(F32)
    rows = tm + 2 * HALO
    half = tm // 2
    mid = HALO + half
    n_u = u_ref.shape[0]

    def up(c, r0=0, r1=rows):
        h = h_ref[r0:r1]
        for part in range(2):
            col = part * D_FF + c * FF_CHUNK
            u_ref[c % n_u, r0:r1, part * FF_CHUNK:(part + 1) * FF_CHUNK] = _dot(h, wu_ref[:, col:col + FF_CHUNK])

    h_ref[0:HALO] = (norm_mod(xp_ref[0]) * keep_prev).astype(BF16)
    h_ref[HALO:mid] = norm_mod(x_ref[0, :half]).astype(BF16)
    up(0, 0, mid)
    h_ref[mid:HALO + tm] = norm_mod(x_ref[0, half:]).astype(BF16)
    h_ref[HALO + tm:] = (norm_mod(xn_ref[0]) * keep_next).astype(BF16)
    up(0, mid, rows)

    def act(c):
        u = u_ref[c % n_u]
        cw = cw_ref[c]
        y = (pltpu.roll(u, 1, 0)[HALO:HALO + tm] * cw[0:1]
             + u[HALO:HALO + tm] * cw[1:2]
             + pltpu.roll(u, rows - 1, 0)[HALO:HALO + tm] * cw[2:3]
             + cw[3:4])
        a_ref[:, c * FF_CHUNK:(c + 1) * FF_CHUNK] = (_silu(y[:, :FF_CHUNK]) * y[:, FF_CHUNK:]).astype(BF16)

    for c in range(1, min(n_u - 1, N_FF_CHUNKS)):
        up(c)
    for c in range(N_FF_CHUNKS):
        if c + n_u - 1 < N_FF_CHUNKS:
            up(c + n_u - 1)
        act(c)
    accs = [_dot(a_ref[r0:r0 + half], wd_ref[...]) for r0 in (0, half)]
    for r0, acc in zip((0, half), accs):
        o_ref[0, r0:r0 + half] = x_ref[0, r0:r0 + half] + mod[5:6] * _rms(acc, gpost_ref[...])


def _ffn(x, mod, g_pre, g_post, wu, cw, wd, layer, tm):
    b, s, d = x.shape
    per_batch = mod.shape[0] > 1
    hb = tm // HALO
    n_halo = s // HALO
    return pl.pallas_call(
        functools.partial(_ffn_kernel, tm=tm),
        out_shape=jax.ShapeDtypeStruct((b, s, d), F32),
        grid=(b, s // tm),
        in_specs=[pl.BlockSpec((1, tm, d), lambda bi, i: (bi, i, 0)),
                  pl.BlockSpec((1, HALO, d), lambda bi, i: (bi, jnp.maximum(i * hb - 1, 0), 0)),
                  pl.BlockSpec((1, HALO, d), lambda bi, i: (bi, jnp.minimum((i + 1) * hb, n_halo - 1), 0)),
                  pl.BlockSpec((1, 6, d), (lambda bi, i: (bi, 0, 0)) if per_batch else (lambda bi, i: (0, 0, 0))),
                  _resident((1, d)), _resident((1, d)),
                  _layer_block(wu, layer), _layer_block(cw, layer), _layer_block(wd, layer)],
        out_specs=pl.BlockSpec((1, tm, d), lambda bi, i: (bi, i, 0)),
        scratch_shapes=[pltpu.VMEM((tm + 2 * HALO, d), BF16),
                        pltpu.VMEM((3, tm + 2 * HALO, 2 * FF_CHUNK), F32),
                        pltpu.VMEM((tm, D_FF), BF16)],
        compiler_params=_params(2),
        name="ffn",
    )(x, x, x, mod, g_pre.reshape(1, d), g_post.reshape(1, d), wu, cw, wd)


def _na_kernel(q_ref, k_ref, v_ref, kc_ref, vc_ref, bias_ref, o_ref, *, rows):
    kc = kc_ref[0]
    vc = vc_ref[0]
    win = NA_WIN_H * GRID_W
    mask_a = _head_mask(GRID_W, 0)
    vc_ext = jnp.concatenate([vc, jnp.ones_like(vc)], axis=1)
    ones_w = jnp.ones((win, LANES), BF16)

    def group(gi, carry):
        starts, scores, probs = [], [], []
        for i in range(NA_GROUP):
            r = gi * NA_GROUP + i
            r0 = jnp.clip(r - NA_WIN_H // 2, 0, rows - NA_WIN_H)
            q0 = pl.multiple_of(r * GRID_W, GRID_W)
            k0 = pl.multiple_of(r0 * GRID_W, GRID_W)
            q = q_ref[0, pl.ds(q0, GRID_W), :]
            zero = jnp.zeros_like(q)
            q2 = jnp.concatenate([jnp.where(mask_a, q, zero), jnp.where(mask_a, zero, q)], axis=0)
            d0 = r0 - r + NA_WIN_H - 1
            bias = jnp.concatenate(
                [jnp.concatenate([bias_ref[0, hh, d0 + 2 * a] for a in range(NA_WIN_H // 2)], axis=1)
                 for hh in range(2)], axis=0)
            s = _dot_nt(q2, k_ref[0, pl.ds(k0, win), :]) + bias
            sc = _dot_nt(q2, kc)
            starts.append((q0, k0))
            scores.append((s, sc))
        for s, sc in scores:
            m = jnp.maximum(jnp.max(s, axis=-1, keepdims=True), jnp.max(sc, axis=-1, keepdims=True))
            probs.append((jnp.exp2(s - m).astype(BF16), jnp.exp2(sc - m).astype(BF16)))
        for (q0, k0), (p, pc) in zip(starts, probs):
            v_ext = jnp.concatenate([v_ref[0, pl.ds(k0, win), :], ones_w], axis=1)
            o_ext = _dot(p, v_ext) + _dot(pc, vc_ext)
            o2 = o_ext[:, :LANES] / o_ext[:, LANES:]
            o_ref[0, pl.ds(q0, GRID_W), :] = jnp.where(mask_a, o2[:GRID_W], o2[GRID_W:]).astype(BF16)
        return carry

    lax.fori_loop(0, rows // NA_GROUP, group, 0)


def _na(pl_, pc_, bias):
    b, s, _ = pl_.shape
    n_ctx = pc_.shape[1]
    n_pairs = NA_HEADS // 2
    return pl.pallas_call(
        functools.partial(_na_kernel, rows=s // GRID_W),
        out_shape=jax.ShapeDtypeStruct((b, s, NA_WIDTH), BF16),
        grid=(b, n_pairs),
        in_specs=[pl.BlockSpec((1, s, LANES), lambda bi, p: (bi, 0, AB_QA + p)),
                  pl.BlockSpec((1, s, LANES), lambda bi, p: (bi, 0, AB_KA + p)),
                  pl.BlockSpec((1, s, LANES), lambda bi, p: (bi, 0, AB_VA + p)),
                  pl.BlockSpec((1, n_ctx, LANES), lambda bi, p: (bi, 0, AB_KA + p)),
                  pl.BlockSpec((1, n_ctx, LANES), lambda bi, p: (bi, 0, AB_VA + p)),
                  pl.BlockSpec((1,) + bias.shape[1:], lambda bi, p: (p, 0, 0, 0, 0))],
        out_specs=pl.BlockSpec((1, s, LANES), lambda bi, p: (bi, 0, p)),
        compiler_params=_params(2),
        name="na",
    )(pl_, pl_, pl_, pc_, pc_, bias)


def _na_bias(rpb):
    h = rpb.shape[0]
    cidx = np.arange(GRID_W)
    c_start = np.clip(cidx - NA_WIN_W // 2, 0, GRID_W - NA_WIN_W)
    col_ok = (cidx[None, :] >= c_start[:, None]) & (cidx[None, :] < c_start[:, None] + NA_WIN_W)
    lo, hi = GRID_W - NA_WIN_W, GRID_W - NA_WIN_W
    rp = rpb.astype(F32) * LOG2E
    ext = jnp.concatenate([jnp.repeat(rp[..., :1], lo, axis=-1), rp, jnp.repeat(rp[..., -1:], hi, axis=-1)], axis=-1)
    bm = jnp.stack([ext[..., GRID_W - 1 - q:2 * GRID_W - 1 - q] for q in range(GRID_W)], axis=2)
    bm = jnp.where(col_ok[None, None], bm, NEG_INF)
    pairs = jnp.concatenate([bm[:, :-1], bm[:, 1:]], axis=-1)
    return pairs.reshape((h // 2, 2) + pairs.shape[1:])


def _ctx_attn_kernel(*refs, has_sink):
    if has_sink:
        q_ref, k_ref, v_ref, sink_ref, o_ref = refs
    else:
        q_ref, k_ref, v_ref, o_ref = refs
    q, k, v = q_ref[0], k_ref[0], v_ref[0]
    n = q.shape[0]
    masks = [_head_mask(n, 0), _head_mask(n, 1)]
    outs = []
    for h in range(2):
        qm = jnp.where(masks[h], q, jnp.zeros_like(q))
        s = _dot_nt(qm, k)
        m = jnp.max(s, axis=-1, keepdims=True)
        if has_sink:
            sk = sink_ref[0, h:h + 1, 0:1]
            m = jnp.maximum(m, sk)
        p = jnp.exp2(s - m)
        den = jnp.sum(p, axis=-1, keepdims=True)
        if has_sink:
            den = den + jnp.exp2(sk - m)
        outs.append(_dot(p.astype(BF16), v) / den)
    o_ref[0] = jnp.where(masks[0], outs[0], outs[1]).astype(BF16)


def _ctx_attn(pc_, q_off, k_off, v_off, n_q_tiles, group, sink=None):
    b, n, _ = pc_.shape
    in_specs = [pl.BlockSpec((1, n, LANES), lambda bi, t: (bi, 0, q_off + t)),
                pl.BlockSpec((1, n, LANES), lambda bi, t: (bi, 0, k_off + t // group)),
                pl.BlockSpec((1, n, LANES), lambda bi, t: (bi, 0, v_off + t // group))]
    args = [pc_, pc_, pc_]
    if sink is not None:
        in_specs.append(pl.BlockSpec((1, 2, LANES), lambda bi, t: (t, 0, 0)))
        args.append(sink)
    return pl.pallas_call(
        functools.partial(_ctx_attn_kernel, has_sink=sink is not None),
        out_shape=jax.ShapeDtypeStruct((b, n, n_q_tiles * LANES), BF16),
        grid=(b, n_q_tiles),
        in_specs=in_specs,
        out_specs=pl.BlockSpec((1, n, LANES), lambda bi, t: (bi, 0, t)),
        compiler_params=_params(2),
        name="ctx_attn",
    )(*args)


def _gla_kernel(*refs, n_lat_blocks, need_ctx):
    (ql, kl, vl, zl, rl, qc, kc, vc, zc, rc, w2f, w2b, baf, bab, g_ref) = refs[:15]
    n_out = 2 if need_ctx else 1
    yl_ref = refs[15]
    yc_ref = refs[16] if need_ctx else None
    qt_s, kd_s, dec_s, vt_s, o_s = refs[15 + n_out:]
    blk, ck = GLA_BLOCK, GLA_CHUNK
    cpb = blk // ck
    shift = ck.bit_length() - 1
    ri = lax.broadcasted_iota(jnp.int32, (blk, blk), 0)
    ci = lax.broadcasted_iota(jnp.int32, (blk, blk), 1)
    same = lax.shift_right_logical(ri, shift) == lax.shift_right_logical(ci, shift)
    att_mask = [same & (ci <= ri), same & (ci >= ri)]
    as_bf16 = lambda m: jnp.where(m, 1.0, 0.0).astype(BF16)
    cum = [jnp.concatenate([as_bf16(att_mask[d]), as_bf16(same)], axis=0) for d in range(2)]
    head = [_head_mask(blk, 0), _head_mask(blk, 1)]
    head_ck = [_head_mask(ck, 0), _head_mask(ck, 1)]
    tok_lo = lax.broadcasted_iota(jnp.int32, (2 * GLA_DV, LANES), 1) < ck
    w2 = jnp.concatenate([w2f[...], w2b[...]], axis=1)
    ba = jnp.concatenate([baf[...], bab[...]], axis=1)
    g = g_ref[...]
    lat4, ctx4 = (ql, kl, vl, zl), (qc, kc, vc, zc)

    def phase1(blocks):
        gates = []
        for (_, _, _, z_ref), t_in, _ in blocks:
            x = _dot(z_ref[0, pl.ds(t_in, blk), :], w2) + ba
            la = (jnp.minimum(x, 0.0) - jnp.log(1.0 + jnp.exp(-jnp.abs(x)))) * (1.0 / GLA_TAU)
            hi = la.astype(BF16)
            gates.append((hi, (la - hi.astype(F32)).astype(BF16)))
        cums = []
        for hi, lo in gates:
            per_dir = []
            for d in range(2):
                c2 = _dot(cum[d], jnp.concatenate([hi[:, d * LANES:(d + 1) * LANES],
                                                   lo[:, d * LANES:(d + 1) * LANES]], axis=1))
                per_dir.append((c2[:blk, :LANES] + c2[:blk, LANES:], c2[blk:, :LANES] + c2[blk:, LANES:]))
            cums.append(per_dir)
        scores = []
        for ((q_ref, k_ref, _, _), t_in, bo), per_dir in zip(blocks, cums):
            q = q_ref[0, pl.ds(t_in, blk), :].astype(F32)
            k = k_ref[0, pl.ds(t_in, blk), :].astype(F32)
            t_out = pl.multiple_of(bo * blk, blk)
            items = []
            for d, (b, tot) in enumerate(per_dir):
                qt = q * jnp.exp(b)
                kt = (k * jnp.exp(-b)).astype(BF16)
                qt_s[d, pl.ds(t_out, blk), :] = qt.astype(BF16)
                kd_s[d, pl.ds(t_out, blk), :] = (k * jnp.exp(tot - b)).astype(BF16)
                for j in range(cpb):
                    dec_s[d, pl.ds(bo * cpb + j, 1), :] = jnp.exp(tot[j * ck:j * ck + 1])
                for h in range(2):
                    items.append((d, h, _dot_nt(jnp.where(head[h], qt, 0.0).astype(BF16), kt)))
            scores.append(items)
        for ((_, _, v_ref, _), t_in, bo), items in zip(blocks, scores):
            v = v_ref[0, pl.ds(t_in, blk), :]
            oi = [None, None]
            for d, h, s in items:
                c = _dot(jnp.where(att_mask[d], s, 0.0).astype(BF16), v[:, h * GLA_DV:(h + 1) * GLA_DV])
                oi[h] = c if oi[h] is None else oi[h] + c
            o_s[pl.ds(pl.multiple_of(bo * blk, blk), blk), :] = jnp.concatenate(oi, axis=1)
            vt_s[bo] = v.astype(F32).T.astype(BF16)

    orders = [list(range(cpb)), list(range(cpb - 1, -1, -1))]

    def scan_pair(bos, sts):
        upd = [{}, {}]
        for d in range(2):
            vt = vt_s[bos[d]]
            kd = kd_s[d, pl.ds(pl.multiple_of(bos[d] * blk, blk), blk), :]
            for c in orders[d]:
                half = c // 2
                vt_c = vt[:, half * LANES:(half + 1) * LANES]
                vt_c = jnp.where(tok_lo if c % 2 == 0 else ~tok_lo, vt_c, jnp.zeros_like(vt_c))
                upd[d][c] = _dot(vt_c, kd[half * LANES:(half + 1) * LANES])
        sts = list(sts)
        for step in range(cpb):
            for d in range(2):
                c = orders[d][step]
                tc = pl.multiple_of(bos[d] * blk + c * ck, ck)
                qt = qt_s[d, pl.ds(tc, ck), :]
                stb = sts[d].astype(BF16)
                outs = [_dot_nt(jnp.where(head_ck[h], qt, jnp.zeros_like(qt)), stb[h * GLA_DV:(h + 1) * GLA_DV])
                        for h in range(2)]
                o_s[pl.ds(tc, ck), :] += jnp.concatenate(outs, axis=1)
                sts[d] = dec_s[d, pl.ds(bos[d] * cpb + c, 1), :] * sts[d] + upd[d][c]
        return tuple(sts)

    def finalize(bo, r_ref, t_in, y_ref):
        o = o_s[pl.ds(pl.multiple_of(bo * blk, blk), blk), :]
        r = r_ref[0, pl.ds(t_in, blk), :].astype(F32)
        on = jnp.concatenate(
            [o[:, h * GLA_DV:(h + 1) * GLA_DV]
             * lax.rsqrt(jnp.mean(o[:, h * GLA_DV:(h + 1) * GLA_DV] ** 2, axis=-1, keepdims=True) + EPS)
             for h in range(2)], axis=1)
        y_ref[0, pl.ds(t_in, blk), :] = (on * g * _silu(r)).astype(BF16)

    phase1([(ctx4, 0, 0)])

    def phase1_lat(i, carry):
        phase1([(lat4, pl.multiple_of((2 * i + j) * blk, blk), 1 + 2 * i + j) for j in range(2)])
        return carry

    lax.fori_loop(0, n_lat_blocks // 2, phase1_lat, 0)

    zero = jnp.zeros((2 * GLA_DV, 2 * GLA_DK), F32)
    sts = scan_pair((0, 0), (zero, zero))
    lax.fori_loop(0, n_lat_blocks, lambda i, s: scan_pair((1 + i, n_lat_blocks - i), s), sts)

    if need_ctx:
        finalize(0, rc, 0, yc_ref)

    def finalize_lat(i, carry):
        for j in range(2):
            finalize(1 + 2 * i + j, rl, pl.multiple_of((2 * i + j) * blk, blk), yl_ref)
        return carry

    lax.fori_loop(0, n_lat_blocks // 2, finalize_lat, 0)


def _gla(pl_, pc_, w2, ba, g, need_ctx):
    b, s, _ = pl_.shape
    n_c = pc_.shape[1]
    n_pairs = GLA_HEADS // 2
    w = 2 * GLA_DV

    def stream(n):
        return [pl.BlockSpec((1, n, LANES), lambda bi, p: (bi, 0, AB_QB + p)),
                pl.BlockSpec((1, n, LANES), lambda bi, p: (bi, 0, AB_KB + p)),
                pl.BlockSpec((1, n, w), lambda bi, p: (bi, 0, AB_VB + p)),
                pl.BlockSpec((1, n, LANES), lambda bi, p: (bi, 0, AB_Z)),
                pl.BlockSpec((1, n, w), lambda bi, p: (bi, 0, AB_RB + p))]

    in_specs = stream(s) + stream(n_c) + [
        pl.BlockSpec((LANES, LANES), lambda bi, p: (0, p)),
        pl.BlockSpec((LANES, LANES), lambda bi, p: (0, n_pairs + p)),
        pl.BlockSpec((1, LANES), lambda bi, p: (0, p)),
        pl.BlockSpec((1, LANES), lambda bi, p: (0, n_pairs + p)),
        pl.BlockSpec((1, w), lambda bi, p: (0, p))]
    out_shape = [jax.ShapeDtypeStruct((b, s, GLA_VW), BF16)]
    out_specs = [pl.BlockSpec((1, s, w), lambda bi, p: (bi, 0, p))]
    if need_ctx:
        out_shape.append(jax.ShapeDtypeStruct((b, n_c, GLA_VW), BF16))
        out_specs.append(pl.BlockSpec((1, n_c, w), lambda bi, p: (bi, 0, p)))
    assert n_c == GLA_BLOCK and s % (2 * GLA_BLOCK) == 0
    n_lat_blocks = s // GLA_BLOCK
    t_all = n_c + s
    n_chunks = -(-(t_all // GLA_CHUNK) // 8) * 8
    scratch = [pltpu.VMEM((2, t_all, LANES), BF16),
               pltpu.VMEM((2, t_all, LANES), BF16),
               pltpu.VMEM((2, n_chunks, LANES), F32),
               pltpu.VMEM((1 + n_lat_blocks, w, GLA_BLOCK), BF16),
               pltpu.VMEM((t_all, w), F32)]
    res = pl.pallas_call(
        functools.partial(_gla_kernel, n_lat_blocks=n_lat_blocks, need_ctx=need_ctx),
        out_shape=out_shape,
        grid=(b, n_pairs),
        in_specs=in_specs,
        out_specs=out_specs,
        scratch_shapes=scratch,
        compiler_params=_params(2),
        name="gla",
    )(*([pl_] * 5), *([pc_] * 5), w2, w2, ba, ba, g.reshape(1, GLA_VW))
    return (res[0], res[1]) if need_ctx else (res[0], None)


def _gla_gate_weights(wa2, ba):
    w = jnp.zeros((LANES, 2 * GLA_KW), F32)
    w = w.at[0:GLA_RANK, 0:GLA_KW].set(wa2[0])
    w = w.at[GLA_RANK:2 * GLA_RANK, GLA_KW:].set(wa2[1])
    return w.astype(BF16), ba.reshape(1, 2 * GLA_KW).astype(F32)


def _swa_kernel(q_ref, kp_ref, kn_ref, kx_ref, vp_ref, vn_ref, vx_ref, kc_ref, vc_ref, sink_ref, o_ref):
    n = pl.program_id(1)
    blk = SWA_BLOCK
    n_c = kc_ref.shape[1]
    n_keys = 3 * blk + n_c
    k_lat = jnp.concatenate([kp_ref[0], kn_ref[0], kx_ref[0]], axis=0)
    v_lat = jnp.concatenate([vp_ref[0], vn_ref[0], vx_ref[0]], axis=0)
    kj = lax.broadcasted_iota(jnp.int32, (blk, blk), 0)
    qi = lax.broadcasted_iota(jnp.int32, (blk, blk), 1)
    in_prev = kj >= qi
    in_next = kj <= qi
    first, last = n == 0, n == pl.num_programs(1) - 1
    n_pairs = SWA_KV // 2
    n_units = SWA_QB * n_pairs
    rows = SWA_GROUP * blk
    q_head = [_head_mask(rows, 0), _head_mask(rows, 1)]
    out_a = lax.broadcasted_iota(jnp.int32, (LANES, rows), 0) < HEAD_DIM

    def block_inputs(j):
        k = jnp.concatenate([k_lat[j * blk:(j + 3) * blk], kc_ref[0]], axis=0)
        v = jnp.concatenate([v_lat[j * blk:(j + 3) * blk], vc_ref[0]], axis=0)
        has_prev = jnp.logical_not(first) if j == 0 else True
        has_next = jnp.logical_not(last) if j == SWA_QB - 1 else True
        return k, v, tuple(jnp.concatenate([m] * SWA_GROUP, axis=1) for m in (in_prev & has_prev, in_next & has_next))

    blocks = [block_inputs(j) for j in range(SWA_QB)]

    def unit_inputs(u):
        j, pair = u // n_pairs, u % n_pairs
        tiles = [pair * SWA_GROUP + g for g in range(SWA_GROUP)]
        q = jnp.concatenate([q_ref[0, j * blk:(j + 1) * blk, t * LANES:(t + 1) * LANES] for t in tiles], axis=0)
        vt = blocks[j][1][:, pair * LANES:(pair + 1) * LANES].astype(F32).T.astype(BF16)
        return j, pair, tiles, q, vt

    def scores(u, h, unit):
        j, pair, tiles, q, _ = unit
        k, _, (ok_prev, ok_next) = blocks[j]
        qm = jnp.where(q_head[h], q, jnp.zeros_like(q))
        st = _dot_nt(k[:, pair * LANES:(pair + 1) * LANES], qm)
        st = jnp.concatenate([jnp.where(ok_prev, st[:blk], NEG_INF), st[blk:2 * blk],
                              jnp.where(ok_next, st[2 * blk:3 * blk], NEG_INF), st[3 * blk:]], axis=0)
        sk = jnp.concatenate([jnp.broadcast_to(sink_ref[t, h:h + 1, 0:1], (1, blk)) for t in tiles], axis=1)
        return st, sk

    def finish(unit, st, sk):
        m = jnp.maximum(jnp.max(st, axis=0, keepdims=True), sk)
        p = jnp.exp2(st - m)
        den = jnp.sum(p, axis=0, keepdims=True) + jnp.exp2(sk - m)
        return _dot(unit[4], p.astype(BF16)) / den

    def store(unit, outs):
        j, _, tiles, _, _ = unit
        o = jnp.where(out_a, outs[0], outs[1]).T.astype(BF16)
        for g, t in enumerate(tiles):
            o_ref[0, j * blk:(j + 1) * blk, t * LANES:(t + 1) * LANES] = o[g * blk:(g + 1) * blk]

    units = {0: unit_inputs(0)}
    pending = scores(0, 0, units[0])
    outs = []
    for c in range(2 * n_units):
        u, h = c // 2, c % 2
        nxt = None
        if c + 1 < 2 * n_units:
            un, hn = (c + 1) // 2, (c + 1) % 2
            if un not in units:
                units[un] = unit_inputs(un)
            nxt = scores(un, hn, units[un])
        outs.append(finish(units[u], *pending))
        if h == 1:
            store(units[u], outs)
            outs = []
        pending = nxt


def _swa(pl_, pc_, sink):
    b, s, _ = pl_.shape
    n_c = pc_.shape[1]
    nb = s // SWA_BLOCK
    step = SWA_QB * SWA_BLOCK
    assert s % step == 0
    qw = SWA_HEADS * HEAD_DIM
    kw = SWA_KV * HEAD_DIM
    k_blk, v_blk = (C_K * LANES) // kw, (C_V * LANES) // kw

    def kv(off):
        return [pl.BlockSpec((1, SWA_BLOCK, kw), lambda bi, n: (bi, jnp.maximum(n * SWA_QB - 1, 0), off)),
                pl.BlockSpec((1, step, kw), lambda bi, n: (bi, n, off)),
                pl.BlockSpec((1, SWA_BLOCK, kw), lambda bi, n: (bi, jnp.minimum((n + 1) * SWA_QB, nb - 1), off))]

    in_specs = ([pl.BlockSpec((1, step, qw), lambda bi, n: (bi, n, 0))] + kv(k_blk) + kv(v_blk)
                + [pl.BlockSpec((1, n_c, kw), lambda bi, n: (bi, 0, k_blk)),
                   pl.BlockSpec((1, n_c, kw), lambda bi, n: (bi, 0, v_blk)),
                   _resident(sink.shape)])
    return pl.pallas_call(
        _swa_kernel,
        out_shape=jax.ShapeDtypeStruct((b, s, qw), BF16),
        grid=(b, s // step),
        in_specs=in_specs,
        out_specs=pl.BlockSpec((1, step, qw), lambda bi, n: (bi, n, 0)),
        compiler_params=_params(2),
        name="swa",
    )(pl_, pl_, pl_, pl_, pl_, pl_, pl_, pc_, pc_, sink)


def _swa_pair_heads(a, axis):
    shape = a.shape
    a = a.reshape(shape[:axis] + (SWA_KV // 2, 2, SWA_GROUP, -1) + shape[axis + 1:])
    return jnp.swapaxes(a, axis + 1, axis + 2).reshape(shape)


def _rope_tables(n):
    t = jnp.arange(n)
    row = (t // GRID_W).astype(F32)
    col = (t % GRID_W).astype(F32)
    half = HEAD_DIM // 2
    inv = 1.0 / (ROPE_BASE ** (jnp.arange(0, half, 2, dtype=F32) / half))
    ang = jnp.concatenate([row[:, None] * inv, col[:, None] * inv], axis=-1)
    cos = jnp.repeat(jnp.cos(ang), 2, axis=1)
    sin = jnp.repeat(jnp.sin(ang), 2, axis=1) * jnp.tile(jnp.array([-1.0, 1.0], F32), half)
    reps = ROPE_TILE // HEAD_DIM
    return jnp.tile(cos, (1, reps)), jnp.tile(sin, (1, reps))


def kernel(x, c, ctx, c_ctx, w_mod, b_mod, g_mix_pre, g_mix_post, g_ffn_pre, g_ffn_post, w_out, w_up, conv_w,
           conv_b, w_down, w_in_ab, na_rpb, gla_wa2, gla_ba, gla_g, w_in_c, swa_sink):
    bsz, seq, d = x.shape
    n_ctx = ctx.shape[1]
    depth = w_mod.shape[0]
    tm_l = 512 if seq % 512 == 0 else 256
    tm_c = n_ctx
    tm_o = 1024 if seq % 1024 == 0 else tm_l

    c_all = jnp.zeros((8, d), F32).at[:bsz].set(c).at[bsz].set(c_ctx)
    mods = _modulation(c_all, w_mod, b_mod).reshape(depth, 8, 6, d)
    rope = _rope_tables(seq)

    col_scale_ab = np.ones((IN_AB,), np.float32)
    col_scale_ab[0:NA_WIDTH] = ATTN_SCALE
    col_scale_ab[3 * NA_WIDTH:3 * NA_WIDTH + GLA_KW] = GLA_DK ** -0.5

    n_q = SWA_HEADS * HEAD_DIM
    w_ab = jnp.pad(w_in_ab * col_scale_ab, ((0, 0), (0, 0), (0, IN_AB_PAD - IN_AB))).astype(BF16)
    w_c = jnp.concatenate([_swa_pair_heads(w_in_c[:, :, :n_q], 2) * ATTN_SCALE, w_in_c[:, :, n_q:]],
                          axis=2).astype(BF16)
    w_o = w_out.astype(BF16)
    head_rows = np.arange(n_q).reshape(SWA_KV // 2, 2, SWA_GROUP, HEAD_DIM).swapaxes(1, 2).reshape(-1)
    w_o_swa = jnp.take(w_o[1::2], head_rows, axis=1)
    wu = w_up.astype(BF16)
    wd = w_down.astype(BF16)
    cwb = jnp.concatenate([conv_w, conv_b[:, None]], axis=1)
    cw = jnp.concatenate([cwb[:, :, :D_FF].reshape(depth, 4, N_FF_CHUNKS, FF_CHUNK),
                          cwb[:, :, D_FF:].reshape(depth, 4, N_FF_CHUNKS, FF_CHUNK)], axis=3).transpose(0, 2, 1, 3)

    xl, xc = x, ctx
    for i in range(depth):
        need_ctx = i < depth - 1
        j = i // 2
        mod_l = mods[i, :bsz]
        mod_c = mods[i, bsz:bsz + 1]
        if i % 2 == 0:
            p_l = _proj(xl, mod_l, g_mix_pre[i], w_ab, j, tm_l, 640)
            p_c = _proj(xc, mod_c, g_mix_pre[i], w_ab, j, tm_c, 640)
            w2, ba = _gla_gate_weights(gla_wa2[j], gla_ba[j])
            y_na = _na(p_l, p_c, _na_bias(na_rpb[j]))
            y_gla, yc_gla = _gla(p_l, p_c, w2, ba, gla_g[j], need_ctx)
            ys_l = [y_na, y_gla]
            w_post, l_post = w_o, i
            if need_ctx:
                ys_c = [_ctx_attn(p_c, AB_QA, AB_KA, AB_VA, NA_HEADS // 2, 1), yc_gla]
        else:
            p_l = _proj(xl, mod_l, g_mix_pre[i], w_c, j, tm_l, ROPE_TILE, rope, (C_V * LANES) // ROPE_TILE)
            p_c = _proj(xc, mod_c, g_mix_pre[i], w_c, j, tm_c, ROPE_TILE)
            sink = jnp.broadcast_to(_swa_pair_heads(swa_sink[j] * LOG2E, 0).reshape(-1, 2, 1),
                                    (SWA_HEADS // 2, 2, LANES)).astype(F32)
            ys_l = [_swa(p_l, p_c, sink)]
            w_post, l_post = w_o_swa, j
            if need_ctx:
                ys_c = [_ctx_attn(p_c, C_Q, C_K, C_V, SWA_HEADS // 2, SWA_GROUP, sink)]

        xl = _outproj(ys_l, w_post, l_post, xl, mod_l, g_mix_post[i], tm_o)
        xl = _ffn(xl, mod_l, g_ffn_pre[i], g_ffn_post[i], wu, cw, wd, i, tm_l)
        if need_ctx:
            xc = _outproj(ys_c, w_post, l_post, xc, mod_c, g_mix_post[i], tm_c)
            xc = _ffn(xc, mod_c, g_ffn_pre[i], g_ffn_post[i], wu, cw, wd, i, tm_c)
    return xl
```

```python
import functools

import numpy as np
import jax
import jax.numpy as jnp
from jax import lax
from jax.experimental import pallas as pl
from jax.experimental.pallas import tpu as pltpu

F32 = jnp.float32
BF16 = jnp.bfloat16

LANES = 128
HEAD_DIM = 64
GRID_W = 64
D_MODEL = 1024
EPS = 1e-6
NEG_INF = -1e30
LOG2E = float(np.log2(np.e))
ATTN_SCALE = HEAD_DIM ** -0.5 * LOG2E

NA_HEADS = 8
NA_WIN_H = 8
NA_WIN_W = 16
NA_WIDTH = NA_HEADS * HEAD_DIM
NA_GROUP = 16
GLA_HEADS = 4
GLA_DK = 64
GLA_DV = 128
GLA_KW = GLA_HEADS * GLA_DK
GLA_VW = GLA_HEADS * GLA_DV
GLA_RANK = 16
GLA_TAU = 16.0
GLA_CHUNK = 64
GLA_BLOCK = 256
IN_AB = 3 * NA_WIDTH + 2 * GLA_KW + 2 * GLA_VW + 2 * GLA_RANK
IN_AB_PAD = 3200
SWA_HEADS = 16
SWA_KV = 4
SWA_GROUP = SWA_HEADS // SWA_KV
SWA_BLOCK = 128
SWA_QB = 2
IN_C = (SWA_HEADS + 2 * SWA_KV) * HEAD_DIM
D_FF = 2816
FF_CHUNK = 256
N_FF_CHUNKS = D_FF // FF_CHUNK
assert N_FF_CHUNKS * FF_CHUNK == D_FF
HALO = 16
OUTPROJ_SUB = 256
PROJ_SUB = 256
ROPE_BASE = 10000.0
ROPE_TILE = 256
VMEM_LIMIT = 56 * 1024 * 1024

AB_QA, AB_KA, AB_VA = 0, 4, 8
AB_QB, AB_KB = 12, 14
AB_VB, AB_RB = 8, 10
AB_Z = 24
C_Q, C_K, C_V = 0, 8, 10


def _dot(a, b):
    return jnp.dot(a, b, preferred_element_type=F32)


def _dot_nt(a, b):
    return lax.dot_general(a, b, (((1,), (1,)), ((), ())), preferred_element_type=F32)


def _rms(x, g):
    return x * lax.rsqrt(jnp.mean(x * x, axis=-1, keepdims=True) + EPS) * g


def _silu(x):
    return x / (1.0 + jnp.exp(-x))


def _params(n_grid):
    return pltpu.CompilerParams(dimension_semantics=("arbitrary",) * n_grid,
                                vmem_limit_bytes=VMEM_LIMIT)


def _resident(shape):
    nd = len(shape)
    return pl.BlockSpec(shape, lambda *_: (0,) * nd, pipeline_mode=pl.Buffered(1))


def _layer_block(arr, layer, rows=None, row_block=0):
    shape = list(arr.shape[1:])
    if rows is not None:
        shape[0] = rows
    tail = (0,) * (len(shape) - 1)
    return pl.BlockSpec((None, *shape), lambda *_: (layer, row_block) + tail, pipeline_mode=pl.Buffered(1))


def _head_mask(rows, h):
    lane = lax.broadcasted_iota(jnp.int32, (rows, LANES), 1)
    return (lane < HEAD_DIM) if h == 0 else (lane >= HEAD_DIM)


def _mod_kernel(c_ref, w_ref, b_ref, o_ref):
    s = _silu(c_ref[...]).astype(BF16)
    o_ref[0] = _dot(s, w_ref[0].astype(BF16)) + b_ref[0]


def _modulation(c_all, w_mod, b_mod):
    depth, d, n = w_mod.shape
    tn = 1536
    return pl.pallas_call(
        _mod_kernel,
        out_shape=jax.ShapeDtypeStruct((depth, 8, n), F32),
        grid=(depth, n // tn),
        in_specs=[pl.BlockSpec((8, d), lambda i, j: (0, 0)),
                  pl.BlockSpec((1, d, tn), lambda i, j: (i, 0, j)),
                  pl.BlockSpec((1, 1, tn), lambda i, j: (i, 0, j))],
        out_specs=pl.BlockSpec((1, 8, tn), lambda i, j: (i, 0, j)),
        compiler_params=_params(2),
        name="modulation",
    )(c_all, w_mod, b_mod.reshape(depth, 1, n))


def _rope(acc, cos, sin):
    n = acc.shape[1]
    lane = lax.broadcasted_iota(jnp.int32, acc.shape, 1)
    partner = jnp.where(lane % 2 == 0, pltpu.roll(acc, n - 1, 1), pltpu.roll(acc, 1, 1))
    return acc * cos + partner * sin


def _proj_kernel(*refs, chunk, rope_chunks, shift_idx):
    if rope_chunks:
        x_ref, mod_ref, g_ref, w_ref, cos_ref, sin_ref, o_ref = refs
    else:
        x_ref, mod_ref, g_ref, w_ref, o_ref = refs
    mod = mod_ref[0]
    tm = x_ref.shape[1]
    sub = min(tm, PROJ_SUB)
    for r0 in range(0, tm, sub):
        h = (_rms(x_ref[0, r0:r0 + sub], g_ref[...]) * (1.0 + mod[shift_idx + 1:shift_idx + 2])
             + mod[shift_idx:shift_idx + 1])
        hb = h.astype(BF16)
        for c in range(w_ref.shape[1] // chunk):
            acc = _dot(hb, w_ref[:, c * chunk:(c + 1) * chunk])
            if c < rope_chunks:
                acc = _rope(acc, cos_ref[r0:r0 + sub], sin_ref[r0:r0 + sub])
            o_ref[0, r0:r0 + sub, c * chunk:(c + 1) * chunk] = acc.astype(BF16)


def _proj(x, mod, g, w, layer, tm, chunk, rope=None, rope_chunks=0):
    b, s, d = x.shape
    n = w.shape[2]
    per_batch = mod.shape[0] > 1
    in_specs = [pl.BlockSpec((1, tm, d), lambda bi, i: (bi, i, 0)),
                pl.BlockSpec((1, 6, d), (lambda bi, i: (bi, 0, 0)) if per_batch else (lambda bi, i: (0, 0, 0))),
                _resident((1, d)),
                _layer_block(w, layer)]
    args = [x, mod, g.reshape(1, d), w]
    if rope_chunks:
        in_specs += [pl.BlockSpec((tm, chunk), lambda bi, i: (i, 0))] * 2
        args += list(rope)
    return pl.pallas_call(
        functools.partial(_proj_kernel, chunk=chunk, rope_chunks=rope_chunks, shift_idx=0),
        out_shape=jax.ShapeDtypeStruct((b, s, n), BF16),
        grid=(b, s // tm),
        in_specs=in_specs,
        out_specs=pl.BlockSpec((1, tm, n), lambda bi, i: (bi, i, 0)),
        compiler_params=_params(2),
        name="proj",
    )(*args)


def _outproj_kernel(*refs, n_parts):
    y_refs, w_refs = refs[:n_parts], refs[n_parts:2 * n_parts]
    x_ref, mod_ref, g_ref, o_ref = refs[2 * n_parts:]
    tm = x_ref.shape[1]
    sub = min(tm, OUTPROJ_SUB)
    gate, g = mod_ref[0][2:3], g_ref[...]
    accs = []
    for r0 in range(0, tm, sub):
        acc = _dot(y_refs[0][0, r0:r0 + sub], w_refs[0][...])
        for yr, wr in zip(y_refs[1:], w_refs[1:]):
            acc = acc + _dot(yr[0, r0:r0 + sub], wr[...])
        accs.append(acc)
    for k, acc in enumerate(accs):
        o_ref[0, k * sub:(k + 1) * sub] = x_ref[0, k * sub:(k + 1) * sub] + gate * _rms(acc, g)


def _outproj(ys, w, layer, x, mod, g, tm):
    b, s, d = x.shape
    per_batch = mod.shape[0] > 1
    in_specs = [pl.BlockSpec((1, tm, y.shape[2]), lambda bi, i: (bi, i, 0)) for y in ys]
    in_specs += [_layer_block(w, layer, y.shape[2], k) for k, y in enumerate(ys)]
    in_specs += [pl.BlockSpec((1, tm, d), lambda bi, i: (bi, i, 0)),
                 pl.BlockSpec((1, 6, d), (lambda bi, i: (bi, 0, 0)) if per_batch else (lambda bi, i: (0, 0, 0))),
                 _resident((1, d))]
    return pl.pallas_call(
        functools.partial(_outproj_kernel, n_parts=len(ys)),
        out_shape=jax.ShapeDtypeStruct((b, s, d), F32),
        grid=(b, s // tm),
        in_specs=in_specs,
        out_specs=pl.BlockSpec((1, tm, d), lambda bi, i: (bi, i, 0)),
        compiler_params=_params(2),
        name="outproj",
    )(*ys, *([w] * len(ys)), x, mod, g.reshape(1, d))


def _ffn_kernel(x_ref, xp_ref, xn_ref, mod_ref, gpre_ref, gpost_ref, wu_ref, cw_ref, wd_ref, o_ref,
                h_ref, u_ref, a_ref, *, tm):
    i = pl.program_id(1)
    mod = mod_ref[0]
    gpre = gpre_ref[...]

    def norm_mod(x):
        return _rms(x, gpre) * (1.0 + mod[4:5]) + mod[3:4]

    keep_prev = (i > 0).astype(F32)
    keep_next = (i < pl.num_programs(1) - 1).astype(F32)
    rows = tm + 2 * HALO
    half = tm // 2
    mid = HALO + half
    n_u = u_ref.shape[0]

    def up(c, r0=0, r1=rows):
        h = h_ref[r0:r1]
        for part in range(2):
            col = part * D_FF + c * FF_CHUNK
            u_ref[c % n_u, r0:r1, part * FF_CHUNK:(part + 1) * FF_CHUNK] = _dot(h, wu_ref[:, col:col + FF_CHUNK])

    h_ref[0:HALO] = (norm_mod(xp_ref[0]) * keep_prev).astype(BF16)
    h_ref[HALO:mid] = norm_mod(x_ref[0, :half]).astype(BF16)
    up(0, 0, mid)
    h_ref[mid:HALO + tm] = norm_mod(x_ref[0, half:]).astype(BF16)
    h_ref[HALO + tm:] = (norm_mod(xn_ref[0]) * keep_next).astype(BF16)
    up(0, mid, rows)

    def act(c):
        u = u_ref[c % n_u]
        cw = cw_ref[c]
        y = (pltpu.roll(u, 1, 0)[HALO:HALO + tm] * cw[0:1]
             + u[HALO:HALO + tm] * cw[1:2]
             + pltpu.roll(u, rows - 1, 0)[HALO:HALO + tm] * cw[2:3]
             + cw[3:4])
        a_ref[:, c * FF_CHUNK:(c + 1) * FF_CHUNK] = (_silu(y[:, :FF_CHUNK]) * y[:, FF_CHUNK:]).astype(BF16)

    for c in range(1, min(n_u - 1, N_FF_CHUNKS)):
        up(c)
    for c in range(N_FF_CHUNKS):
        if c + n_u - 1 < N_FF_CHUNKS:
            up(c + n_u - 1)
        act(c)
    accs = [_dot(a_ref[r0:r0 + half], wd_ref[...]) for r0 in (0, half)]
    for r0, acc in zip((0, half), accs):
        o_ref[0, r0:r0 + half] = x_ref[0, r0:r0 + half] + mod[5:6] * _rms(acc, gpost_ref[...])


def _ffn(x, mod, g_pre, g_post, wu, cw, wd, layer, tm):
    b, s, d = x.shape
    per_batch = mod.shape[0] > 1
    hb = tm // HALO
    n_halo = s // HALO
    return pl.pallas_call(
        functools.partial(_ffn_kernel, tm=tm),
        out_shape=jax.ShapeDtypeStruct((b, s, d), F32),
        grid=(b, s // tm),
        in_specs=[pl.BlockSpec((1, tm, d), lambda bi, i: (bi, i, 0)),
                  pl.BlockSpec((1, HALO, d), lambda bi, i: (bi, jnp.maximum(i * hb - 1, 0), 0)),
                  pl.BlockSpec((1, HALO, d), lambda bi, i: (bi, jnp.minimum((i + 1) * hb, n_halo - 1), 0)),
                  pl.BlockSpec((1, 6, d), (lambda bi, i: (bi, 0, 0)) if per_batch else (lambda bi, i: (0, 0, 0))),
                  _resident((1, d)), _resident((1, d)),
                  _layer_block(wu, layer), _layer_block(cw, layer), _layer_block(wd, layer)],
        out_specs=pl.BlockSpec((1, tm, d), lambda bi, i: (bi, i, 0)),
        scratch_shapes=[pltpu.VMEM((tm + 2 * HALO, d), BF16),
                        pltpu.VMEM((3, tm + 2 * HALO, 2 * FF_CHUNK), F32),
                        pltpu.VMEM((tm, D_FF), BF16)],
        compiler_params=_params(2),
        name="ffn",
    )(x, x, x, mod, g_pre.reshape(1, d), g_post.reshape(1, d), wu, cw, wd)


def _na_kernel(q_ref, k_ref, v_ref, kc_ref, vc_ref, bias_ref, o_ref, *, rows):
    kc = kc_ref[0]
    vc = vc_ref[0]
    win = NA_WIN_H * GRID_W
    mask_a = _head_mask(GRID_W, 0)
    vc_ext = jnp.concatenate([vc, jnp.ones_like(vc)], axis=1)
    ones_w = jnp.ones((win, LANES), BF16)

    def group(gi, carry):
        starts, scores, probs = [], [], []
        for i in range(NA_GROUP):
            r = gi * NA_GROUP + i
            r0 = jnp.clip(r - NA_WIN_H // 2, 0, rows - NA_WIN_H)
            q0 = pl.multiple_of(r * GRID_W, GRID_W)
            k0 = pl.multiple_of(r0 * GRID_W, GRID_W)
            q = q_ref[0, pl.ds(q0, GRID_W), :]
            zero = jnp.zeros_like(q)
            q2 = jnp.concatenate([jnp.where(mask_a, q, zero), jnp.where(mask_a, zero, q)], axis=0)
            d0 = r0 - r + NA_WIN_H - 1
            bias = jnp.concatenate(
                [jnp.concatenate([bias_ref[0, hh, d0 + 2 * a] for a in range(NA_WIN_H // 2)], axis=1)
                 for hh in range(2)], axis=0)
            s = _dot_nt(q2, k_ref[0, pl.ds(k0, win), :]) + bias
            sc = _dot_nt(q2, kc)
            starts.append((q0, k0))
            scores.append((s, sc))
        for s, sc in scores:
            m = jnp.maximum(jnp.max(s, axis=-1, keepdims=True), jnp.max(sc, axis=-1, keepdims=True))
            probs.append((jnp.exp2(s - m).astype(BF16), jnp.exp2(sc - m).astype(BF16)))
        for (q0, k0), (p, pc) in zip(starts, probs):
            v_ext = jnp.concatenate([v_ref[0, pl.ds(k0, win), :], ones_w], axis=1)
            o_ext = _dot(p, v_ext) + _dot(pc, vc_ext)
            o2 = o_ext[:, :LANES] / o_ext[:, LANES:]
            o_ref[0, pl.ds(q0, GRID_W), :] = jnp.where(mask_a, o2[:GRID_W], o2[GRID_W:]).astype(BF16)
        return carry

    lax.fori_loop(0, rows // NA_GROUP, group, 0)


def _na(pl_, pc_, bias):
    b, s, _ = pl_.shape
    n_ctx = pc_.shape[1]
    n_pairs = NA_HEADS // 2
    return pl.pallas_call(
        functools.partial(_na_kernel, rows=s // GRID_W),
        out_shape=jax.ShapeDtypeStruct((b, s, NA_WIDTH), BF16),
        grid=(b, n_pairs),
        in_specs=[pl.BlockSpec((1, s, LANES), lambda bi, p: (bi, 0, AB_QA + p)),
                  pl.BlockSpec((1, s, LANES), lambda bi, p: (bi, 0, AB_KA + p)),
                  pl.BlockSpec((1, s, LANES), lambda bi, p: (bi, 0, AB_VA + p)),
                  pl.BlockSpec((1, n_ctx, LANES), lambda bi, p: (bi, 0, AB_KA + p)),
                  pl.BlockSpec((1, n_ctx, LANES), lambda bi, p: (bi, 0, AB_VA + p)),
                  pl.BlockSpec((1,) + bias.shape[1:], lambda bi, p: (p, 0, 0, 0, 0))],
        out_specs=pl.BlockSpec((1, s, LANES), lambda bi, p: (bi, 0, p)),
        compiler_params=_params(2),
        name="na",
    )(pl_, pl_, pl_, pc_, pc_, bias)


def _na_bias(rpb):
    h = rpb.shape[0]
    cidx = np.arange(GRID_W)
    c_start = np.clip(cidx - NA_WIN_W // 2, 0, GRID_W - NA_WIN_W)
    col_ok = (cidx[None, :] >= c_start[:, None]) & (cidx[None, :] < c_start[:, None] + NA_WIN_W)
    lo, hi = GRID_W - NA_WIN_W, GRID_W - NA_WIN_W
    rp = rpb.astype(F32) * LOG2E
    ext = jnp.concatenate([jnp.repeat(rp[..., :1], lo, axis=-1), rp, jnp.repeat(rp[..., -1:], hi, axis=-1)], axis=-1)
    bm = jnp.stack([ext[..., GRID_W - 1 - q:2 * GRID_W - 1 - q] for q in range(GRID_W)], axis=2)
    bm = jnp.where(col_ok[None, None], bm, NEG_INF)
    pairs = jnp.concatenate([bm[:, :-1], bm[:, 1:]], axis=-1)
    return pairs.reshape((h // 2, 2) + pairs.shape[1:])


def _ctx_attn_kernel(*refs, n_q_tiles, group, has_sink):
    if has_sink:
        q_ref, k_ref, v_ref, sink_ref, o_ref = refs
    else:
        q_ref, k_ref, v_ref, o_ref = refs
    n = q_ref.shape[1]
    mask_a = _head_mask(n, 0)
    scores, probs = [], []
    for t in range(n_q_tiles):
        q = q_ref[0, :, t * LANES:(t + 1) * LANES]
        zero = jnp.zeros_like(q)
        q2 = jnp.concatenate([jnp.where(mask_a, q, zero), jnp.where(mask_a, zero, q)], axis=0)
        kt = t // group
        scores.append(_dot_nt(q2, k_ref[0, :, kt * LANES:(kt + 1) * LANES]))
    for t, s in enumerate(scores):
        m = jnp.max(s, axis=-1, keepdims=True)
        if has_sink:
            sk = jnp.concatenate([jnp.broadcast_to(sink_ref[t, h:h + 1, 0:1], (n, 1)) for h in range(2)], axis=0)
            m = jnp.maximum(m, sk)
        p = jnp.exp2(s - m)
        den = jnp.sum(p, axis=-1, keepdims=True)
        if has_sink:
            den = den + jnp.exp2(sk - m)
        probs.append((p.astype(BF16), den))
    for t, (p, den) in enumerate(probs):
        kt = t // group
        o2 = _dot(p, v_ref[0, :, kt * LANES:(kt + 1) * LANES]) / den
        o_ref[0, :, t * LANES:(t + 1) * LANES] = jnp.where(mask_a, o2[:n], o2[n:]).astype(BF16)


def _ctx_attn(pc_, q_off, k_off, v_off, n_q_tiles, group, sink=None):
    b, n, _ = pc_.shape
    n_k_tiles = n_q_tiles // group
    qw, kw = n_q_tiles * LANES, n_k_tiles * LANES
    assert q_off % n_q_tiles == 0 and k_off % n_k_tiles == 0 and v_off % n_k_tiles == 0
    in_specs = [pl.BlockSpec((1, n, qw), lambda bi: (bi, 0, q_off // n_q_tiles)),
                pl.BlockSpec((1, n, kw), lambda bi: (bi, 0, k_off // n_k_tiles)),
                pl.BlockSpec((1, n, kw), lambda bi: (bi, 0, v_off // n_k_tiles))]
    args = [pc_, pc_, pc_]
    if sink is not None:
        in_specs.append(_resident(sink.shape))
        args.append(sink)
    return pl.pallas_call(
        functools.partial(_ctx_attn_kernel, n_q_tiles=n_q_tiles, group=group, has_sink=sink is not None),
        out_shape=jax.ShapeDtypeStruct((b, n, qw), BF16),
        grid=(b,),
        in_specs=in_specs,
        out_specs=pl.BlockSpec((1, n, qw), lambda bi: (bi, 0, 0)),
        compiler_params=_params(1),
        name="ctx_attn",
    )(*args)


def _gla_kernel(*refs, n_lat_blocks, need_ctx):
    (ql, kl, vl, zl, rl, qc, kc, vc, zc, rc, w2f, w2b, baf, bab, g_ref) = refs[:15]
    n_out = 2 if need_ctx else 1
    yl_ref = refs[15]
    yc_ref = refs[16] if need_ctx else None
    qt_s, kd_s, dec_s, vt_s, o_s = refs[15 + n_out:]
    blk, ck = GLA_BLOCK, GLA_CHUNK
    cpb = blk // ck
    shift = ck.bit_length() - 1
    ri = lax.broadcasted_iota(jnp.int32, (blk, blk), 0)
    ci = lax.broadcasted_iota(jnp.int32, (blk, blk), 1)
    same = lax.shift_right_logical(ri, shift) == lax.shift_right_logical(ci, shift)
    att_mask = [same & (ci <= ri), same & (ci >= ri)]
    as_bf16 = lambda m: jnp.where(m, 1.0, 0.0).astype(BF16)
    cum = [jnp.concatenate([as_bf16(att_mask[d]), as_bf16(same)], axis=0) for d in range(2)]
    head = [_head_mask(blk, 0), _head_mask(blk, 1)]
    head_ck = [_head_mask(ck, 0), _head_mask(ck, 1)]
    tok_lo = lax.broadcasted_iota(jnp.int32, (2 * GLA_DV, LANES), 1) < ck
    w2 = jnp.concatenate([w2f[...], w2b[...]], axis=1)
    ba = jnp.concatenate([baf[...], bab[...]], axis=1)
    g = g_ref[...]
    lat4, ctx4 = (ql, kl, vl, zl), (qc, kc, vc, zc)

    def phase1(blocks):
        gates = []
        for (_, _, _, z_ref), t_in, _ in blocks:
            x = _dot(z_ref[0, pl.ds(t_in, blk), :], w2) + ba
            la = (jnp.minimum(x, 0.0) - jnp.log(1.0 + jnp.exp(-jnp.abs(x)))) * (1.0 / GLA_TAU)
            hi = la.astype(BF16)
            gates.append((hi, (la - hi.astype(F32)).astype(BF16)))
        cums = []
        for hi, lo in gates:
            per_dir = []
            for d in range(2):
                c2 = _dot(cum[d], jnp.concatenate([hi[:, d * LANES:(d + 1) * LANES],
                                                   lo[:, d * LANES:(d + 1) * LANES]], axis=1))
                per_dir.append((c2[:blk, :LANES] + c2[:blk, LANES:], c2[blk:, :LANES] + c2[blk:, LANES:]))
            cums.append(per_dir)
        scores = []
        for ((q_ref, k_ref, _, _), t_in, bo), per_dir in zip(blocks, cums):
            q = q_ref[0, pl.ds(t_in, blk), :].astype(F32)
            k = k_ref[0, pl.ds(t_in, blk), :].astype(F32)
            t_out = pl.multiple_of(bo * blk, blk)
            items = []
            for d, (b, tot) in enumerate(per_dir):
                qt = q * jnp.exp(b)
                kt = (k * jnp.exp(-b)).astype(BF16)
                qt_s[d, pl.ds(t_out, blk), :] = qt.astype(BF16)
                kd_s[d, pl.ds(t_out, blk), :] = (k * jnp.exp(tot - b)).astype(BF16)
                for j in range(cpb):
                    dec_s[d, pl.ds(bo * cpb + j, 1), :] = jnp.exp(tot[j * ck:j * ck + 1])
                for h in range(2):
                    items.append((d, h, _dot_nt(jnp.where(head[h], qt, 0.0).astype(BF16), kt)))
            scores.append(items)
        for ((_, _, v_ref, _), t_in, bo), items in zip(blocks, scores):
            v = v_ref[0, pl.ds(t_in, blk), :]
            oi = [None, None]
            for d, h, s in items:
                c = _dot(jnp.where(att_mask[d], s, 0.0).astype(BF16), v[:, h * GLA_DV:(h + 1) * GLA_DV])
                oi[h] = c if oi[h] is None else oi[h] + c
            o_s[pl.ds(pl.multiple_of(bo * blk, blk), blk), :] = jnp.concatenate(oi, axis=1)
            vt_s[bo] = v.astype(F32).T.astype(BF16)

    orders = [list(range(cpb)), list(range(cpb - 1, -1, -1))]

    def scan_pairs(pairs, sts):
        upds = []
        for bos in pairs:
            upd = [{}, {}]
            for d in range(2):
                vt = vt_s[bos[d]]
                kd = kd_s[d, pl.ds(pl.multiple_of(bos[d] * blk, blk), blk), :]
                for c in orders[d]:
                    half = c // 2
                    vt_c = vt[:, half * LANES:(half + 1) * LANES]
                    vt_c = jnp.where(tok_lo if c % 2 == 0 else ~tok_lo, vt_c, jnp.zeros_like(vt_c))
                    upd[d][c] = _dot(vt_c, kd[half * LANES:(half + 1) * LANES])
            upds.append(upd)
        sts = list(sts)
        for bos, upd in zip(pairs, upds):
            for step in range(cpb):
                for d in range(2):
                    c = orders[d][step]
                    tc = pl.multiple_of(bos[d] * blk + c * ck, ck)
                    qt = qt_s[d, pl.ds(tc, ck), :]
                    stb = sts[d].astype(BF16)
                    outs = [_dot_nt(jnp.where(head_ck[h], qt, jnp.zeros_like(qt)),
                                    stb[h * GLA_DV:(h + 1) * GLA_DV]) for h in range(2)]
                    o_s[pl.ds(tc, ck), :] += jnp.concatenate(outs, axis=1)
                    sts[d] = dec_s[d, pl.ds(bos[d] * cpb + c, 1), :] * sts[d] + upd[d][c]
        return tuple(sts)

    def finalize(bo, r_ref, t_in, y_ref):
        o = o_s[pl.ds(pl.multiple_of(bo * blk, blk), blk), :]
        r = r_ref[0, pl.ds(t_in, blk), :].astype(F32)
        on = jnp.concatenate(
            [o[:, h * GLA_DV:(h + 1) * GLA_DV]
             * lax.rsqrt(jnp.mean(o[:, h * GLA_DV:(h + 1) * GLA_DV] ** 2, axis=-1, keepdims=True) + EPS)
             for h in range(2)], axis=1)
        y_ref[0, pl.ds(t_in, blk), :] = (on * g * _silu(r)).astype(BF16)

    phase1([(ctx4, 0, 0)])

    def phase1_lat(i, carry):
        phase1([(lat4, pl.multiple_of((2 * i + j) * blk, blk), 1 + 2 * i + j) for j in range(2)])
        return carry

    lax.fori_loop(0, n_lat_blocks // 2, phase1_lat, 0)

    zero = jnp.zeros((2 * GLA_DV, 2 * GLA_DK), F32)
    sts = scan_pairs([(0, 0)], (zero, zero))
    lax.fori_loop(
        0, n_lat_blocks // 2,
        lambda i, s: scan_pairs([(1 + 2 * i + j, n_lat_blocks - 2 * i - j) for j in range(2)], s), sts)

    if need_ctx:
        finalize(0, rc, 0, yc_ref)

    def finalize_lat(i, carry):
        for j in range(2):
            finalize(1 + 2 * i + j, rl, pl.multiple_of((2 * i + j) * blk, blk), yl_ref)
        return carry

    lax.fori_loop(0, n_lat_blocks // 2, finalize_lat, 0)


def _gla(pl_, pc_, w2, ba, g, need_ctx):
    b, s, _ = pl_.shape
    n_c = pc_.shape[1]
    n_pairs = GLA_HEADS // 2
    w = 2 * GLA_DV

    def stream(n):
        return [pl.BlockSpec((1, n, LANES), lambda bi, p: (bi, 0, AB_QB + p)),
                pl.BlockSpec((1, n, LANES), lambda bi, p: (bi, 0, AB_KB + p)),
                pl.BlockSpec((1, n, w), lambda bi, p: (bi, 0, AB_VB + p)),
                pl.BlockSpec((1, n, LANES), lambda bi, p: (bi, 0, AB_Z)),
                pl.BlockSpec((1, n, w), lambda bi, p: (bi, 0, AB_RB + p))]

    in_specs = stream(s) + stream(n_c) + [
        pl.BlockSpec((LANES, LANES), lambda bi, p: (0, p)),
        pl.BlockSpec((LANES, LANES), lambda bi, p: (0, n_pairs + p)),
        pl.BlockSpec((1, LANES), lambda bi, p: (0, p)),
        pl.BlockSpec((1, LANES), lambda bi, p: (0, n_pairs + p)),
        pl.BlockSpec((1, w), lambda bi, p: (0, p))]
    out_shape = [jax.ShapeDtypeStruct((b, s, GLA_VW), BF16)]
    out_specs = [pl.BlockSpec((1, s, w), lambda bi, p: (bi, 0, p))]
    if need_ctx:
        out_shape.append(jax.ShapeDtypeStruct((b, n_c, GLA_VW), BF16))
        out_specs.append(pl.BlockSpec((1, n_c, w), lambda bi, p: (bi, 0, p)))
    assert n_c == GLA_BLOCK and s % (2 * GLA_BLOCK) == 0
    n_lat_blocks = s // GLA_BLOCK
    t_all = n_c + s
    n_chunks = -(-(t_all // GLA_CHUNK) // 8) * 8
    scratch = [pltpu.VMEM((2, t_all, LANES), BF16),
               pltpu.VMEM((2, t_all, LANES), BF16),
               pltpu.VMEM((2, n_chunks, LANES), F32),
               pltpu.VMEM((1 + n_lat_blocks, w, GLA_BLOCK), BF16),
               pltpu.VMEM((t_all, w), F32)]
    res = pl.pallas_call(
        functools.partial(_gla_kernel, n_lat_blocks=n_lat_blocks, need_ctx=need_ctx),
        out_shape=out_shape,
        grid=(b, n_pairs),
        in_specs=in_specs,
        out_specs=out_specs,
        scratch_shapes=scratch,
        compiler_params=_params(2),
        name="gla",
    )(*([pl_] * 5), *([pc_] * 5), w2, w2, ba, ba, g.reshape(1, GLA_VW))
    return (res[0], res[1]) if need_ctx else (res[0], None)


def _gla_gate_weights(wa2, ba):
    w = jnp.zeros((LANES, 2 * GLA_KW), F32)
    w = w.at[0:GLA_RANK, 0:GLA_KW].set(wa2[0])
    w = w.at[GLA_RANK:2 * GLA_RANK, GLA_KW:].set(wa2[1])
    return w.astype(BF16), ba.reshape(1, 2 * GLA_KW).astype(F32)


def _swa_kernel(q_ref, kp_ref, kn_ref, kx_ref, vp_ref, vn_ref, vx_ref, kc_ref, vc_ref, sink_ref, o_ref):
    n = pl.program_id(1)
    blk = SWA_BLOCK
    n_c = kc_ref.shape[1]
    n_keys = 3 * blk + n_c
    k_lat = jnp.concatenate([kp_ref[0], kn_ref[0], kx_ref[0]], axis=0)
    v_lat = jnp.concatenate([vp_ref[0], vn_ref[0], vx_ref[0]], axis=0)
    kj = lax.broadcasted_iota(jnp.int32, (blk, blk), 0)
    qi = lax.broadcasted_iota(jnp.int32, (blk, blk), 1)
    in_prev = kj >= qi
    in_next = kj <= qi
    first, last = n == 0, n == pl.num_programs(1) - 1
    n_pairs = SWA_KV // 2
    n_units = SWA_QB * n_pairs
    rows = SWA_GROUP * blk
    q_head = [_head_mask(rows, 0), _head_mask(rows, 1)]
    out_a = lax.broadcasted_iota(jnp.int32, (LANES, rows), 0) < HEAD_DIM

    def block_inputs(j):
        k = jnp.concatenate([k_lat[j * blk:(j + 3) * blk], kc_ref[0]], axis=0)
        v = jnp.concatenate([v_lat[j * blk:(j + 3) * blk], vc_ref[0]], axis=0)
        has_prev = jnp.logical_not(first) if j == 0 else True
        has_next = jnp.logical_not(last) if j == SWA_QB - 1 else True
        return k, v, tuple(jnp.concatenate([m] * SWA_GROUP, axis=1) for m in (in_prev & has_prev, in_next & has_next))

    blocks = [block_inputs(j) for j in range(SWA_QB)]

    def unit_inputs(u):
        j, pair = u // n_pairs, u % n_pairs
        tiles = [pair * SWA_GROUP + g for g in range(SWA_GROUP)]
        q = jnp.concatenate([q_ref[0, j * blk:(j + 1) * blk, t * LANES:(t + 1) * LANES] for t in tiles], axis=0)
        vt = blocks[j][1][:, pair * LANES:(pair + 1) * LANES].astype(F32).T.astype(BF16)
        return j, pair, tiles, q, vt

    def scores(u, h, unit):
        j, pair, tiles, q, _ = unit
        k, _, (ok_prev, ok_next) = blocks[j]
        qm = jnp.where(q_head[h], q, jnp.zeros_like(q))
        st = _dot_nt(k[:, pair * LANES:(pair + 1) * LANES], qm)
        st = jnp.concatenate([jnp.where(ok_prev, st[:blk], NEG_INF), st[blk:2 * blk],
                              jnp.where(ok_next, st[2 * blk:3 * blk], NEG_INF), st[3 * blk:]], axis=0)
        sk = jnp.concatenate([jnp.broadcast_to(sink_ref[t, h:h + 1, 0:1], (1, blk)) for t in tiles], axis=1)
        return st, sk

    def finish(unit, st, sk):
        m = jnp.maximum(jnp.max(st, axis=0, keepdims=True), sk)
        p = jnp.exp2(st - m)
        den = jnp.sum(p, axis=0, keepdims=True) + jnp.exp2(sk - m)
        return _dot(unit[4], p.astype(BF16)) / den

    def store(unit, outs):
        j, _, tiles, _, _ = unit
        o = jnp.where(out_a, outs[0], outs[1]).T.astype(BF16)
        for g, t in enumerate(tiles):
            o_ref[0, j * blk:(j + 1) * blk, t * LANES:(t + 1) * LANES] = o[g * blk:(g + 1) * blk]

    units = {0: unit_inputs(0)}
    pending = scores(0, 0, units[0])
    outs = []
    for c in range(2 * n_units):
        u, h = c // 2, c % 2
        nxt = None
        if c + 1 < 2 * n_units:
            un, hn = (c + 1) // 2, (c + 1) % 2
            if un not in units:
                units[un] = unit_inputs(un)
            nxt = scores(un, hn, units[un])
        outs.append(finish(units[u], *pending))
        if h == 1:
            store(units[u], outs)
            outs = []
        pending = nxt


def _swa(pl_, pc_, sink):
    b, s, _ = pl_.shape
    n_c = pc_.shape[1]
    nb = s // SWA_BLOCK
    step = SWA_QB * SWA_BLOCK
    assert s % step == 0
    qw = SWA_HEADS * HEAD_DIM
    kw = SWA_KV * HEAD_DIM
    k_blk, v_blk = (C_K * LANES) // kw, (C_V * LANES) // kw

    def kv(off):
        return [pl.BlockSpec((1, SWA_BLOCK, kw), lambda bi, n: (bi, jnp.maximum(n * SWA_QB - 1, 0), off)),
                pl.BlockSpec((1, step, kw), lambda bi, n: (bi, n, off)),
                pl.BlockSpec((1, SWA_BLOCK, kw), lambda bi, n: (bi, jnp.minimum((n + 1) * SWA_QB, nb - 1), off))]

    in_specs = ([pl.BlockSpec((1, step, qw), lambda bi, n: (bi, n, 0))] + kv(k_blk) + kv(v_blk)
                + [pl.BlockSpec((1, n_c, kw), lambda bi, n: (bi, 0, k_blk)),
                   pl.BlockSpec((1, n_c, kw), lambda bi, n: (bi, 0, v_blk)),
                   _resident(sink.shape)])
    return pl.pallas_call(
        _swa_kernel,
        out_shape=jax.ShapeDtypeStruct((b, s, qw), BF16),
        grid=(b, s // step),
        in_specs=in_specs,
        out_specs=pl.BlockSpec((1, step, qw), lambda bi, n: (bi, n, 0)),
        compiler_params=_params(2),
        name="swa",
    )(pl_, pl_, pl_, pl_, pl_, pl_, pl_, pc_, pc_, sink)


def _swa_pair_heads(a, axis):
    shape = a.shape
    a = a.reshape(shape[:axis] + (SWA_KV // 2, 2, SWA_GROUP, -1) + shape[axis + 1:])
    return jnp.swapaxes(a, axis + 1, axis + 2).reshape(shape)


def _rope_tables(n):
    t = jnp.arange(n)
    row = (t // GRID_W).astype(F32)
    col = (t % GRID_W).astype(F32)
    half = HEAD_DIM // 2
    inv = 1.0 / (ROPE_BASE ** (jnp.arange(0, half, 2, dtype=F32) / half))
    ang = jnp.concatenate([row[:, None] * inv, col[:, None] * inv], axis=-1)
    cos = jnp.repeat(jnp.cos(ang), 2, axis=1)
    sin = jnp.repeat(jnp.sin(ang), 2, axis=1) * jnp.tile(jnp.array([-1.0, 1.0], F32), half)
    reps = ROPE_TILE // HEAD_DIM
    return jnp.tile(cos, (1, reps)), jnp.tile(sin, (1, reps))


def kernel(x, c, ctx, c_ctx, w_mod, b_mod, g_mix_pre, g_mix_post, g_ffn_pre, g_ffn_post, w_out, w_up, conv_w,
           conv_b, w_down, w_in_ab, na_rpb, gla_wa2, gla_ba, gla_g, w_in_c, swa_sink):
    bsz, seq, d = x.shape
    n_ctx = ctx.shape[1]
    depth = w_mod.shape[0]
    tm_l = 512 if seq % 512 == 0 else 256
    tm_c = n_ctx
    tm_o = 1024 if seq % 1024 == 0 else tm_l

    c_all = jnp.zeros((8, d), F32).at[:bsz].set(c).at[bsz].set(c_ctx)
    mods = _modulation(c_all, w_mod, b_mod).reshape(depth, 8, 6, d)
    rope = _rope_tables(seq)

    col_scale_ab = np.ones((IN_AB,), np.float32)
    col_scale_ab[0:NA_WIDTH] = ATTN_SCALE
    col_scale_ab[3 * NA_WIDTH:3 * NA_WIDTH + GLA_KW] = GLA_DK ** -0.5

    n_q = SWA_HEADS * HEAD_DIM
    w_ab = jnp.pad(w_in_ab * col_scale_ab, ((0, 0), (0, 0), (0, IN_AB_PAD - IN_AB))).astype(BF16)
    w_c = jnp.concatenate([_swa_pair_heads(w_in_c[:, :, :n_q], 2) * ATTN_SCALE, w_in_c[:, :, n_q:]],
                          axis=2).astype(BF16)
    w_o = w_out.astype(BF16)
    head_rows = np.arange(n_q).reshape(SWA_KV // 2, 2, SWA_GROUP, HEAD_DIM).swapaxes(1, 2).reshape(-1)
    w_o_swa = jnp.take(w_o[1::2], head_rows, axis=1)
    wu = w_up.astype(BF16)
    wd = w_down.astype(BF16)
    cwb = jnp.concatenate([conv_w, conv_b[:, None]], axis=1)
    cw = jnp.concatenate([cwb[:, :, :D_FF].reshape(depth, 4, N_FF_CHUNKS, FF_CHUNK),
                          cwb[:, :, D_FF:].reshape(depth, 4, N_FF_CHUNKS, FF_CHUNK)], axis=3).transpose(0, 2, 1, 3)

    xl, xc = x, ctx
    for i in range(depth):
        need_ctx = i < depth - 1
        j = i // 2
        mod_l = mods[i, :bsz]
        mod_c = mods[i, bsz:bsz + 1]
        if i % 2 == 0:
            p_l = _proj(xl, mod_l, g_mix_pre[i], w_ab, j, tm_l, 640)
            p_c = _proj(xc, mod_c, g_mix_pre[i], w_ab, j, tm_c, 640)
            w2, ba = _gla_gate_weights(gla_wa2[j], gla_ba[j])
            y_na = _na(p_l, p_c, _na_bias(na_rpb[j]))
            y_gla, yc_gla = _gla(p_l, p_c, w2, ba, gla_g[j], need_ctx)
            ys_l = [y_na, y_gla]
            w_post, l_post = w_o, i
            if need_ctx:
                ys_c = [_ctx_attn(p_c, AB_QA, AB_KA, AB_VA, NA_HEADS // 2, 1), yc_gla]
        else:
            p_l = _proj(xl, mod_l, g_mix_pre[i], w_c, j, tm_l, ROPE_TILE, rope, (C_V * LANES) // ROPE_TILE)
            p_c = _proj(xc, mod_c, g_mix_pre[i], w_c, j, tm_c, ROPE_TILE)
            sink = jnp.broadcast_to(_swa_pair_heads(swa_sink[j] * LOG2E, 0).reshape(-1, 2, 1),
                                    (SWA_HEADS // 2, 2, LANES)).astype(F32)
            ys_l = [_swa(p_l, p_c, sink)]
            w_post, l_post = w_o_swa, j
            if need_ctx:
                ys_c = [_ctx_attn(p_c, C_Q, C_K, C_V, SWA_HEADS // 2, SWA_GROUP, sink)]

        xl = _outproj(ys_l, w_post, l_post, xl, mod_l, g_mix_post[i], tm_o)
        xl = _ffn(xl, mod_l, g_ffn_pre[i], g_ffn_post[i], wu, cw, wd, i, tm_l)
        if need_ctx:
            xc = _outproj(ys_c, w_post, l_post, xc, mod_c, g_mix_post[i], tm_c)
            xc = _ffn(xc, mod_c, g_ffn_pre[i], g_ffn_post[i], wu, cw, wd, i, tm_c)
    return xl
```

```python
import functools
import math

import numpy as np
import jax
import jax.numpy as jnp
from jax import lax
from jax.experimental import pallas as pl
from jax.experimental.pallas import tpu as pltpu

F32 = jnp.float32
BF16 = jnp.bfloat16

LANES = 128
HEAD_DIM = 64
GRID_W = 64
D_MODEL = 1024
EPS = 1e-6
NEG_INF = -1e30
LOG2E = float(np.log2(np.e))
ATTN_SCALE = HEAD_DIM ** -0.5 * LOG2E

NA_HEADS = 8
NA_WIN_H = 8
NA_WIN_W = 16
NA_WIDTH = NA_HEADS * HEAD_DIM
NA_GROUP = 32
GLA_HEADS = 4
GLA_DK = 64
GLA_DV = 128
GLA_KW = GLA_HEADS * GLA_DK
GLA_VW = GLA_HEADS * GLA_DV
GLA_RANK = 16
GLA_TAU = 16.0
GLA_CHUNK = 64
GLA_BLOCK = 256
IN_AB = 3 * NA_WIDTH + 2 * GLA_KW + 2 * GLA_VW + 2 * GLA_RANK
IN_AB_PAD = 3200
SWA_HEADS = 16
SWA_KV = 4
SWA_GROUP = SWA_HEADS // SWA_KV
SWA_BLOCK = 128
SWA_QB = 2
SWA_SPLIT = 1
IN_C = (SWA_HEADS + 2 * SWA_KV) * HEAD_DIM
D_FF = 2816
FF_CHUNK = 256
N_FF_CHUNKS = D_FF // FF_CHUNK
assert N_FF_CHUNKS * FF_CHUNK == D_FF
HALO = 16
OUTPROJ_SUB = 256
PROJ_SUB = 256
ROPE_BASE = 10000.0
ROPE_TILE = 256
VMEM_LIMIT = 56 * 1024 * 1024

AB_QA, AB_KA, AB_VA = 0, 4, 8
AB_QB, AB_KB = 12, 14
AB_VB, AB_RB = 8, 10
AB_Z = 24
C_Q, C_K, C_V = 0, 8, 10


def _dot(a, b):
    return jnp.dot(a, b, preferred_element_type=F32)


def _dot_nt(a, b):
    return lax.dot_general(a, b, (((1,), (1,)), ((), ())), preferred_element_type=F32)


def _rms(x, g):
    return x * lax.rsqrt(jnp.mean(x * x, axis=-1, keepdims=True) + EPS) * g


def _silu(x):
    return x / (1.0 + jnp.exp(-x))


def _params(n_grid):
    return pltpu.CompilerParams(dimension_semantics=("arbitrary",) * n_grid,
                                vmem_limit_bytes=VMEM_LIMIT)


def _resident(shape):
    nd = len(shape)
    return pl.BlockSpec(shape, lambda *_: (0,) * nd, pipeline_mode=pl.Buffered(1))


def _layer_block(arr, layer, rows=None, row_block=0):
    shape = list(arr.shape[1:])
    if rows is not None:
        shape[0] = rows
    tail = (0,) * (len(shape) - 1)
    return pl.BlockSpec((None, *shape), lambda *_: (layer, row_block) + tail, pipeline_mode=pl.Buffered(1))


def _head_mask(rows, h):
    lane = lax.broadcasted_iota(jnp.int32, (rows, LANES), 1)
    return (lane < HEAD_DIM) if h == 0 else (lane >= HEAD_DIM)


def _mod_kernel(c_ref, w_ref, b_ref, o_ref):
    s = _silu(c_ref[...]).astype(BF16)
    o_ref[0] = _dot(s, w_ref[0].astype(BF16)) + b_ref[0]


def _modulation(c_all, w_mod, b_mod):
    depth, d, n = w_mod.shape
    tn = 1536
    return pl.pallas_call(
        _mod_kernel,
        out_shape=jax.ShapeDtypeStruct((depth, 8, n), F32),
        grid=(depth, n // tn),
        in_specs=[pl.BlockSpec((8, d), lambda i, j: (0, 0)),
                  pl.BlockSpec((1, d, tn), lambda i, j: (i, 0, j)),
                  pl.BlockSpec((1, 1, tn), lambda i, j: (i, 0, j))],
        out_specs=pl.BlockSpec((1, 8, tn), lambda i, j: (i, 0, j)),
        compiler_params=_params(2),
        name="modulation",
    )(c_all, w_mod, b_mod.reshape(depth, 1, n))


def _rope(acc, cos, sin):
    n = acc.shape[1]
    lane = lax.broadcasted_iota(jnp.int32, acc.shape, 1)
    partner = jnp.where(lane % 2 == 0, pltpu.roll(acc, n - 1, 1), pltpu.roll(acc, 1, 1))
    return acc * cos + partner * sin


def _proj_kernel(*refs, chunk, rope_chunks, shift_idx, w_transposed):
    if rope_chunks:
        x_ref, mod_ref, g_ref, w_ref, cos_ref, sin_ref, o_ref = refs
    else:
        x_ref, mod_ref, g_ref, w_ref, o_ref = refs
    mod = mod_ref[0]
    tm = x_ref.shape[1]
    sub = min(tm, PROJ_SUB)
    for r0 in range(0, tm, sub):
        h = (_rms(x_ref[0, r0:r0 + sub], g_ref[...]) * (1.0 + mod[shift_idx + 1:shift_idx + 2])
             + mod[shift_idx:shift_idx + 1])
        hb = h.astype(BF16)
        for c in range(o_ref.shape[2] // chunk):
            if w_transposed:
                acc = _dot_nt(hb, w_ref[c * chunk:(c + 1) * chunk, :])
            else:
                acc = _dot(hb, w_ref[:, c * chunk:(c + 1) * chunk])
            if c < rope_chunks:
                acc = _rope(acc, cos_ref[r0:r0 + sub], sin_ref[r0:r0 + sub])
            o_ref[0, r0:r0 + sub, c * chunk:(c + 1) * chunk] = acc.astype(BF16)


def _proj(x, mod, g, w, layer, tm, chunk, rope=None, rope_chunks=0, w_transposed=False):
    b, s, d = x.shape
    n = w.shape[1] if w_transposed else w.shape[2]
    per_batch = mod.shape[0] > 1
    in_specs = [pl.BlockSpec((1, tm, d), lambda bi, i: (bi, i, 0)),
                pl.BlockSpec((1, 6, d), (lambda bi, i: (bi, 0, 0)) if per_batch else (lambda bi, i: (0, 0, 0))),
                _resident((1, d)),
                _layer_block(w, layer)]
    args = [x, mod, g.reshape(1, d), w]
    if rope_chunks:
        in_specs += [pl.BlockSpec((tm, chunk), lambda bi, i: (i, 0))] * 2
        args += list(rope)
    return pl.pallas_call(
        functools.partial(_proj_kernel, chunk=chunk, rope_chunks=rope_chunks, shift_idx=0,
                          w_transposed=w_transposed),
        out_shape=jax.ShapeDtypeStruct((b, s, n), BF16),
        grid=(b, s // tm),
        in_specs=in_specs,
        out_specs=pl.BlockSpec((1, tm, n), lambda bi, i: (bi, i, 0)),
        compiler_params=_params(2),
        name="proj",
    )(*args)


def _outproj_kernel(*refs, n_parts):
    y_refs, w_refs = refs[:n_parts], refs[n_parts:2 * n_parts]
    x_ref, mod_ref, g_ref, o_ref = refs[2 * n_parts:]
    tm = x_ref.shape[1]
    sub = min(tm, OUTPROJ_SUB)
    gate, g = mod_ref[0][2:3], g_ref[...]
    accs = []
    for r0 in range(0, tm, sub):
        acc = _dot(y_refs[0][0, r0:r0 + sub], w_refs[0][...])
        for yr, wr in zip(y_refs[1:], w_refs[1:]):
            acc = acc + _dot(yr[0, r0:r0 + sub], wr[...])
        accs.append(acc)
    for k, acc in enumerate(accs):
        o_ref[0, k * sub:(k + 1) * sub] = x_ref[0, k * sub:(k + 1) * sub] + gate * _rms(acc, g)


def _outproj(ys, w, layer, x, mod, g, tm):
    b, s, d = x.shape
    per_batch = mod.shape[0] > 1
    in_specs = [pl.BlockSpec((1, tm, y.shape[2]), lambda bi, i: (bi, i, 0)) for y in ys]
    in_specs += [_layer_block(w, layer, y.shape[2], k) for k, y in enumerate(ys)]
    in_specs += [pl.BlockSpec((1, tm, d), lambda bi, i: (bi, i, 0)),
                 pl.BlockSpec((1, 6, d), (lambda bi, i: (bi, 0, 0)) if per_batch else (lambda bi, i: (0, 0, 0))),
                 _resident((1, d))]
    return pl.pallas_call(
        functools.partial(_outproj_kernel, n_parts=len(ys)),
        out_shape=jax.ShapeDtypeStruct((b, s, d), F32),
        grid=(b, s // tm),
        in_specs=in_specs,
        out_specs=pl.BlockSpec((1, tm, d), lambda bi, i: (bi, i, 0)),
        compiler_params=_params(2),
        name="outproj",
    )(*ys, *([w] * len(ys)), x, mod, g.reshape(1, d))


def _ffn_kernel(x_ref, xp_ref, xn_ref, mod_ref, gpre_ref, gpost_ref, wu_ref, cw_ref, wd_ref, o_ref,
                h_ref, u_ref, a_ref, *, tm):
    i = pl.program_id(1)
    mod = mod_ref[0]
    gpre = gpre_ref[...]

    def norm_mod(x):
        return _rms(x, gpre) * (1.0 + mod[4:5]) + mod[3:4]

    keep_prev = (i > 0).astype(F32)
    keep_next = (i < pl.num_programs(1) - 1).astype(F32)
    rows = tm + 2 * HALO
    half = tm // 2
    mid = HALO + half
    n_u = u_ref.shape[0]

    def up(c, r0=0, r1=rows):
        h = h_ref[r0:r1]
        for part in range(2):
            col = part * D_FF + c * FF_CHUNK
            u_ref[c % n_u, r0:r1, part * FF_CHUNK:(part + 1) * FF_CHUNK] = _dot(h, wu_ref[:, col:col + FF_CHUNK])

    h_ref[0:HALO] = (norm_mod(xp_ref[0]) * keep_prev).astype(BF16)
    h_ref[HALO:mid] = norm_mod(x_ref[0, :half]).astype(BF16)
    up(0, 0, mid)
    h_ref[mid:HALO + tm] = norm_mod(x_ref[0, half:]).astype(BF16)
    h_ref[HALO + tm:] = (norm_mod(xn_ref[0]) * keep_next).astype(BF16)
    up(0, mid, rows)

    def act(c):
        u = u_ref[c % n_u]
        cw = cw_ref[c]
        y = (pltpu.roll(u, 1, 0)[HALO:HALO + tm] * cw[0:1]
             + u[HALO:HALO + tm] * cw[1:2]
             + pltpu.roll(u, rows - 1, 0)[HALO:HALO + tm] * cw[2:3]
             + cw[3:4])
        a_ref[:, c * FF_CHUNK:(c + 1) * FF_CHUNK] = (_silu(y[:, :FF_CHUNK]) * y[:, FF_CHUNK:]).astype(BF16)

    for c in range(1, min(n_u - 1, N_FF_CHUNKS)):
        up(c)
    for c in range(N_FF_CHUNKS):
        if c + n_u - 1 < N_FF_CHUNKS:
            up(c + n_u - 1)
        act(c)
    accs = [_dot(a_ref[r0:r0 + half], wd_ref[...]) for r0 in (0, half)]
    for r0, acc in zip((0, half), accs):
        o_ref[0, r0:r0 + half] = x_ref[0, r0:r0 + half] + mod[5:6] * _rms(acc, gpost_ref[...])


def _ffn(x, mod, g_pre, g_post, wu, cw, wd, layer, tm):
    b, s, d = x.shape
    per_batch = mod.shape[0] > 1
    hb = tm // HALO
    n_halo = s // HALO
    return pl.pallas_call(
        functools.partial(_ffn_kernel, tm=tm),
        out_shape=jax.ShapeDtypeStruct((b, s, d), F32),
        grid=(b, s // tm),
        in_specs=[pl.BlockSpec((1, tm, d), lambda bi, i: (bi, i, 0)),
                  pl.BlockSpec((1, HALO, d), lambda bi, i: (bi, jnp.maximum(i * hb - 1, 0), 0)),
                  pl.BlockSpec((1, HALO, d), lambda bi, i: (bi, jnp.minimum((i + 1) * hb, n_halo - 1), 0)),
                  pl.BlockSpec((1, 6, d), (lambda bi, i: (bi, 0, 0)) if per_batch else (lambda bi, i: (0, 0, 0))),
                  _resident((1, d)), _resident((1, d)),
                  _layer_block(wu, layer), _layer_block(cw, layer), _layer_block(wd, layer)],
        out_specs=pl.BlockSpec((1, tm, d), lambda bi, i: (bi, i, 0)),
        scratch_shapes=[pltpu.VMEM((tm + 2 * HALO, d), BF16),
                        pltpu.VMEM((3, tm + 2 * HALO, 2 * FF_CHUNK), F32),
                        pltpu.VMEM((tm, D_FF), BF16)],
        compiler_params=_params(2),
        name="ffn",
    )(x, x, x, mod, g_pre.reshape(1, d), g_post.reshape(1, d), wu, cw, wd)


def _na_kernel(q_ref, k_ref, v_ref, kc_ref, vc_ref, bias_ref, o_ref, *, rows):
    kc = kc_ref[0]
    vc = vc_ref[0]
    win = NA_WIN_H * GRID_W
    mask_a = _head_mask(GRID_W, 0)
    vc_ext = jnp.concatenate([vc, jnp.ones_like(vc)], axis=1)
    ones_w = jnp.ones((win, LANES), BF16)
    group_rows = math.gcd(NA_GROUP, rows)

    def group(gi, carry):
        starts, scores, probs = [], [], []
        for i in range(group_rows):
            r = gi * group_rows + i
            r0 = jnp.clip(r - NA_WIN_H // 2, 0, rows - NA_WIN_H)
            q0 = pl.multiple_of(r * GRID_W, GRID_W)
            k0 = pl.multiple_of(r0 * GRID_W, GRID_W)
            q = q_ref[0, pl.ds(q0, GRID_W), :]
            zero = jnp.zeros_like(q)
            q2 = jnp.concatenate([jnp.where(mask_a, q, zero), jnp.where(mask_a, zero, q)], axis=0)
            d0 = r0 - r + NA_WIN_H - 1
            bias = jnp.concatenate(
                [jnp.concatenate([bias_ref[0, hh, d0 + 2 * a] for a in range(NA_WIN_H // 2)], axis=1)
                 for hh in range(2)], axis=0)
            s = _dot_nt(q2, k_ref[0, pl.ds(k0, win), :]) + bias
            sc = _dot_nt(q2, kc)
            starts.append((q0, k0))
            scores.append((s, sc))
        for s, sc in scores:
            m = jnp.maximum(jnp.max(s, axis=-1, keepdims=True), jnp.max(sc, axis=-1, keepdims=True))
            probs.append((jnp.exp2(s - m).astype(BF16), jnp.exp2(sc - m).astype(BF16)))
        for (q0, k0), (p, pc) in zip(starts, probs):
            v_ext = jnp.concatenate([v_ref[0, pl.ds(k0, win), :], ones_w], axis=1)
            o_ext = _dot(p, v_ext) + _dot(pc, vc_ext)
            o2 = o_ext[:, :LANES] / o_ext[:, LANES:]
            o_ref[0, pl.ds(q0, GRID_W), :] = jnp.where(mask_a, o2[:GRID_W], o2[GRID_W:]).astype(BF16)
        return carry

    lax.fori_loop(0, rows // group_rows, group, 0)


def _na(pl_, pc_, bias):
    b, s, _ = pl_.shape
    n_ctx = pc_.shape[1]
    n_pairs = NA_HEADS // 2
    return pl.pallas_call(
        functools.partial(_na_kernel, rows=s // GRID_W),
        out_shape=jax.ShapeDtypeStruct((b, s, NA_WIDTH), BF16),
        grid=(b, n_pairs),
        in_specs=[pl.BlockSpec((1, s, LANES), lambda bi, p: (bi, 0, AB_QA + p)),
                  pl.BlockSpec((1, s, LANES), lambda bi, p: (bi, 0, AB_KA + p)),
                  pl.BlockSpec((1, s, LANES), lambda bi, p: (bi, 0, AB_VA + p)),
                  pl.BlockSpec((1, n_ctx, LANES), lambda bi, p: (bi, 0, AB_KA + p)),
                  pl.BlockSpec((1, n_ctx, LANES), lambda bi, p: (bi, 0, AB_VA + p)),
                  pl.BlockSpec((1,) + bias.shape[1:], lambda bi, p: (p, 0, 0, 0, 0))],
        out_specs=pl.BlockSpec((1, s, LANES), lambda bi, p: (bi, 0, p)),
        compiler_params=_params(2),
        name="na",
    )(pl_, pl_, pl_, pc_, pc_, bias)


def _na_bias(rpb):
    h = rpb.shape[0]
    cidx = np.arange(GRID_W)
    c_start = np.clip(cidx - NA_WIN_W // 2, 0, GRID_W - NA_WIN_W)
    col_ok = (cidx[None, :] >= c_start[:, None]) & (cidx[None, :] < c_start[:, None] + NA_WIN_W)
    lo, hi = GRID_W - NA_WIN_W, GRID_W - NA_WIN_W
    rp = rpb.astype(F32) * LOG2E
    ext = jnp.concatenate([jnp.repeat(rp[..., :1], lo, axis=-1), rp, jnp.repeat(rp[..., -1:], hi, axis=-1)], axis=-1)
    bm = jnp.stack([ext[..., GRID_W - 1 - q:2 * GRID_W - 1 - q] for q in range(GRID_W)], axis=2)
    bm = jnp.where(col_ok[None, None], bm, NEG_INF)
    pairs = jnp.concatenate([bm[:, :-1], bm[:, 1:]], axis=-1)
    return pairs.reshape((h // 2, 2) + pairs.shape[1:])


def _ctx_attn_kernel(*refs, n_q_tiles, group, has_sink):
    if has_sink:
        q_ref, k_ref, v_ref, sink_ref, o_ref = refs
    else:
        q_ref, k_ref, v_ref, o_ref = refs
    n = q_ref.shape[1]
    mask_a = _head_mask(n, 0)
    scores, probs = [], []
    for t in range(n_q_tiles):
        q = q_ref[0, :, t * LANES:(t + 1) * LANES]
        zero = jnp.zeros_like(q)
        q2 = jnp.concatenate([jnp.where(mask_a, q, zero), jnp.where(mask_a, zero, q)], axis=0)
        kt = t // group
        scores.append(_dot_nt(q2, k_ref[0, :, kt * LANES:(kt + 1) * LANES]))
    for t, s in enumerate(scores):
        m = jnp.max(s, axis=-1, keepdims=True)
        if has_sink:
            sk = jnp.concatenate([jnp.broadcast_to(sink_ref[t, h:h + 1, 0:1], (n, 1)) for h in range(2)], axis=0)
            m = jnp.maximum(m, sk)
        p = jnp.exp2(s - m)
        den = jnp.sum(p, axis=-1, keepdims=True)
        if has_sink:
            den = den + jnp.exp2(sk - m)
        probs.append((p.astype(BF16), den))
    for t, (p, den) in enumerate(probs):
        kt = t // group
        o2 = _dot(p, v_ref[0, :, kt * LANES:(kt + 1) * LANES]) / den
        o_ref[0, :, t * LANES:(t + 1) * LANES] = jnp.where(mask_a, o2[:n], o2[n:]).astype(BF16)


def _ctx_attn(pc_, q_off, k_off, v_off, n_q_tiles, group, sink=None):
    b, n, _ = pc_.shape
    n_k_tiles = n_q_tiles // group
    qw, kw = n_q_tiles * LANES, n_k_tiles * LANES
    assert q_off % n_q_tiles == 0 and k_off % n_k_tiles == 0 and v_off % n_k_tiles == 0
    in_specs = [pl.BlockSpec((1, n, qw), lambda bi: (bi, 0, q_off // n_q_tiles)),
                pl.BlockSpec((1, n, kw), lambda bi: (bi, 0, k_off // n_k_tiles)),
                pl.BlockSpec((1, n, kw), lambda bi: (bi, 0, v_off // n_k_tiles))]
    args = [pc_, pc_, pc_]
    if sink is not None:
        in_specs.append(_resident(sink.shape))
        args.append(sink)
    return pl.pallas_call(
        functools.partial(_ctx_attn_kernel, n_q_tiles=n_q_tiles, group=group, has_sink=sink is not None),
        out_shape=jax.ShapeDtypeStruct((b, n, qw), BF16),
        grid=(b,),
        in_specs=in_specs,
        out_specs=pl.BlockSpec((1, n, qw), lambda bi: (bi, 0, 0)),
        compiler_params=_params(1),
        name="ctx_attn",
    )(*args)


def _gla_kernel(*refs, n_lat_blocks, need_ctx):
    (ql, kl, vl, zl, rl, qc, kc, vc, zc, rc, w2f, w2b, baf, bab, g_ref) = refs[:15]
    n_out = 2 if need_ctx else 1
    yl_ref = refs[15]
    yc_ref = refs[16] if need_ctx else None
    qt_s, kd_s, dec_s, vt_s, o_s = refs[15 + n_out:]
    blk, ck = GLA_BLOCK, GLA_CHUNK
    cpb = blk // ck
    shift = ck.bit_length() - 1
    ri = lax.broadcasted_iota(jnp.int32, (blk, blk), 0)
    ci = lax.broadcasted_iota(jnp.int32, (blk, blk), 1)
    same = lax.shift_right_logical(ri, shift) == lax.shift_right_logical(ci, shift)
    att_mask = [same & (ci <= ri), same & (ci >= ri)]
    as_bf16 = lambda m: jnp.where(m, 1.0, 0.0).astype(BF16)
    cum = [jnp.concatenate([as_bf16(att_mask[d]), as_bf16(same)], axis=0) for d in range(2)]
    head = [_head_mask(blk, 0), _head_mask(blk, 1)]
    head_ck = [_head_mask(ck, 0), _head_mask(ck, 1)]
    tok_lo = lax.broadcasted_iota(jnp.int32, (2 * GLA_DV, LANES), 1) < ck
    w2 = jnp.concatenate([w2f[...], w2b[...]], axis=1)
    ba = jnp.concatenate([baf[...], bab[...]], axis=1)
    g = g_ref[...]
    lat4, ctx4 = (ql, kl, vl, zl), (qc, kc, vc, zc)

    def phase1(blocks):
        gates = []
        for (_, _, _, z_ref), t_in, _ in blocks:
            x = _dot(z_ref[0, pl.ds(t_in, blk), :], w2) + ba
            la = (jnp.minimum(x, 0.0) - jnp.log(1.0 + jnp.exp(-jnp.abs(x)))) * (1.0 / GLA_TAU)
            hi = la.astype(BF16)
            gates.append((hi, (la - hi.astype(F32)).astype(BF16)))
        cums = []
        for hi, lo in gates:
            per_dir = []
            for d in range(2):
                c2 = _dot(cum[d], jnp.concatenate([hi[:, d * LANES:(d + 1) * LANES],
                                                   lo[:, d * LANES:(d + 1) * LANES]], axis=1))
                per_dir.append((c2[:blk, :LANES] + c2[:blk, LANES:], c2[blk:, :LANES] + c2[blk:, LANES:]))
            cums.append(per_dir)
        scores = []
        for ((q_ref, k_ref, _, _), t_in, bo), per_dir in zip(blocks, cums):
            q = q_ref[0, pl.ds(t_in, blk), :].astype(F32)
            k = k_ref[0, pl.ds(t_in, blk), :].astype(F32)
            t_out = pl.multiple_of(bo * blk, blk)
            items = []
            for d, (b, tot) in enumerate(per_dir):
                qt = q * jnp.exp(b)
                kt = (k * jnp.exp(-b)).astype(BF16)
                qt_s[d, pl.ds(t_out, blk), :] = qt.astype(BF16)
                kd_s[d, pl.ds(t_out, blk), :] = (k * jnp.exp(tot - b)).astype(BF16)
                for j in range(cpb):
                    dec_s[d, pl.ds(bo * cpb + j, 1), :] = jnp.exp(tot[j * ck:j * ck + 1])
                for h in range(2):
                    items.append((d, h, _dot_nt(jnp.where(head[h], qt, 0.0).astype(BF16), kt)))
            scores.append(items)
        for ((_, _, v_ref, _), t_in, bo), items in zip(blocks, scores):
            v = v_ref[0, pl.ds(t_in, blk), :]
            oi = [None, None]
            for d, h, s in items:
                c = _dot(jnp.where(att_mask[d], s, 0.0).astype(BF16), v[:, h * GLA_DV:(h + 1) * GLA_DV])
                oi[h] = c if oi[h] is None else oi[h] + c
            o_s[pl.ds(pl.multiple_of(bo * blk, blk), blk), :] = jnp.concatenate(oi, axis=1)
            vt_s[bo] = v.astype(F32).T.astype(BF16)

    orders = [list(range(cpb)), list(range(cpb - 1, -1, -1))]

    def scan_pairs(pairs, sts):
        upds = []
        for bos in pairs:
            upd = [{}, {}]
            for d in range(2):
                vt = vt_s[bos[d]]
                kd = kd_s[d, pl.ds(pl.multiple_of(bos[d] * blk, blk), blk), :]
                for c in orders[d]:
                    half = c // 2
                    vt_c = vt[:, half * LANES:(half + 1) * LANES]
                    vt_c = jnp.where(tok_lo if c % 2 == 0 else ~tok_lo, vt_c, jnp.zeros_like(vt_c))
                    upd[d][c] = _dot(vt_c, kd[half * LANES:(half + 1) * LANES])
            upds.append(upd)
        sts = list(sts)
        for bos, upd in zip(pairs, upds):
            for step in range(cpb):
                for d in range(2):
                    c = orders[d][step]
                    tc = pl.multiple_of(bos[d] * blk + c * ck, ck)
                    qt = qt_s[d, pl.ds(tc, ck), :]
                    stb = sts[d].astype(BF16)
                    outs = [_dot_nt(jnp.where(head_ck[h], qt, jnp.zeros_like(qt)),
                                    stb[h * GLA_DV:(h + 1) * GLA_DV]) for h in range(2)]
                    o_s[pl.ds(tc, ck), :] += jnp.concatenate(outs, axis=1)
                    sts[d] = dec_s[d, pl.ds(bos[d] * cpb + c, 1), :] * sts[d] + upd[d][c]
        return tuple(sts)

    def finalize(bo, r_ref, t_in, y_ref):
        o = o_s[pl.ds(pl.multiple_of(bo * blk, blk), blk), :]
        r = r_ref[0, pl.ds(t_in, blk), :].astype(F32)
        on = jnp.concatenate(
            [o[:, h * GLA_DV:(h + 1) * GLA_DV]
             * lax.rsqrt(jnp.mean(o[:, h * GLA_DV:(h + 1) * GLA_DV] ** 2, axis=-1, keepdims=True) + EPS)
             for h in range(2)], axis=1)
        y_ref[0, pl.ds(t_in, blk), :] = (on * g * _silu(r)).astype(BF16)

    phase1([(ctx4, 0, 0)])

    def phase1_lat(i, carry):
        phase1([(lat4, pl.multiple_of((2 * i + j) * blk, blk), 1 + 2 * i + j) for j in range(2)])
        return carry

    lax.fori_loop(0, n_lat_blocks // 2, phase1_lat, 0)

    zero = jnp.zeros((2 * GLA_DV, 2 * GLA_DK), F32)
    sts = scan_pairs([(0, 0)], (zero, zero))
    lax.fori_loop(
        0, n_lat_blocks // 2,
        lambda i, s: scan_pairs([(1 + 2 * i + j, n_lat_blocks - 2 * i - j) for j in range(2)], s), sts)

    if need_ctx:
        finalize(0, rc, 0, yc_ref)

    def finalize_lat(i, carry):
        for j in range(2):
            finalize(1 + 2 * i + j, rl, pl.multiple_of((2 * i + j) * blk, blk), yl_ref)
        return carry

    lax.fori_loop(0, n_lat_blocks // 2, finalize_lat, 0)


def _gla(pl_, pc_, w2, ba, g, need_ctx):
    b, s, _ = pl_.shape
    n_c = pc_.shape[1]
    n_pairs = GLA_HEADS // 2
    w = 2 * GLA_DV

    def stream(n):
        return [pl.BlockSpec((1, n, LANES), lambda bi, p: (bi, 0, AB_QB + p)),
                pl.BlockSpec((1, n, LANES), lambda bi, p: (bi, 0, AB_KB + p)),
                pl.BlockSpec((1, n, w), lambda bi, p: (bi, 0, AB_VB + p)),
                pl.BlockSpec((1, n, LANES), lambda bi, p: (bi, 0, AB_Z)),
                pl.BlockSpec((1, n, w), lambda bi, p: (bi, 0, AB_RB + p))]

    in_specs = stream(s) + stream(n_c) + [
        pl.BlockSpec((LANES, LANES), lambda bi, p: (0, p)),
        pl.BlockSpec((LANES, LANES), lambda bi, p: (0, n_pairs + p)),
        pl.BlockSpec((1, LANES), lambda bi, p: (0, p)),
        pl.BlockSpec((1, LANES), lambda bi, p: (0, n_pairs + p)),
        pl.BlockSpec((1, w), lambda bi, p: (0, p))]
    out_shape = [jax.ShapeDtypeStruct((b, s, GLA_VW), BF16)]
    out_specs = [pl.BlockSpec((1, s, w), lambda bi, p: (bi, 0, p))]
    if need_ctx:
        out_shape.append(jax.ShapeDtypeStruct((b, n_c, GLA_VW), BF16))
        out_specs.append(pl.BlockSpec((1, n_c, w), lambda bi, p: (bi, 0, p)))
    assert n_c == GLA_BLOCK and s % (2 * GLA_BLOCK) == 0
    n_lat_blocks = s // GLA_BLOCK
    t_all = n_c + s
    n_chunks = -(-(t_all // GLA_CHUNK) // 8) * 8
    scratch = [pltpu.VMEM((2, t_all, LANES), BF16),
               pltpu.VMEM((2, t_all, LANES), BF16),
               pltpu.VMEM((2, n_chunks, LANES), F32),
               pltpu.VMEM((1 + n_lat_blocks, w, GLA_BLOCK), BF16),
               pltpu.VMEM((t_all, w), F32)]
    res = pl.pallas_call(
        functools.partial(_gla_kernel, n_lat_blocks=n_lat_blocks, need_ctx=need_ctx),
        out_shape=out_shape,
        grid=(b, n_pairs),
        in_specs=in_specs,
        out_specs=out_specs,
        scratch_shapes=scratch,
        compiler_params=_params(2),
        name="gla",
    )(*([pl_] * 5), *([pc_] * 5), w2, w2, ba, ba, g.reshape(1, GLA_VW))
    return (res[0], res[1]) if need_ctx else (res[0], None)


def _gla_gate_weights(wa2, ba):
    w = jnp.zeros((LANES, 2 * GLA_KW), F32)
    w = w.at[0:GLA_RANK, 0:GLA_KW].set(wa2[0])
    w = w.at[GLA_RANK:2 * GLA_RANK, GLA_KW:].set(wa2[1])
    return w.astype(BF16), ba.reshape(1, 2 * GLA_KW).astype(F32)


def _swa_kernel(q_ref, kp_ref, kn_ref, kx_ref, vp_ref, vn_ref, vx_ref, kc_ref, vc_ref, sink_ref, o_ref):
    n = pl.program_id(1)
    blk = SWA_BLOCK
    n_c = kc_ref.shape[1]
    n_keys = 3 * blk + n_c
    k_lat = jnp.concatenate([kp_ref[0], kn_ref[0], kx_ref[0]], axis=0)
    v_lat = jnp.concatenate([vp_ref[0], vn_ref[0], vx_ref[0]], axis=0)
    kj = lax.broadcasted_iota(jnp.int32, (blk, blk), 0)
    qi = lax.broadcasted_iota(jnp.int32, (blk, blk), 1)
    in_prev = kj >= qi
    in_next = kj <= qi
    first, last = n == 0, n == pl.num_programs(1) - 1
    n_pairs = SWA_KV // 2
    n_units = SWA_QB * n_pairs * SWA_SPLIT
    unit_tiles = SWA_GROUP // SWA_SPLIT
    rows = unit_tiles * blk
    q_head = [_head_mask(rows, 0), _head_mask(rows, 1)]
    out_a = lax.broadcasted_iota(jnp.int32, (LANES, rows), 0) < HEAD_DIM

    def block_inputs(j):
        k = jnp.concatenate([k_lat[j * blk:(j + 3) * blk], kc_ref[0]], axis=0)
        v = jnp.concatenate([v_lat[j * blk:(j + 3) * blk], vc_ref[0]], axis=0)
        has_prev = jnp.logical_not(first) if j == 0 else True
        has_next = jnp.logical_not(last) if j == SWA_QB - 1 else True
        return k, v, tuple(jnp.concatenate([m] * unit_tiles, axis=1) for m in (in_prev & has_prev, in_next & has_next))

    blocks = [block_inputs(j) for j in range(SWA_QB)]
    vts = {}

    def unit_inputs(u):
        j, pair, part = u // (n_pairs * SWA_SPLIT), (u // SWA_SPLIT) % n_pairs, u % SWA_SPLIT
        tiles = [pair * SWA_GROUP + part * unit_tiles + g for g in range(unit_tiles)]
        q = jnp.concatenate([q_ref[0, j * blk:(j + 1) * blk, t * LANES:(t + 1) * LANES] for t in tiles], axis=0)
        if (j, pair) not in vts:
            vts[j, pair] = blocks[j][1][:, pair * LANES:(pair + 1) * LANES].astype(F32).T.astype(BF16)
        return j, pair, tiles, q, vts[j, pair]

    def scores(u, h, unit):
        j, pair, tiles, q, _ = unit
        k, _, (ok_prev, ok_next) = blocks[j]
        qm = jnp.where(q_head[h], q, jnp.zeros_like(q))
        st = _dot_nt(k[:, pair * LANES:(pair + 1) * LANES], qm)
        st = jnp.concatenate([jnp.where(ok_prev, st[:blk], NEG_INF), st[blk:2 * blk],
                              jnp.where(ok_next, st[2 * blk:3 * blk], NEG_INF), st[3 * blk:]], axis=0)
        sk = jnp.concatenate([jnp.broadcast_to(sink_ref[t, h:h + 1, 0:1], (1, blk)) for t in tiles], axis=1)
        return st, sk

    def finish(unit, st, sk):
        m = jnp.maximum(jnp.max(st, axis=0, keepdims=True), sk)
        p = jnp.exp2(st - m)
        den = jnp.sum(p, axis=0, keepdims=True) + jnp.exp2(sk - m)
        return _dot(unit[4], p.astype(BF16)) / den

    def store(unit, outs):
        j, _, tiles, _, _ = unit
        o = jnp.where(out_a, outs[0], outs[1]).T.astype(BF16)
        for g, t in enumerate(tiles):
            o_ref[0, j * blk:(j + 1) * blk, t * LANES:(t + 1) * LANES] = o[g * blk:(g + 1) * blk]

    units = {0: unit_inputs(0)}
    pending = scores(0, 0, units[0])
    outs = []
    for c in range(2 * n_units):
        u, h = c // 2, c % 2
        nxt = None
        if c + 1 < 2 * n_units:
            un, hn = (c + 1) // 2, (c + 1) % 2
            if un not in units:
                units[un] = unit_inputs(un)
            nxt = scores(un, hn, units[un])
        outs.append(finish(units[u], *pending))
        if h == 1:
            store(units[u], outs)
            outs = []
        pending = nxt


def _swa(pl_, pc_, sink):
    b, s, _ = pl_.shape
    n_c = pc_.shape[1]
    nb = s // SWA_BLOCK
    step = SWA_QB * SWA_BLOCK
    assert s % step == 0
    qw = SWA_HEADS * HEAD_DIM
    kw = SWA_KV * HEAD_DIM
    k_blk, v_blk = (C_K * LANES) // kw, (C_V * LANES) // kw

    def kv(off):
        return [pl.BlockSpec((1, SWA_BLOCK, kw), lambda bi, n: (bi, jnp.maximum(n * SWA_QB - 1, 0), off)),
                pl.BlockSpec((1, step, kw), lambda bi, n: (bi, n, off)),
                pl.BlockSpec((1, SWA_BLOCK, kw), lambda bi, n: (bi, jnp.minimum((n + 1) * SWA_QB, nb - 1), off))]

    in_specs = ([pl.BlockSpec((1, step, qw), lambda bi, n: (bi, n, 0))] + kv(k_blk) + kv(v_blk)
                + [pl.BlockSpec((1, n_c, kw), lambda bi, n: (bi, 0, k_blk)),
                   pl.BlockSpec((1, n_c, kw), lambda bi, n: (bi, 0, v_blk)),
                   _resident(sink.shape)])
    return pl.pallas_call(
        _swa_kernel,
        out_shape=jax.ShapeDtypeStruct((b, s, qw), BF16),
        grid=(b, s // step),
        in_specs=in_specs,
        out_specs=pl.BlockSpec((1, step, qw), lambda bi, n: (bi, n, 0)),
        compiler_params=_params(2),
        name="swa",
    )(pl_, pl_, pl_, pl_, pl_, pl_, pl_, pc_, pc_, sink)


def _swa_pair_heads(a, axis):
    shape = a.shape
    a = a.reshape(shape[:axis] + (SWA_KV // 2, 2, SWA_GROUP, -1) + shape[axis + 1:])
    return jnp.swapaxes(a, axis + 1, axis + 2).reshape(shape)


def _rope_tables(n):
    t = jnp.arange(n)
    row = (t // GRID_W).astype(F32)
    col = (t % GRID_W).astype(F32)
    half = HEAD_DIM // 2
    inv = 1.0 / (ROPE_BASE ** (jnp.arange(0, half, 2, dtype=F32) / half))
    ang = jnp.concatenate([row[:, None] * inv, col[:, None] * inv], axis=-1)
    cos = jnp.repeat(jnp.cos(ang), 2, axis=1)
    sin = jnp.repeat(jnp.sin(ang), 2, axis=1) * jnp.tile(jnp.array([-1.0, 1.0], F32), half)
    reps = ROPE_TILE // HEAD_DIM
    return jnp.tile(cos, (1, reps)), jnp.tile(sin, (1, reps))


def kernel(x, c, ctx, c_ctx, w_mod, b_mod, g_mix_pre, g_mix_post, g_ffn_pre, g_ffn_post, w_out, w_up, conv_w,
           conv_b, w_down, w_in_ab, na_rpb, gla_wa2, gla_ba, gla_g, w_in_c, swa_sink):
    bsz, seq, d = x.shape
    n_ctx = ctx.shape[1]
    depth = w_mod.shape[0]
    tm_l = 512 if seq % 512 == 0 else 256
    tm_c = n_ctx
    tm_o = 1024 if seq % 1024 == 0 else tm_l

    c_all = jnp.zeros((8, d), F32).at[:bsz].set(c).at[bsz].set(c_ctx)
    mods = _modulation(c_all, w_mod, b_mod).reshape(depth, 8, 6, d)
    rope = _rope_tables(seq)

    col_scale_ab = np.ones((IN_AB,), np.float32)
    col_scale_ab[0:NA_WIDTH] = ATTN_SCALE
    col_scale_ab[3 * NA_WIDTH:3 * NA_WIDTH + GLA_KW] = GLA_DK ** -0.5

    n_q = SWA_HEADS * HEAD_DIM
    w_ab = jnp.pad(jnp.swapaxes(w_in_ab, 1, 2) * col_scale_ab[:, None],
                   ((0, 0), (0, IN_AB_PAD - IN_AB), (0, 0))).astype(BF16)
    w_c = jnp.concatenate([_swa_pair_heads(w_in_c[:, :, :n_q], 2) * ATTN_SCALE, w_in_c[:, :, n_q:]],
                          axis=2).astype(BF16)
    w_o = w_out.astype(BF16)
    head_rows = np.arange(n_q).reshape(SWA_KV // 2, 2, SWA_GROUP, HEAD_DIM).swapaxes(1, 2).reshape(-1)
    w_o_swa = jnp.take(w_o[1::2], head_rows, axis=1)
    wu = w_up.astype(BF16)
    wd = w_down.astype(BF16)
    cwb = jnp.concatenate([conv_w, conv_b[:, None]], axis=1)
    cw = jnp.concatenate([cwb[:, :, :D_FF].reshape(depth, 4, N_FF_CHUNKS, FF_CHUNK),
                          cwb[:, :, D_FF:].reshape(depth, 4, N_FF_CHUNKS, FF_CHUNK)], axis=3).transpose(0, 2, 1, 3)

    xl, xc = x, ctx
    for i in range(depth):
        need_ctx = i < depth - 1
        j = i // 2
        mod_l = mods[i, :bsz]
        mod_c = mods[i, bsz:bsz + 1]
        if i % 2 == 0:
            p_l = _proj(xl, mod_l, g_mix_pre[i], w_ab, j, tm_l, 640, w_transposed=True)
            p_c = _proj(xc, mod_c, g_mix_pre[i], w_ab, j, tm_c, 640, w_transposed=True)
            w2, ba = _gla_gate_weights(gla_wa2[j], gla_ba[j])
            y_na = _na(p_l, p_c, _na_bias(na_rpb[j]))
            y_gla, yc_gla = _gla(p_l, p_c, w2, ba, gla_g[j], need_ctx)
            ys_l = [y_na, y_gla]
            w_post, l_post = w_o, i
            if need_ctx:
                ys_c = [_ctx_attn(p_c, AB_QA, AB_KA, AB_VA, NA_HEADS // 2, 1), yc_gla]
        else:
            p_l = _proj(xl, mod_l, g_mix_pre[i], w_c, j, tm_l, ROPE_TILE, rope, (C_V * LANES) // ROPE_TILE)
            p_c = _proj(xc, mod_c, g_mix_pre[i], w_c, j, tm_c, ROPE_TILE)
            sink = jnp.broadcast_to(_swa_pair_heads(swa_sink[j] * LOG2E, 0).reshape(-1, 2, 1),
                                    (SWA_HEADS // 2, 2, LANES)).astype(F32)
            ys_l = [_swa(p_l, p_c, sink)]
            w_post, l_post = w_o_swa, j
            if need_ctx:
                ys_c = [_ctx_attn(p_c, C_Q, C_K, C_V, SWA_HEADS // 2, SWA_GROUP, sink)]

        xl = _outproj(ys_l, w_post, l_post, xl, mod_l, g_mix_post[i], tm_o)
        xl = _ffn(xl, mod_l, g_ffn_pre[i], g_ffn_post[i], wu, cw, wd, i, tm_l)
        if need_ctx:
            xc = _outproj(ys_c, w_post, l_post, xc, mod_c, g_mix_post[i], tm_c)
            xc = _ffn(xc, mod_c, g_ffn_pre[i], g_ffn_post[i], wu, cw, wd, i, tm_c)
    return xl
```

```python
import functools
import math

import numpy as np
import jax
import jax.numpy as jnp
from jax import lax
from jax.experimental import pallas as pl
from jax.experimental.pallas import tpu as pltpu

F32 = jnp.float32
BF16 = jnp.bfloat16

LANES = 128
HEAD_DIM = 64
GRID_W = 64
D_MODEL = 1024
EPS = 1e-6
NEG_INF = -1e30
LOG2E = float(np.log2(np.e))
ATTN_SCALE = HEAD_DIM ** -0.5 * LOG2E

NA_HEADS = 8
NA_WIN_H = 8
NA_WIN_W = 16
NA_WIDTH = NA_HEADS * HEAD_DIM
NA_GROUP = 32
GLA_HEADS = 4
GLA_DK = 64
GLA_DV = 128
GLA_KW = GLA_HEADS * GLA_DK
GLA_VW = GLA_HEADS * GLA_DV
GLA_RANK = 16
GLA_TAU = 16.0
GLA_CHUNK = 64
GLA_BLOCK = 256
GLA_P1_BLOCKS = 4
IN_AB = 3 * NA_WIDTH + 2 * GLA_KW + 2 * GLA_VW + 2 * GLA_RANK
IN_AB_PAD = 3200
SWA_HEADS = 16
SWA_KV = 4
SWA_GROUP = SWA_HEADS // SWA_KV
SWA_BLOCK = 128
SWA_QB = 2
SWA_SPLIT = 1
IN_C = (SWA_HEADS + 2 * SWA_KV) * HEAD_DIM
D_FF = 2816
FF_CHUNK = 256
N_FF_CHUNKS = D_FF // FF_CHUNK
assert N_FF_CHUNKS * FF_CHUNK == D_FF
HALO = 16
OUTPROJ_SUB = 256
PROJ_SUB = 256
ROPE_BASE = 10000.0
ROPE_TILE = 256
VMEM_LIMIT = 56 * 1024 * 1024

AB_QA, AB_KA, AB_VA = 0, 4, 8
AB_QB, AB_KB = 12, 14
AB_VB, AB_RB = 8, 10
AB_Z = 24
C_Q, C_K, C_V = 0, 8, 10


def _dot(a, b):
    return jnp.dot(a, b, preferred_element_type=F32)


def _dot_nt(a, b):
    return lax.dot_general(a, b, (((1,), (1,)), ((), ())), preferred_element_type=F32)


def _rms(x, g):
    return x * lax.rsqrt(jnp.mean(x * x, axis=-1, keepdims=True) + EPS) * g


def _silu(x):
    return x / (1.0 + jnp.exp(-x))


def _params(n_grid):
    return pltpu.CompilerParams(dimension_semantics=("arbitrary",) * n_grid,
                                vmem_limit_bytes=VMEM_LIMIT)


def _resident(shape):
    nd = len(shape)
    return pl.BlockSpec(shape, lambda *_: (0,) * nd, pipeline_mode=pl.Buffered(1))


def _layer_block(arr, layer, rows=None, row_block=0):
    shape = list(arr.shape[1:])
    if rows is not None:
        shape[0] = rows
    tail = (0,) * (len(shape) - 1)
    return pl.BlockSpec((None, *shape), lambda *_: (layer, row_block) + tail, pipeline_mode=pl.Buffered(1))


def _head_mask(rows, h):
    lane = lax.broadcasted_iota(jnp.int32, (rows, LANES), 1)
    return (lane < HEAD_DIM) if h == 0 else (lane >= HEAD_DIM)


def _mod_kernel(c_ref, w_ref, b_ref, o_ref):
    s = _silu(c_ref[...]).astype(BF16)
    o_ref[0] = _dot(s, w_ref[0].astype(BF16)) + b_ref[0]


def _modulation(c_all, w_mod, b_mod):
    depth, d, n = w_mod.shape
    tn = 1536
    return pl.pallas_call(
        _mod_kernel,
        out_shape=jax.ShapeDtypeStruct((depth, 8, n), F32),
        grid=(depth, n // tn),
        in_specs=[pl.BlockSpec((8, d), lambda i, j: (0, 0)),
                  pl.BlockSpec((1, d, tn), lambda i, j: (i, 0, j)),
                  pl.BlockSpec((1, 1, tn), lambda i, j: (i, 0, j))],
        out_specs=pl.BlockSpec((1, 8, tn), lambda i, j: (i, 0, j)),
        compiler_params=_params(2),
        name="modulation",
    )(c_all, w_mod, b_mod.reshape(depth, 1, n))


def _rope(acc, cos, sin):
    n = acc.shape[1]
    lane = lax.broadcasted_iota(jnp.int32, acc.shape, 1)
    partner = jnp.where(lane % 2 == 0, pltpu.roll(acc, n - 1, 1), pltpu.roll(acc, 1, 1))
    return acc * cos + partner * sin


def _proj_kernel(*refs, chunk, rope_chunks, shift_idx, w_transposed):
    if rope_chunks:
        x_ref, mod_ref, g_ref, w_ref, cos_ref, sin_ref, o_ref = refs
    else:
        x_ref, mod_ref, g_ref, w_ref, o_ref = refs
    mod = mod_ref[0]
    tm = x_ref.shape[1]
    sub = min(tm, PROJ_SUB)
    for r0 in range(0, tm, sub):
        h = (_rms(x_ref[0, r0:r0 + sub], g_ref[...]) * (1.0 + mod[shift_idx + 1:shift_idx + 2])
             + mod[shift_idx:shift_idx + 1])
        hb = h.astype(BF16)
        for c in range(o_ref.shape[2] // chunk):
            if w_transposed:
                acc = _dot_nt(hb, w_ref[c * chunk:(c + 1) * chunk, :])
            else:
                acc = _dot(hb, w_ref[:, c * chunk:(c + 1) * chunk])
            if c < rope_chunks:
                acc = _rope(acc, cos_ref[r0:r0 + sub], sin_ref[r0:r0 + sub])
            o_ref[0, r0:r0 + sub, c * chunk:(c + 1) * chunk] = acc.astype(BF16)


def _proj(x, mod, g, w, layer, tm, chunk, rope=None, rope_chunks=0, w_transposed=False):
    b, s, d = x.shape
    n = w.shape[1] if w_transposed else w.shape[2]
    per_batch = mod.shape[0] > 1
    in_specs = [pl.BlockSpec((1, tm, d), lambda bi, i: (bi, i, 0)),
                pl.BlockSpec((1, 6, d), (lambda bi, i: (bi, 0, 0)) if per_batch else (lambda bi, i: (0, 0, 0))),
                _resident((1, d)),
                _layer_block(w, layer)]
    args = [x, mod, g.reshape(1, d), w]
    if rope_chunks:
        in_specs += [pl.BlockSpec((tm, chunk), lambda bi, i: (i, 0))] * 2
        args += list(rope)
    return pl.pallas_call(
        functools.partial(_proj_kernel, chunk=chunk, rope_chunks=rope_chunks, shift_idx=0,
                          w_transposed=w_transposed),
        out_shape=jax.ShapeDtypeStruct((b, s, n), BF16),
        grid=(b, s // tm),
        in_specs=in_specs,
        out_specs=pl.BlockSpec((1, tm, n), lambda bi, i: (bi, i, 0)),
        compiler_params=_params(2),
        name="proj",
    )(*args)


def _outproj_kernel(*refs, n_parts):
    y_refs, w_refs = refs[:n_parts], refs[n_parts:2 * n_parts]
    x_ref, mod_ref, g_ref, o_ref = refs[2 * n_parts:]
    tm = x_ref.shape[1]
    sub = min(tm, OUTPROJ_SUB)
    gate, g = mod_ref[0][2:3], g_ref[...]
    accs = []
    for r0 in range(0, tm, sub):
        acc = _dot(y_refs[0][0, r0:r0 + sub], w_refs[0][...])
        for yr, wr in zip(y_refs[1:], w_refs[1:]):
            acc = acc + _dot(yr[0, r0:r0 + sub], wr[...])
        accs.append(acc)
    for k, acc in enumerate(accs):
        o_ref[0, k * sub:(k + 1) * sub] = x_ref[0, k * sub:(k + 1) * sub] + gate * _rms(acc, g)


def _outproj(ys, w, layer, x, mod, g, tm):
    b, s, d = x.shape
    per_batch = mod.shape[0] > 1
    in_specs = [pl.BlockSpec((1, tm, y.shape[2]), lambda bi, i: (bi, i, 0)) for y in ys]
    in_specs += [_layer_block(w, layer, y.shape[2], k) for k, y in enumerate(ys)]
    in_specs += [pl.BlockSpec((1, tm, d), lambda bi, i: (bi, i, 0)),
                 pl.BlockSpec((1, 6, d), (lambda bi, i: (bi, 0, 0)) if per_batch else (lambda bi, i: (0, 0, 0))),
                 _resident((1, d))]
    return pl.pallas_call(
        functools.partial(_outproj_kernel, n_parts=len(ys)),
        out_shape=jax.ShapeDtypeStruct((b, s, d), F32),
        grid=(b, s // tm),
        in_specs=in_specs,
        out_specs=pl.BlockSpec((1, tm, d), lambda bi, i: (bi, i, 0)),
        compiler_params=_params(2),
        name="outproj",
    )(*ys, *([w] * len(ys)), x, mod, g.reshape(1, d))


def _ffn_kernel(x_ref, xp_ref, xn_ref, mod_ref, gpre_ref, gpost_ref, wu_ref, cw_ref, wd_ref, o_ref,
                h_ref, u_ref, a_ref, *, tm):
    i = pl.program_id(1)
    mod = mod_ref[0]
    gpre = gpre_ref[...]

    def norm_mod(x):
        return _rms(x, gpre) * (1.0 + mod[4:5]) + mod[3:4]

    keep_prev = (i > 0).astype(F32)
    keep_next = (i < pl.num_programs(1) - 1).astype(F32)
    rows = tm + 2 * HALO
    half = tm // 2
    mid = HALO + half
    n_u = u_ref.shape[0]

    def up(c, r0=0, r1=rows):
        h = h_ref[r0:r1]
        for part in range(2):
            col = part * D_FF + c * FF_CHUNK
            u_ref[c % n_u, r0:r1, part * FF_CHUNK:(part + 1) * FF_CHUNK] = _dot(h, wu_ref[:, col:col + FF_CHUNK])

    h_ref[0:HALO] = (norm_mod(xp_ref[0]) * keep_prev).astype(BF16)
    h_ref[HALO:mid] = norm_mod(x_ref[0, :half]).astype(BF16)
    up(0, 0, mid)
    h_ref[mid:HALO + tm] = norm_mod(x_ref[0, half:]).astype(BF16)
    h_ref[HALO + tm:] = (norm_mod(xn_ref[0]) * keep_next).astype(BF16)
    up(0, mid, rows)

    def act(c):
        u = u_ref[c % n_u]
        cw = cw_ref[c]
        y = (pltpu.roll(u, 1, 0)[HALO:HALO + tm] * cw[0:1]
             + u[HALO:HALO + tm] * cw[1:2]
             + pltpu.roll(u, rows - 1, 0)[HALO:HALO + tm] * cw[2:3]
             + cw[3:4])
        a_ref[:, c * FF_CHUNK:(c + 1) * FF_CHUNK] = (_silu(y[:, :FF_CHUNK]) * y[:, FF_CHUNK:]).astype(BF16)

    for c in range(1, min(n_u - 1, N_FF_CHUNKS)):
        up(c)
    for c in range(N_FF_CHUNKS):
        if c + n_u - 1 < N_FF_CHUNKS:
            up(c + n_u - 1)
        act(c)
    accs = [_dot(a_ref[r0:r0 + half], wd_ref[...]) for r0 in (0, half)]
    for r0, acc in zip((0, half), accs):
        o_ref[0, r0:r0 + half] = x_ref[0, r0:r0 + half] + mod[5:6] * _rms(acc, gpost_ref[...])


def _ffn(x, mod, g_pre, g_post, wu, cw, wd, layer, tm):
    b, s, d = x.shape
    per_batch = mod.shape[0] > 1
    hb = tm // HALO
    n_halo = s // HALO
    return pl.pallas_call(
        functools.partial(_ffn_kernel, tm=tm),
        out_shape=jax.ShapeDtypeStruct((b, s, d), F32),
        grid=(b, s // tm),
        in_specs=[pl.BlockSpec((1, tm, d), lambda bi, i: (bi, i, 0)),
                  pl.BlockSpec((1, HALO, d), lambda bi, i: (bi, jnp.maximum(i * hb - 1, 0), 0)),
                  pl.BlockSpec((1, HALO, d), lambda bi, i: (bi, jnp.minimum((i + 1) * hb, n_halo - 1), 0)),
                  pl.BlockSpec((1, 6, d), (lambda bi, i: (bi, 0, 0)) if per_batch else (lambda bi, i: (0, 0, 0))),
                  _resident((1, d)), _resident((1, d)),
                  _layer_block(wu, layer), _layer_block(cw, layer), _layer_block(wd, layer)],
        out_specs=pl.BlockSpec((1, tm, d), lambda bi, i: (bi, i, 0)),
        scratch_shapes=[pltpu.VMEM((tm + 2 * HALO, d), BF16),
                        pltpu.VMEM((3, tm + 2 * HALO, 2 * FF_CHUNK), F32),
                        pltpu.VMEM((tm, D_FF), BF16)],
        compiler_params=_params(2),
        name="ffn",
    )(x, x, x, mod, g_pre.reshape(1, d), g_post.reshape(1, d), wu, cw, wd)


def _na_kernel(q_ref, k_ref, v_ref, kc_ref, vc_ref, bias_ref, o_ref, *, rows):
    kc = kc_ref[0]
    vc = vc_ref[0]
    win = NA_WIN_H * GRID_W
    mask_a = _head_mask(GRID_W, 0)
    vc_ext = jnp.concatenate([vc, jnp.ones_like(vc)], axis=1)
    ones_w = jnp.ones((win, LANES), BF16)
    group_rows = math.gcd(NA_GROUP, rows)

    def group(gi, carry):
        starts, scores, probs = [], [], []
        for i in range(group_rows):
            r = gi * group_rows + i
            r0 = jnp.clip(r - NA_WIN_H // 2, 0, rows - NA_WIN_H)
            q0 = pl.multiple_of(r * GRID_W, GRID_W)
            k0 = pl.multiple_of(r0 * GRID_W, GRID_W)
            q = q_ref[0, pl.ds(q0, GRID_W), :]
            zero = jnp.zeros_like(q)
            q2 = jnp.concatenate([jnp.where(mask_a, q, zero), jnp.where(mask_a, zero, q)], axis=0)
            d0 = r0 - r + NA_WIN_H - 1
            bias = jnp.concatenate(
                [jnp.concatenate([bias_ref[0, hh, d0 + 2 * a] for a in range(NA_WIN_H // 2)], axis=1)
                 for hh in range(2)], axis=0)
            s = _dot_nt(q2, k_ref[0, pl.ds(k0, win), :]) + bias
            sc = _dot_nt(q2, kc)
            starts.append((q0, k0))
            scores.append((s, sc))
        for s, sc in scores:
            m = jnp.maximum(jnp.max(s, axis=-1, keepdims=True), jnp.max(sc, axis=-1, keepdims=True))
            probs.append((jnp.exp2(s - m).astype(BF16), jnp.exp2(sc - m).astype(BF16)))
        for (q0, k0), (p, pc) in zip(starts, probs):
            v_ext = jnp.concatenate([v_ref[0, pl.ds(k0, win), :], ones_w], axis=1)
            o_ext = _dot(p, v_ext) + _dot(pc, vc_ext)
            o2 = o_ext[:, :LANES] / o_ext[:, LANES:]
            o_ref[0, pl.ds(q0, GRID_W), :] = jnp.where(mask_a, o2[:GRID_W], o2[GRID_W:]).astype(BF16)
        return carry

    lax.fori_loop(0, rows // group_rows, group, 0)


def _na(pl_, pc_, bias):
    b, s, _ = pl_.shape
    n_ctx = pc_.shape[1]
    n_pairs = NA_HEADS // 2
    return pl.pallas_call(
        functools.partial(_na_kernel, rows=s // GRID_W),
        out_shape=jax.ShapeDtypeStruct((b, s, NA_WIDTH), BF16),
        grid=(b, n_pairs),
        in_specs=[pl.BlockSpec((1, s, LANES), lambda bi, p: (bi, 0, AB_QA + p)),
                  pl.BlockSpec((1, s, LANES), lambda bi, p: (bi, 0, AB_KA + p)),
                  pl.BlockSpec((1, s, LANES), lambda bi, p: (bi, 0, AB_VA + p)),
                  pl.BlockSpec((1, n_ctx, LANES), lambda bi, p: (bi, 0, AB_KA + p)),
                  pl.BlockSpec((1, n_ctx, LANES), lambda bi, p: (bi, 0, AB_VA + p)),
                  pl.BlockSpec((1,) + bias.shape[1:], lambda bi, p: (p, 0, 0, 0, 0))],
        out_specs=pl.BlockSpec((1, s, LANES), lambda bi, p: (bi, 0, p)),
        compiler_params=_params(2),
        name="na",
    )(pl_, pl_, pl_, pc_, pc_, bias)


def _na_bias(rpb):
    h = rpb.shape[0]
    cidx = np.arange(GRID_W)
    c_start = np.clip(cidx - NA_WIN_W // 2, 0, GRID_W - NA_WIN_W)
    col_ok = (cidx[None, :] >= c_start[:, None]) & (cidx[None, :] < c_start[:, None] + NA_WIN_W)
    lo, hi = GRID_W - NA_WIN_W, GRID_W - NA_WIN_W
    rp = rpb.astype(F32) * LOG2E
    ext = jnp.concatenate([jnp.repeat(rp[..., :1], lo, axis=-1), rp, jnp.repeat(rp[..., -1:], hi, axis=-1)], axis=-1)
    bm = jnp.stack([ext[..., GRID_W - 1 - q:2 * GRID_W - 1 - q] for q in range(GRID_W)], axis=2)
    bm = jnp.where(col_ok[None, None], bm, NEG_INF)
    pairs = jnp.concatenate([bm[:, :-1], bm[:, 1:]], axis=-1)
    return pairs.reshape((h // 2, 2) + pairs.shape[1:])


def _ctx_attn_kernel(*refs, n_q_tiles, group, has_sink):
    if has_sink:
        q_ref, k_ref, v_ref, sink_ref, o_ref = refs
    else:
        q_ref, k_ref, v_ref, o_ref = refs
    n = q_ref.shape[1]
    mask_a = _head_mask(n, 0)
    scores, probs = [], []
    for t in range(n_q_tiles):
        q = q_ref[0, :, t * LANES:(t + 1) * LANES]
        zero = jnp.zeros_like(q)
        q2 = jnp.concatenate([jnp.where(mask_a, q, zero), jnp.where(mask_a, zero, q)], axis=0)
        kt = t // group
        scores.append(_dot_nt(q2, k_ref[0, :, kt * LANES:(kt + 1) * LANES]))
    for t, s in enumerate(scores):
        m = jnp.max(s, axis=-1, keepdims=True)
        if has_sink:
            sk = jnp.concatenate([jnp.broadcast_to(sink_ref[t, h:h + 1, 0:1], (n, 1)) for h in range(2)], axis=0)
            m = jnp.maximum(m, sk)
        p = jnp.exp2(s - m)
        den = jnp.sum(p, axis=-1, keepdims=True)
        if has_sink:
            den = den + jnp.exp2(sk - m)
        probs.append((p.astype(BF16), den))
    for t, (p, den) in enumerate(probs):
        kt = t // group
        o2 = _dot(p, v_ref[0, :, kt * LANES:(kt + 1) * LANES]) / den
        o_ref[0, :, t * LANES:(t + 1) * LANES] = jnp.where(mask_a, o2[:n], o2[n:]).astype(BF16)


def _ctx_attn(pc_, q_off, k_off, v_off, n_q_tiles, group, sink=None):
    b, n, _ = pc_.shape
    n_k_tiles = n_q_tiles // group
    qw, kw = n_q_tiles * LANES, n_k_tiles * LANES
    assert q_off % n_q_tiles == 0 and k_off % n_k_tiles == 0 and v_off % n_k_tiles == 0
    in_specs = [pl.BlockSpec((1, n, qw), lambda bi: (bi, 0, q_off // n_q_tiles)),
                pl.BlockSpec((1, n, kw), lambda bi: (bi, 0, k_off // n_k_tiles)),
                pl.BlockSpec((1, n, kw), lambda bi: (bi, 0, v_off // n_k_tiles))]
    args = [pc_, pc_, pc_]
    if sink is not None:
        in_specs.append(_resident(sink.shape))
        args.append(sink)
    return pl.pallas_call(
        functools.partial(_ctx_attn_kernel, n_q_tiles=n_q_tiles, group=group, has_sink=sink is not None),
        out_shape=jax.ShapeDtypeStruct((b, n, qw), BF16),
        grid=(b,),
        in_specs=in_specs,
        out_specs=pl.BlockSpec((1, n, qw), lambda bi: (bi, 0, 0)),
        compiler_params=_params(1),
        name="ctx_attn",
    )(*args)


def _gla_kernel(*refs, n_lat_blocks, need_ctx):
    (ql, kl, vl, zl, rl, qc, kc, vc, zc, rc, w2f, w2b, baf, bab, g_ref) = refs[:15]
    n_out = 2 if need_ctx else 1
    yl_ref = refs[15]
    yc_ref = refs[16] if need_ctx else None
    qt_s, kd_s, dec_s, vt_s, o_s = refs[15 + n_out:]
    blk, ck = GLA_BLOCK, GLA_CHUNK
    cpb = blk // ck
    shift = ck.bit_length() - 1
    ri = lax.broadcasted_iota(jnp.int32, (blk, blk), 0)
    ci = lax.broadcasted_iota(jnp.int32, (blk, blk), 1)
    same = lax.shift_right_logical(ri, shift) == lax.shift_right_logical(ci, shift)
    att_mask = [same & (ci <= ri), same & (ci >= ri)]
    as_bf16 = lambda m: jnp.where(m, 1.0, 0.0).astype(BF16)
    cum = [jnp.concatenate([as_bf16(att_mask[d]), as_bf16(same)], axis=0) for d in range(2)]
    head = [_head_mask(blk, 0), _head_mask(blk, 1)]
    head_ck = [_head_mask(ck, 0), _head_mask(ck, 1)]
    tok_lo = lax.broadcasted_iota(jnp.int32, (2 * GLA_DV, LANES), 1) < ck
    w2 = jnp.concatenate([w2f[...], w2b[...]], axis=1)
    ba = jnp.concatenate([baf[...], bab[...]], axis=1)
    g = g_ref[...]
    lat4, ctx4 = (ql, kl, vl, zl), (qc, kc, vc, zc)

    def phase1(blocks):
        gates = []
        for (_, _, _, z_ref), t_in, _ in blocks:
            x = _dot(z_ref[0, pl.ds(t_in, blk), :], w2) + ba
            la = (jnp.minimum(x, 0.0) - jnp.log(1.0 + jnp.exp(-jnp.abs(x)))) * (1.0 / GLA_TAU)
            hi = la.astype(BF16)
            gates.append((hi, (la - hi.astype(F32)).astype(BF16)))
        cums = []
        for hi, lo in gates:
            per_dir = []
            for d in range(2):
                c2 = _dot(cum[d], jnp.concatenate([hi[:, d * LANES:(d + 1) * LANES],
                                                   lo[:, d * LANES:(d + 1) * LANES]], axis=1))
                per_dir.append((c2[:blk, :LANES] + c2[:blk, LANES:], c2[blk:, :LANES] + c2[blk:, LANES:]))
            cums.append(per_dir)
        scores = []
        for ((q_ref, k_ref, _, _), t_in, bo), per_dir in zip(blocks, cums):
            q = q_ref[0, pl.ds(t_in, blk), :].astype(F32)
            k = k_ref[0, pl.ds(t_in, blk), :].astype(F32)
            t_out = pl.multiple_of(bo * blk, blk)
            items = []
            for d, (b, tot) in enumerate(per_dir):
                qt = q * jnp.exp(b)
                kt = (k * jnp.exp(-b)).astype(BF16)
                qt_s[d, pl.ds(t_out, blk), :] = qt.astype(BF16)
                kd_s[d, pl.ds(t_out, blk), :] = (k * jnp.exp(tot - b)).astype(BF16)
                for j in range(cpb):
                    dec_s[d, pl.ds(bo * cpb + j, 1), :] = jnp.exp(tot[j * ck:j * ck + 1])
                for h in range(2):
                    items.append((d, h, _dot_nt(jnp.where(head[h], qt, 0.0).astype(BF16), kt)))
            scores.append(items)
        for ((_, _, v_ref, _), t_in, bo), items in zip(blocks, scores):
            v = v_ref[0, pl.ds(t_in, blk), :]
            oi = [None, None]
            for d, h, s in items:
                c = _dot(jnp.where(att_mask[d], s, 0.0).astype(BF16), v[:, h * GLA_DV:(h + 1) * GLA_DV])
                oi[h] = c if oi[h] is None else oi[h] + c
            o_s[pl.ds(pl.multiple_of(bo * blk, blk), blk), :] = jnp.concatenate(oi, axis=1)
            vt_s[bo] = v.astype(F32).T.astype(BF16)

    orders = [list(range(cpb)), list(range(cpb - 1, -1, -1))]

    def scan_pairs(pairs, sts):
        upds = []
        for bos in pairs:
            upd = [{}, {}]
            for d in range(2):
                vt = vt_s[bos[d]]
                kd = kd_s[d, pl.ds(pl.multiple_of(bos[d] * blk, blk), blk), :]
                for c in orders[d]:
                    half = c // 2
                    vt_c = vt[:, half * LANES:(half + 1) * LANES]
                    vt_c = jnp.where(tok_lo if c % 2 == 0 else ~tok_lo, vt_c, jnp.zeros_like(vt_c))
                    upd[d][c] = _dot(vt_c, kd[half * LANES:(half + 1) * LANES])
            upds.append(upd)
        sts = list(sts)
        for bos, upd in zip(pairs, upds):
            for step in range(cpb):
                for d in range(2):
                    c = orders[d][step]
                    tc = pl.multiple_of(bos[d] * blk + c * ck, ck)
                    qt = qt_s[d, pl.ds(tc, ck), :]
                    stb = sts[d].astype(BF16)
                    outs = [_dot_nt(jnp.where(head_ck[h], qt, jnp.zeros_like(qt)),
                                    stb[h * GLA_DV:(h + 1) * GLA_DV]) for h in range(2)]
                    o_s[pl.ds(tc, ck), :] += jnp.concatenate(outs, axis=1)
                    sts[d] = dec_s[d, pl.ds(bos[d] * cpb + c, 1), :] * sts[d] + upd[d][c]
        return tuple(sts)

    def finalize(bo, r_ref, t_in, y_ref):
        o = o_s[pl.ds(pl.multiple_of(bo * blk, blk), blk), :]
        r = r_ref[0, pl.ds(t_in, blk), :].astype(F32)
        on = jnp.concatenate(
            [o[:, h * GLA_DV:(h + 1) * GLA_DV]
             * lax.rsqrt(jnp.mean(o[:, h * GLA_DV:(h + 1) * GLA_DV] ** 2, axis=-1, keepdims=True) + EPS)
             for h in range(2)], axis=1)
        y_ref[0, pl.ds(t_in, blk), :] = (on * g * _silu(r)).astype(BF16)

    phase1([(ctx4, 0, 0)])

    p1 = math.gcd(GLA_P1_BLOCKS, n_lat_blocks)

    def phase1_lat(i, carry):
        phase1([(lat4, pl.multiple_of((p1 * i + j) * blk, blk), 1 + p1 * i + j) for j in range(p1)])
        return carry

    lax.fori_loop(0, n_lat_blocks // p1, phase1_lat, 0)

    zero = jnp.zeros((2 * GLA_DV, 2 * GLA_DK), F32)
    sts = scan_pairs([(0, 0)], (zero, zero))
    lax.fori_loop(
        0, n_lat_blocks // 2,
        lambda i, s: scan_pairs([(1 + 2 * i + j, n_lat_blocks - 2 * i - j) for j in range(2)], s), sts)

    if need_ctx:
        finalize(0, rc, 0, yc_ref)

    def finalize_lat(i, carry):
        for j in range(2):
            finalize(1 + 2 * i + j, rl, pl.multiple_of((2 * i + j) * blk, blk), yl_ref)
        return carry

    lax.fori_loop(0, n_lat_blocks // 2, finalize_lat, 0)


def _gla(pl_, pc_, w2, ba, g, need_ctx):
    b, s, _ = pl_.shape
    n_c = pc_.shape[1]
    n_pairs = GLA_HEADS // 2
    w = 2 * GLA_DV

    def stream(n):
        return [pl.BlockSpec((1, n, LANES), lambda bi, p: (bi, 0, AB_QB + p)),
                pl.BlockSpec((1, n, LANES), lambda bi, p: (bi, 0, AB_KB + p)),
                pl.BlockSpec((1, n, w), lambda bi, p: (bi, 0, AB_VB + p)),
                pl.BlockSpec((1, n, LANES), lambda bi, p: (bi, 0, AB_Z)),
                pl.BlockSpec((1, n, w), lambda bi, p: (bi, 0, AB_RB + p))]

    in_specs = stream(s) + stream(n_c) + [
        pl.BlockSpec((LANES, LANES), lambda bi, p: (0, p)),
        pl.BlockSpec((LANES, LANES), lambda bi, p: (0, n_pairs + p)),
        pl.BlockSpec((1, LANES), lambda bi, p: (0, p)),
        pl.BlockSpec((1, LANES), lambda bi, p: (0, n_pairs + p)),
        pl.BlockSpec((1, w), lambda bi, p: (0, p))]
    out_shape = [jax.ShapeDtypeStruct((b, s, GLA_VW), BF16)]
    out_specs = [pl.BlockSpec((1, s, w), lambda bi, p: (bi, 0, p))]
    if need_ctx:
        out_shape.append(jax.ShapeDtypeStruct((b, n_c, GLA_VW), BF16))
        out_specs.append(pl.BlockSpec((1, n_c, w), lambda bi, p: (bi, 0, p)))
    assert n_c == GLA_BLOCK and s % (2 * GLA_BLOCK) == 0
    n_lat_blocks = s // GLA_BLOCK
    t_all = n_c + s
    n_chunks = -(-(t_all // GLA_CHUNK) // 8) * 8
    scratch = [pltpu.VMEM((2, t_all, LANES), BF16),
               pltpu.VMEM((2, t_all, LANES), BF16),
               pltpu.VMEM((2, n_chunks, LANES), F32),
               pltpu.VMEM((1 + n_lat_blocks, w, GLA_BLOCK), BF16),
               pltpu.VMEM((t_all, w), F32)]
    res = pl.pallas_call(
        functools.partial(_gla_kernel, n_lat_blocks=n_lat_blocks, need_ctx=need_ctx),
        out_shape=out_shape,
        grid=(b, n_pairs),
        in_specs=in_specs,
        out_specs=out_specs,
        scratch_shapes=scratch,
        compiler_params=_params(2),
        name="gla",
    )(*([pl_] * 5), *([pc_] * 5), w2, w2, ba, ba, g.reshape(1, GLA_VW))
    return (res[0], res[1]) if need_ctx else (res[0], None)


def _gla_gate_weights(wa2, ba):
    w = jnp.zeros((LANES, 2 * GLA_KW), F32)
    w = w.at[0:GLA_RANK, 0:GLA_KW].set(wa2[0])
    w = w.at[GLA_RANK:2 * GLA_RANK, GLA_KW:].set(wa2[1])
    return w.astype(BF16), ba.reshape(1, 2 * GLA_KW).astype(F32)


def _swa_kernel(q_ref, kp_ref, kn_ref, kx_ref, vp_ref, vn_ref, vx_ref, kc_ref, vc_ref, sink_ref, o_ref):
    n = pl.program_id(1)
    blk = SWA_BLOCK
    n_c = kc_ref.shape[1]
    n_keys = 3 * blk + n_c
    k_lat = jnp.concatenate([kp_ref[0], kn_ref[0], kx_ref[0]], axis=0)
    v_lat = jnp.concatenate([vp_ref[0], vn_ref[0], vx_ref[0]], axis=0)
    kj = lax.broadcasted_iota(jnp.int32, (blk, blk), 0)
    qi = lax.broadcasted_iota(jnp.int32, (blk, blk), 1)
    in_prev = kj >= qi
    in_next = kj <= qi
    first, last = n == 0, n == pl.num_programs(1) - 1
    n_pairs = SWA_KV // 2
    n_units = SWA_QB * n_pairs * SWA_SPLIT
    unit_tiles = SWA_GROUP // SWA_SPLIT
    rows = unit_tiles * blk
    q_head = [_head_mask(rows, 0), _head_mask(rows, 1)]
    out_a = lax.broadcasted_iota(jnp.int32, (LANES, rows), 0) < HEAD_DIM

    def block_inputs(j):
        k = jnp.concatenate([k_lat[j * blk:(j + 3) * blk], kc_ref[0]], axis=0)
        v = jnp.concatenate([v_lat[j * blk:(j + 3) * blk], vc_ref[0]], axis=0)
        has_prev = jnp.logical_not(first) if j == 0 else True
        has_next = jnp.logical_not(last) if j == SWA_QB - 1 else True
        return k, v, tuple(jnp.concatenate([m] * unit_tiles, axis=1) for m in (in_prev & has_prev, in_next & has_next))

    blocks = [block_inputs(j) for j in range(SWA_QB)]
    vts = {}

    def unit_inputs(u):
        j, pair, part = u // (n_pairs * SWA_SPLIT), (u // SWA_SPLIT) % n_pairs, u % SWA_SPLIT
        tiles = [pair * SWA_GROUP + part * unit_tiles + g for g in range(unit_tiles)]
        q = jnp.concatenate([q_ref[0, j * blk:(j + 1) * blk, t * LANES:(t + 1) * LANES] for t in tiles], axis=0)
        if (j, pair) not in vts:
            vts[j, pair] = blocks[j][1][:, pair * LANES:(pair + 1) * LANES].astype(F32).T.astype(BF16)
        return j, pair, tiles, q, vts[j, pair]

    def scores(u, h, unit):
        j, pair, tiles, q, _ = unit
        k, _, (ok_prev, ok_next) = blocks[j]
        qm = jnp.where(q_head[h], q, jnp.zeros_like(q))
        st = _dot_nt(k[:, pair * LANES:(pair + 1) * LANES], qm)
        st = jnp.concatenate([jnp.where(ok_prev, st[:blk], NEG_INF), st[blk:2 * blk],
                              jnp.where(ok_next, st[2 * blk:3 * blk], NEG_INF), st[3 * blk:]], axis=0)
        sk = jnp.concatenate([jnp.broadcast_to(sink_ref[t, h:h + 1, 0:1], (1, blk)) for t in tiles], axis=1)
        return st, sk

    def finish(unit, st, sk):
        m = jnp.maximum(jnp.max(st, axis=0, keepdims=True), sk)
        p = jnp.exp2(st - m)
        den = jnp.sum(p, axis=0, keepdims=True) + jnp.exp2(sk - m)
        return _dot(unit[4], p.astype(BF16)) / den

    def store(unit, outs):
        j, _, tiles, _, _ = unit
        o = jnp.where(out_a, outs[0], outs[1]).T.astype(BF16)
        for g, t in enumerate(tiles):
            o_ref[0, j * blk:(j + 1) * blk, t * LANES:(t + 1) * LANES] = o[g * blk:(g + 1) * blk]

    units = {0: unit_inputs(0)}
    pending = scores(0, 0, units[0])
    outs = []
    for c in range(2 * n_units):
        u, h = c // 2, c % 2
        nxt = None
        if c + 1 < 2 * n_units:
            un, hn = (c + 1) // 2, (c + 1) % 2
            if un not in units:
                units[un] = unit_inputs(un)
            nxt = scores(un, hn, units[un])
        outs.append(finish(units[u], *pending))
        if h == 1:
            store(units[u], outs)
            outs = []
        pending = nxt


def _swa(pl_, pc_, sink):
    b, s, _ = pl_.shape
    n_c = pc_.shape[1]
    nb = s // SWA_BLOCK
    step = SWA_QB * SWA_BLOCK
    assert s % step == 0
    qw = SWA_HEADS * HEAD_DIM
    kw = SWA_KV * HEAD_DIM
    k_blk, v_blk = (C_K * LANES) // kw, (C_V * LANES) // kw

    def kv(off):
        return [pl.BlockSpec((1, SWA_BLOCK, kw), lambda bi, n: (bi, jnp.maximum(n * SWA_QB - 1, 0), off)),
                pl.BlockSpec((1, step, kw), lambda bi, n: (bi, n, off)),
                pl.BlockSpec((1, SWA_BLOCK, kw), lambda bi, n: (bi, jnp.minimum((n + 1) * SWA_QB, nb - 1), off))]

    in_specs = ([pl.BlockSpec((1, step, qw), lambda bi, n: (bi, n, 0))] + kv(k_blk) + kv(v_blk)
                + [pl.BlockSpec((1, n_c, kw), lambda bi, n: (bi, 0, k_blk)),
                   pl.BlockSpec((1, n_c, kw), lambda bi, n: (bi, 0, v_blk)),
                   _resident(sink.shape)])
    return pl.pallas_call(
        _swa_kernel,
        out_shape=jax.ShapeDtypeStruct((b, s, qw), BF16),
        grid=(b, s // step),
        in_specs=in_specs,
        out_specs=pl.BlockSpec((1, step, qw), lambda bi, n: (bi, n, 0)),
        compiler_params=_params(2),
        name="swa",
    )(pl_, pl_, pl_, pl_, pl_, pl_, pl_, pc_, pc_, sink)


def _swa_pair_heads(a, axis):
    shape = a.shape
    a = a.reshape(shape[:axis] + (SWA_KV // 2, 2, SWA_GROUP, -1) + shape[axis + 1:])
    return jnp.swapaxes(a, axis + 1, axis + 2).reshape(shape)


def _rope_tables(n):
    t = jnp.arange(n)
    row = (t // GRID_W).astype(F32)
    col = (t % GRID_W).astype(F32)
    half = HEAD_DIM // 2
    inv = 1.0 / (ROPE_BASE ** (jnp.arange(0, half, 2, dtype=F32) / half))
    ang = jnp.concatenate([row[:, None] * inv, col[:, None] * inv], axis=-1)
    cos = jnp.repeat(jnp.cos(ang), 2, axis=1)
    sin = jnp.repeat(jnp.sin(ang), 2, axis=1) * jnp.tile(jnp.array([-1.0, 1.0], F32), half)
    reps = ROPE_TILE // HEAD_DIM
    return jnp.tile(cos, (1, reps)), jnp.tile(sin, (1, reps))


def kernel(x, c, ctx, c_ctx, w_mod, b_mod, g_mix_pre, g_mix_post, g_ffn_pre, g_ffn_post, w_out, w_up, conv_w,
           conv_b, w_down, w_in_ab, na_rpb, gla_wa2, gla_ba, gla_g, w_in_c, swa_sink):
    bsz, seq, d = x.shape
    n_ctx = ctx.shape[1]
    depth = w_mod.shape[0]
    tm_l = 512 if seq % 512 == 0 else 256
    tm_c = n_ctx
    tm_o = 1024 if seq % 1024 == 0 else tm_l

    c_all = jnp.zeros((8, d), F32).at[:bsz].set(c).at[bsz].set(c_ctx)
    mods = _modulation(c_all, w_mod, b_mod).reshape(depth, 8, 6, d)
    rope = _rope_tables(seq)

    col_scale_ab = np.ones((IN_AB,), np.float32)
    col_scale_ab[0:NA_WIDTH] = ATTN_SCALE
    col_scale_ab[3 * NA_WIDTH:3 * NA_WIDTH + GLA_KW] = GLA_DK ** -0.5

    n_q = SWA_HEADS * HEAD_DIM
    w_ab = jnp.pad(jnp.swapaxes(w_in_ab, 1, 2) * col_scale_ab[:, None],
                   ((0, 0), (0, IN_AB_PAD - IN_AB), (0, 0))).astype(BF16)
    w_c = jnp.concatenate([_swa_pair_heads(w_in_c[:, :, :n_q], 2) * ATTN_SCALE, w_in_c[:, :, n_q:]],
                          axis=2).astype(BF16)
    w_o = w_out.astype(BF16)
    head_rows = np.arange(n_q).reshape(SWA_KV // 2, 2, SWA_GROUP, HEAD_DIM).swapaxes(1, 2).reshape(-1)
    w_o_swa = jnp.take(w_o[1::2], head_rows, axis=1)
    wu = w_up.astype(BF16)
    wd = w_down.astype(BF16)
    cwb = jnp.concatenate([conv_w, conv_b[:, None]], axis=1)
    cw = jnp.concatenate([cwb[:, :, :D_FF].reshape(depth, 4, N_FF_CHUNKS, FF_CHUNK),
                          cwb[:, :, D_FF:].reshape(depth, 4, N_FF_CHUNKS, FF_CHUNK)], axis=3).transpose(0, 2, 1, 3)

    xl, xc = x, ctx
    for i in range(depth):
        need_ctx = i < depth - 1
        j = i // 2
        mod_l = mods[i, :bsz]
        mod_c = mods[i, bsz:bsz + 1]
        if i % 2 == 0:
            p_l = _proj(xl, mod_l, g_mix_pre[i], w_ab, j, tm_l, 640, w_transposed=True)
            p_c = _proj(xc, mod_c, g_mix_pre[i], w_ab, j, tm_c, 640, w_transposed=True)
            w2, ba = _gla_gate_weights(gla_wa2[j], gla_ba[j])
            y_na = _na(p_l, p_c, _na_bias(na_rpb[j]))
            y_gla, yc_gla = _gla(p_l, p_c, w2, ba, gla_g[j], need_ctx)
            ys_l = [y_na, y_gla]
            w_post, l_post = w_o, i
            if need_ctx:
                ys_c = [_ctx_attn(p_c, AB_QA, AB_KA, AB_VA, NA_HEADS // 2, 1), yc_gla]
        else:
            p_l = _proj(xl, mod_l, g_mix_pre[i], w_c, j, tm_l, ROPE_TILE, rope, (C_V * LANES) // ROPE_TILE)
            p_c = _proj(xc, mod_c, g_mix_pre[i], w_c, j, tm_c, ROPE_TILE)
            sink = jnp.broadcast_to(_swa_pair_heads(swa_sink[j] * LOG2E, 0).reshape(-1, 2, 1),
                                    (SWA_HEADS // 2, 2, LANES)).astype(F32)
            ys_l = [_swa(p_l, p_c, sink)]
            w_post, l_post = w_o_swa, j
            if need_ctx:
                ys_c = [_ctx_attn(p_c, C_Q, C_K, C_V, SWA_HEADS // 2, SWA_GROUP, sink)]

        xl = _outproj(ys_l, w_post, l_post, xl, mod_l, g_mix_post[i], tm_o)
        xl = _ffn(xl, mod_l, g_ffn_pre[i], g_ffn_post[i], wu, cw, wd, i, tm_l)
        if need_ctx:
            xc = _outproj(ys_c, w_post, l_post, xc, mod_c, g_mix_post[i], tm_c)
            xc = _ffn(xc, mod_c, g_ffn_pre[i], g_ffn_post[i], wu, cw, wd, i, tm_c)
    return xl
```
